```python
import math
import jax, jax.numpy as jnp
from jax import lax
import numpy as np

D_MODEL = 1024
BATCH = 8
SEQ = 2048
DEPTH = 1
DEC_BATCH = 128
DEC_SEQ = 1
PAST_LEN = 16384
PAGE_SIZE = 128

D_MIX = D_MODEL
D_GDN = D_MIX // 2
D_SGU = D_MIX - D_GDN
GDN_HEADS = 4
GDN_DK = D_GDN // GDN_HEADS
GDN_DV = D_GDN // GDN_HEADS
GDN_CHUNK = 64
CONV_W = 4
D_QKV = 3 * D_GDN
SGU_HEADS = 4
SGU_DH = D_SGU // SGU_HEADS
SGU_CHUNK = 128
D_FF = 4 * D_MODEL
D_PLE = 256
D_IN = D_QKV + D_GDN + 2 * GDN_HEADS + 2 * D_SGU
EPS = 1e-6

kernel_name = 'hybrid_gdn_chunkmlp_decode_step'


def rms_norm(x, gain):
    xf = x.astype(jnp.float32)
    y = xf * lax.rsqrt(jnp.mean(xf * xf, axis=-1, keepdims=True) + EPS)
    return (y * gain.astype(jnp.float32)).astype(x.dtype)


def layer_norm(x, gain, bias):
    xf = x.astype(jnp.float32)
    mu = jnp.mean(xf, axis=-1, keepdims=True)
    var = jnp.mean(jnp.square(xf - mu), axis=-1, keepdims=True)
    y = (xf - mu) * lax.rsqrt(var + EPS)
    return (y * gain.astype(jnp.float32) + bias.astype(jnp.float32)).astype(x.dtype)


def l2_normalize(x):
    return x * lax.rsqrt(jnp.sum(x * x, axis=-1, keepdims=True) + 1e-6)


def causal_short_conv(x, hist, w):
    l = x.shape[1]
    xx = jnp.concatenate([hist.astype(x.dtype), x], axis=1)
    y = sum(xx[:, j:j + l] * w[j] for j in range(CONV_W))
    return jax.nn.silu(y), xx[:, l:]


def gated_delta_chunked(q, k, v, g, beta, s0):
    b, l, h, dk = q.shape
    dv = v.shape[-1]
    c = GDN_CHUNK
    pad = (-l) % c
    n = (l + pad) // c

    def blocks4(t):
        t = jnp.pad(t, ((0, 0), (0, pad), (0, 0), (0, 0))).reshape(b, n, c, h, t.shape[-1])
        return jnp.transpose(t, (1, 0, 3, 2, 4))

    def blocks3(t):
        t = jnp.pad(t, ((0, 0), (0, pad), (0, 0))).reshape(b, n, c, h)
        return jnp.transpose(t, (1, 0, 3, 2))

    q = blocks4(q) * (dk ** -0.5)
    k = blocks4(k)
    v = blocks4(v)
    beta = blocks3(beta)
    gc = jnp.cumsum(blocks3(g), axis=-1)
    incl = jnp.tril(jnp.ones((c, c), bool))
    strict = jnp.tril(jnp.ones((c, c), bool), -1)
    decay = jnp.exp(jnp.where(incl, gc[..., :, None] - gc[..., None, :], -jnp.inf))
    kb = k * beta[..., None]
    a_mat = jnp.where(strict, jnp.einsum('nbhcd,nbhsd->nbhcs', kb, k) * decay, 0.0)
    lhs = a_mat + jnp.eye(c, dtype=a_mat.dtype)
    rhs = jnp.concatenate([v * beta[..., None], kb * jnp.exp(gc)[..., None]], axis=-1)
    sol = lax.linalg.triangular_solve(lhs, rhs, left_side=True, lower=True, unit_diagonal=True)
    u_wy = sol[..., :dv]
    w_wy = sol[..., dv:]
    attn = jnp.where(incl, jnp.einsum('nbhcd,nbhsd->nbhcs', q, k) * decay, 0.0)

    def step(s, inp):
        qi, ki, ui, wi, gi, ai = inp
        v_new = ui - jnp.einsum('bhcd,bhde->bhce', wi, s)
        o = (jnp.einsum('bhcd,bhde->bhce', qi * jnp.exp(gi)[..., None], s)
             + jnp.einsum('bhcs,bhse->bhce', ai, v_new))
        g_last = gi[..., -1]
        s = (s * jnp.exp(g_last)[..., None, None]
             + jnp.einsum('bhcd,bhce->bhde', ki * jnp.exp(g_last[..., None] - gi)[..., None], v_new))
        return s, o

    s_fin, o = lax.scan(step, s0, (q, k, u_wy, w_wy, gc, attn))
    o = jnp.transpose(o, (1, 0, 3, 2, 4)).reshape(b, n * c, h, dv)[:, :l]
    return o, s_fin


def chunk_spatial_gate(u, v, w_s, b_s):
    b, l, h, dh = u.shape
    c = SGU_CHUNK
    pad = (-l) % c
    n = (l + pad) // c
    vp = jnp.pad(v, ((0, 0), (0, pad), (0, 0), (0, 0))).reshape(b, n, c, h, dh)
    w = jnp.where(jnp.tril(jnp.ones((c, c), bool))[None], w_s, 0.0)
    mix = jnp.einsum('hts,bnshd->bnthd', w, vp) + jnp.transpose(b_s)[None, None, :, :, None]
    return u * mix.reshape(b, n * c, h, dh)[:, :l]


def hybrid_layer(x, p, conv_hist, s0, g_mix, w_in, w_conv, a_log, dt_bias, gdn_norm,
                 sgu_ln_g, sgu_ln_b, w_s, b_s, w_out, g_ff, w_up, w_down, g_ple, w_ple, w_ple_gate):
    bsz, l, _ = x.shape
    f32 = jnp.float32
    h = rms_norm(x, g_mix)
    proj = h @ w_in
    o0 = D_QKV
    o1 = o0 + D_GDN
    o2 = o1 + GDN_HEADS
    o3 = o2 + GDN_HEADS
    qkv, new_hist = causal_short_conv(proj[..., :o0], conv_hist, w_conv)
    qkv = qkv.astype(f32).reshape(bsz, l, 3, GDN_HEADS, GDN_DK)
    q = l2_normalize(qkv[:, :, 0])
    k = l2_normalize(qkv[:, :, 1])
    v = qkv[:, :, 2]
    z = proj[..., o0:o1].astype(f32).reshape(bsz, l, GDN_HEADS, GDN_DV)
    beta = jax.nn.sigmoid(proj[..., o1:o2].astype(f32))
    g = -jnp.exp(a_log.astype(f32)) * jax.nn.softplus(proj[..., o2:o3].astype(f32) + dt_bias.astype(f32))
    o, s_new = gated_delta_chunked(q, k, v, g, beta, s0.astype(f32))
    o = (rms_norm(o, gdn_norm) * jax.nn.silu(z)).astype(x.dtype).reshape(bsz, l, D_GDN)
    uv = jax.nn.gelu(proj[..., o3:])
    u = uv[..., :D_SGU]
    vg = layer_norm(uv[..., D_SGU:], sgu_ln_g, sgu_ln_b)
    sg = chunk_spatial_gate(u.reshape(bsz, l, SGU_HEADS, SGU_DH), vg.reshape(bsz, l, SGU_HEADS, SGU_DH),
                            w_s, b_s).reshape(bsz, l, D_SGU)
    x = x + jnp.concatenate([o, sg.astype(x.dtype)], axis=-1) @ w_out
    x = x + jnp.square(jax.nn.relu(rms_norm(x, g_ff) @ w_up)) @ w_down
    x = x + (p @ w_ple) * jax.nn.sigmoid(rms_norm(x, g_ple) @ w_ple_gate)
    return x, new_hist, s_new.astype(x.dtype), vg


def setup_inputs(seed: int = 0) -> dict:
    key = jax.random.key(seed)
    ks = jax.random.split(key, 32)
    nrm = lambda k, shape, scale: jax.random.normal(k, shape, jnp.float32) * scale
    gain = lambda k, shape: 1.0 + 0.02 * jax.random.normal(k, shape, jnp.float32)
    dt = jnp.exp(jax.random.uniform(ks[9], (DEPTH, GDN_HEADS), jnp.float32,
                                    minval=math.log(1e-3), maxval=math.log(1e-1)))
    return {
        'x_prompt': nrm(ks[0], (BATCH, SEQ, D_MODEL), 1.0),
        'x_sample': nrm(ks[1], (DEC_BATCH, DEC_SEQ, D_MODEL), 1.0),
        'state_conv': nrm(ks[2], (DEPTH, DEC_BATCH, CONV_W - 1, D_QKV), 1.0),
        'state_gdn': nrm(ks[3], (DEPTH, DEC_BATCH, GDN_HEADS, GDN_DK, GDN_DV), GDN_DK ** -0.5),
        'p_prompt': nrm(ks[4], (DEPTH, BATCH, SEQ, D_PLE), 1.0),
        'p_sample': nrm(ks[5], (DEPTH, DEC_BATCH, DEC_SEQ, D_PLE), 1.0),
        'g_mix': gain(ks[6], (DEPTH, D_MODEL)),
        'w_in': nrm(ks[7], (DEPTH, D_MODEL, D_IN), D_MODEL ** -0.5),
        'w_conv': nrm(ks[8], (DEPTH, CONV_W, D_QKV), CONV_W ** -0.5),
        'a_log': jnp.log(jax.random.uniform(ks[10], (DEPTH, GDN_HEADS), jnp.float32, minval=1.0, maxval=16.0)),
        'dt_bias': dt + jnp.log(-jnp.expm1(-dt)),
        'gdn_norm': gain(ks[11], (DEPTH, GDN_DV)),
        'sgu_ln_g': gain(ks[12], (DEPTH, D_SGU)),
        'sgu_ln_b': nrm(ks[13], (DEPTH, D_SGU), 0.02),
        'w_s': nrm(ks[14], (DEPTH, SGU_HEADS, SGU_CHUNK, SGU_CHUNK), SGU_CHUNK ** -0.5),
        'b_s': gain(ks[15], (DEPTH, SGU_HEADS, SGU_CHUNK)),
        'w_out': nrm(ks[16], (DEPTH, D_MIX, D_MODEL), D_MIX ** -0.5),
        'g_ff': gain(ks[17], (DEPTH, D_MODEL)),
        'w_up': nrm(ks[18], (DEPTH, D_MODEL, D_FF), D_MODEL ** -0.5),
        'w_down': nrm(ks[19], (DEPTH, D_FF, D_MODEL), D_FF ** -0.5),
        'g_ple': gain(ks[20], (DEPTH, D_MODEL)),
        'w_ple': nrm(ks[21], (DEPTH, D_PLE, D_MODEL), D_PLE ** -0.5),
        'w_ple_gate': nrm(ks[22], (DEPTH, D_MODEL, D_MODEL), D_MODEL ** -0.5),
        'g_final': gain(ks[23], (D_MODEL,)),
    }


def reference(x_prompt, x_sample, state_conv, state_gdn, p_prompt, p_sample, g_mix, w_in, w_conv,
              a_log, dt_bias, gdn_norm, sgu_ln_g, sgu_ln_b, w_s, b_s, w_out, g_ff, w_up, w_down,
              g_ple, w_ple, w_ple_gate, g_final):
    hp = x_prompt
    hs = x_sample
    bp = x_prompt.shape[0]
    zero_hist = jnp.zeros((bp, CONV_W - 1, D_QKV), x_prompt.dtype)
    zero_state = jnp.zeros((bp, GDN_HEADS, GDN_DK, GDN_DV), jnp.float32)
    conv_p, gdn_p, conv_s, gdn_s, v_s = [], [], [], [], []
    for i in range(DEPTH):
        lw = (g_mix[i], w_in[i], w_conv[i], a_log[i], dt_bias[i], gdn_norm[i], sgu_ln_g[i], sgu_ln_b[i],
              w_s[i], b_s[i], w_out[i], g_ff[i], w_up[i], w_down[i], g_ple[i], w_ple[i], w_ple_gate[i])
        hp, c_p, s_p, _ = hybrid_layer(hp, p_prompt[i], zero_hist, zero_state, *lw)
        hs, c_s, s_s, vg_s = hybrid_layer(hs, p_sample[i], state_conv[i], state_gdn[i], *lw)
        conv_p.append(c_p)
        gdn_p.append(s_p)
        conv_s.append(c_s)
        gdn_s.append(s_s)
        v_s.append(vg_s)
    y_prompt = rms_norm(hp, g_final)
    y_sample = rms_norm(hs, g_final)
    return (y_prompt, y_sample, jnp.stack(conv_p), jnp.stack(gdn_p), jnp.stack(conv_s), jnp.stack(gdn_s), jnp.stack(v_s))
```

```python
import functools

import jax
import jax.numpy as jnp
from jax import lax
from jax.experimental import pallas as pl
from jax.experimental.pallas import tpu as pltpu

F32 = jnp.float32
BF16 = jnp.bfloat16

D_MODEL = 1024
D_GDN = 512
D_SGU = 512
HEADS = 4
DH = 128
D_QKV = 3 * D_GDN
D_FF = 4 * D_MODEL
D_PLE = 256
CONV_W = 4
GDN_CHUNK = 64
SGU_CHUNK = 128
EPS = 1e-6
LANES = 128
SUBLANES = 8
VMEM_LIMIT = 56 * 1024 * 1024

PROJ_ROWS = 512
GDN_ROWS = 512
TAIL_ROWS = 512
SAMPLE_BLOCK = 8


def _dot(a, b):
    return jnp.dot(a, b, preferred_element_type=F32)


def _dot_nt(a, b):
    return lax.dot_general(a, b, (((1,), (1,)), ((), ())), preferred_element_type=F32)


def _dot_tn(a, b):
    return lax.dot_general(a, b, (((0,), (0,)), ((), ())), preferred_element_type=F32)


def _split3(x):
    hi = x.astype(BF16)
    r1 = x - hi.astype(F32)
    mid = r1.astype(BF16)
    lo = (r1 - mid.astype(F32)).astype(BF16)
    return hi, mid, lo


def _rms(x, gain):
    return x * lax.rsqrt(jnp.mean(x * x, axis=-1, keepdims=True) + EPS) * gain


def _sigmoid(x):
    return 1.0 / (1.0 + jnp.exp(-x))


def _silu(x):
    return x * _sigmoid(x)


def _softplus(x):
    return jnp.maximum(x, 0.0) + jnp.log(1.0 + jnp.exp(-jnp.abs(x)))


def _gelu_tanh(x):
    c = 0.7978845608028654
    return 0.5 * x * (1.0 + jnp.tanh(c * (x + 0.044715 * (x * x * x))))


def _l2n(x):
    return x * lax.rsqrt(jnp.sum(x * x, axis=-1, keepdims=True) + 1e-6)


def _layer_norm(x, gain, bias):
    mu = jnp.mean(x, axis=-1, keepdims=True)
    xc = x - mu
    var = jnp.mean(xc * xc, axis=-1, keepdims=True)
    return xc * lax.rsqrt(var + EPS) * gain + bias


def _resident(shape):
    nd = len(shape)
    return pl.BlockSpec(shape, lambda *_: (0,) * nd, pipeline_mode=pl.Buffered(1))


def _proj_kernel(x_ref, gmix_ref, wqkv_ref, wz_ref, wab_ref, wabt_ref, wuv_ref, wconv_ref,
                 alog_row_ref, dtb_row_ref, alog_col_ref, dtb_col_ref, lng_ref, lnb_ref,
                 ws_ref, bst_ref,
                 qs_ref, k_ref, v_ref, zs_ref, sg_ref, col_ref, row_ref, cstate_ref,
                 cbuf):
    rows = x_ref.shape[0]
    t = pl.program_id(1)
    h = _rms(x_ref[...], gmix_ref[...]).astype(BF16)

    pre = _dot(h, wqkv_ref[...])

    @pl.when(t == 0)
    def _():
        cbuf[0:SUBLANES, :] = jnp.zeros((SUBLANES, D_QKV), F32)

    @pl.when(t > 0)
    def _():
        cbuf[0:SUBLANES, :] = cbuf[rows:rows + SUBLANES, :]

    cbuf[SUBLANES:SUBLANES + rows, :] = pre
    base = SUBLANES - (CONV_W - 1)
    y = cbuf[base:base + rows, :] * wconv_ref[0:1, :]
    for j in range(1, CONV_W):
        y = y + cbuf[base + j:base + j + rows, :] * wconv_ref[j:j + 1, :]
    y = _silu(y)

    @pl.when(t == pl.num_programs(1) - 1)
    def _():
        cstate_ref[...] = pre[rows - (CONV_W - 1):rows, :]

    for hh in range(HEADS):
        lo = hh * DH
        qs_ref[:, lo:lo + DH] = (_l2n(y[:, lo:lo + DH]) * (DH ** -0.5)).astype(BF16)
        k_ref[:, lo:lo + DH] = _l2n(y[:, D_GDN + lo:D_GDN + lo + DH]).astype(BF16)
    v_ref[...] = y[:, 2 * D_GDN:].astype(BF16)

    zs_ref[...] = _silu(_dot(h, wz_ref[...])).astype(BF16)

    ab = _dot(h, wab_ref[...])
    lane = lax.broadcasted_iota(jnp.int32, (GDN_CHUNK, LANES), 1)
    g_col = -jnp.exp(alog_row_ref[...]) * _softplus(ab + dtb_row_ref[...])
    abt = _dot_nt(wabt_ref[...], h)[0:2 * HEADS, :]
    sub = lax.broadcasted_iota(jnp.int32, (2 * HEADS, GDN_CHUNK), 0)
    g_row = -jnp.exp(alog_col_ref[...]) * _softplus(abt + dtb_col_ref[...])
    ri = lax.broadcasted_iota(jnp.int32, (GDN_CHUNK, GDN_CHUNK), 0)
    ci = lax.broadcasted_iota(jnp.int32, (GDN_CHUNK, GDN_CHUNK), 1)
    tri_l = (ri >= ci).astype(BF16)
    tri_u = (ri <= ci).astype(BF16)
    beta_col = _sigmoid(ab)
    beta_row = _sigmoid(abt)
    for c in range(rows // GDN_CHUNK):
        r0 = c * GDN_CHUNK
        p0, p1, p2 = _split3(g_col[r0:r0 + GDN_CHUNK, :])
        gc = _dot(tri_l, p0) + _dot(tri_l, p1) + _dot(tri_l, p2)
        col_ref[r0:r0 + GDN_CHUNK, :] = jnp.where(lane < HEADS, beta_col[r0:r0 + GDN_CHUNK, :], gc)
        p0, p1, p2 = _split3(g_row[:, r0:r0 + GDN_CHUNK])
        gr = _dot(p0, tri_u) + _dot(p1, tri_u) + _dot(p2, tri_u)
        row_ref[c] = jnp.where(sub < HEADS, beta_row[:, r0:r0 + GDN_CHUNK], gr)

    uv = _gelu_tanh(_dot(h, wuv_ref[...]))
    vg = _layer_norm(uv[:, D_SGU:], lng_ref[...], lnb_ref[...]).astype(BF16)
    ti = lax.broadcasted_iota(jnp.int32, (SGU_CHUNK, SGU_CHUNK), 0)
    si = lax.broadcasted_iota(jnp.int32, (SGU_CHUNK, SGU_CHUNK), 1)
    for hh in range(HEADS):
        lo = hh * DH
        w_tril = jnp.where(ti >= si, ws_ref[hh], 0.0).astype(BF16)
        bias = bst_ref[:, hh:hh + 1]
        for c in range(rows // SGU_CHUNK):
            r0 = c * SGU_CHUNK
            mix = _dot(w_tril, vg[r0:r0 + SGU_CHUNK, lo:lo + DH]) + bias
            sg_ref[r0:r0 + SGU_CHUNK, lo:lo + DH] = (uv[r0:r0 + SGU_CHUNK, lo:lo + DH] * mix).astype(BF16)


def _proj_call(x, gmix, wqkv, wz, wab, wabt, wuv, wconv, alog_row, dtb_row, alog_col, dtb_col,
               lng, lnb, ws, bst):
    b, l, _ = x.shape
    rows = PROJ_ROWS
    nt = l // rows
    nchunk = rows // GDN_CHUNK
    tile = lambda w: pl.BlockSpec((None, rows, w), lambda i, j: (i, j, 0))
    ins = [x, gmix, wqkv, wz, wab, wabt, wuv, wconv, alog_row, dtb_row, alog_col, dtb_col, lng, lnb, ws, bst]
    in_specs = [tile(D_MODEL)] + [_resident(a.shape) for a in ins[1:]]
    out_shape = [jax.ShapeDtypeStruct((b, l, D_GDN), BF16)] * 5 + [
        jax.ShapeDtypeStruct((b, l, LANES), F32),
        jax.ShapeDtypeStruct((b, l // GDN_CHUNK, 2 * HEADS, GDN_CHUNK), F32),
        jax.ShapeDtypeStruct((b, CONV_W - 1, D_QKV), F32),
    ]
    out_specs = [tile(D_GDN)] * 5 + [
        tile(LANES),
        pl.BlockSpec((None, nchunk, 2 * HEADS, GDN_CHUNK), lambda i, j: (i, j, 0, 0)),
        pl.BlockSpec((None, CONV_W - 1, D_QKV), lambda i, j: (i, 0, 0)),
    ]
    return pl.pallas_call(
        _proj_kernel,
        grid=(b, nt),
        in_specs=in_specs,
        out_specs=out_specs,
        out_shape=out_shape,
        scratch_shapes=[pltpu.VMEM((rows + SUBLANES, D_QKV), F32)],
        compiler_params=pltpu.CompilerParams(
            dimension_semantics=("arbitrary", "arbitrary"), vmem_limit_bytes=VMEM_LIMIT),
        name="proj",
    )(*ins)


def _neumann(a):
    c = a.shape[0]
    bm = -a
    bb = bm.astype(BF16)
    q = _dot(bb, bb)
    p = bm
    power = 2
    while 2 * power < c:
        qb = q.astype(BF16)
        r = _dot(jnp.concatenate([p.astype(BF16), qb], axis=0), qb)
        p = p + q + r[:c]
        q = r[c:]
        power *= 2
    return p + q + _dot(p.astype(BF16), q.astype(BF16))


def _gdn_kernel(qs_ref, k_ref, v_ref, col_ref, row_ref, zs_ref, gnorm_ref,
                og_ref, state_ref, s_scr):
    rows = qs_ref.shape[0]
    c_ = GDN_CHUNK
    t = pl.program_id(1)

    @pl.when(t == 0)
    def _():
        s_scr[...] = jnp.zeros(s_scr.shape, F32)

    ri = lax.broadcasted_iota(jnp.int32, (c_, c_), 0)
    ci = lax.broadcasted_iota(jnp.int32, (c_, c_), 1)
    incl = ri >= ci
    strict = ri > ci
    gnorm = gnorm_ref[...]

    def chunk(c, carry):
        r0 = pl.multiple_of(c * c_, c_)
        col = col_ref[pl.ds(r0, c_), :]
        row = row_ref[c]
        for hh in range(HEADS):
            lo = hh * DH
            beta = col[:, hh:hh + 1]
            gc = col[:, HEADS + hh:HEADS + hh + 1]
            gr = row[HEADS + hh:HEADS + hh + 1, :]
            glast = col[c_ - 1:c_, HEADS + hh:HEADS + hh + 1]
            decay = jnp.exp(jnp.where(incl, gc - gr, -jnp.inf))
            kh = k_ref[pl.ds(r0, c_), lo:lo + DH]
            qh = qs_ref[pl.ds(r0, c_), lo:lo + DH]
            vh = v_ref[pl.ds(r0, c_), lo:lo + DH].astype(F32)
            kf = kh.astype(F32)
            kb = kf * beta
            eg = jnp.exp(gc)
            m1 = _dot_nt(jnp.concatenate([kb.astype(BF16), qh], axis=0), kh)
            a_mat = jnp.where(strict, m1[:c_] * decay, 0.0)
            attn = (m1[c_:] * decay).astype(BF16)
            p = _neumann(a_mat)
            rhs = jnp.concatenate([vh * beta, kb * eg], axis=1)
            sol = rhs + _dot(p.astype(BF16), rhs.astype(BF16))
            u = sol[:, :DH]
            w = sol[:, DH:]
            s_old = s_scr[hh]
            qe = qh.astype(F32) * eg
            ws = _dot(jnp.concatenate([w.astype(BF16), qe.astype(BF16)], axis=0), s_old.astype(BF16))
            v_new = (u - ws[:c_]).astype(BF16)
            o = ws[c_:] + _dot(attn, v_new)
            ke = (kf * jnp.exp(glast - gc)).astype(BF16)
            s_scr[hh] = s_old * jnp.exp(glast) + _dot_tn(ke, v_new)
            og = _rms(o, gnorm) * zs_ref[pl.ds(r0, c_), lo:lo + DH].astype(F32)
            og_ref[pl.ds(r0, c_), lo:lo + DH] = og.astype(BF16)
        return carry

    lax.fori_loop(0, rows // c_, chunk, 0)

    @pl.when(t == pl.num_programs(1) - 1)
    def _():
        state_ref[...] = s_scr[...]


def _gdn_call(qs, k, v, col, row, zs, gnorm):
    b, l, _ = qs.shape
    rows = GDN_ROWS
    nt = l // rows
    nchunk = rows // GDN_CHUNK
    tile = lambda w: pl.BlockSpec((None, rows, w), lambda i, j: (i, j, 0))
    return pl.pallas_call(
        _gdn_kernel,
        grid=(b, nt),
        in_specs=[tile(D_GDN)] * 3 + [
            tile(LANES),
            pl.BlockSpec((None, nchunk, 2 * HEADS, GDN_CHUNK), lambda i, j: (i, j, 0, 0)),
            tile(D_GDN),
            _resident(gnorm.shape),
        ],
        out_specs=[tile(D_GDN), pl.BlockSpec((None, HEADS, DH, DH), lambda i, j: (i, 0, 0, 0))],
        out_shape=[jax.ShapeDtypeStruct((b, l, D_GDN), BF16),
                   jax.ShapeDtypeStruct((b, HEADS, DH, DH), F32)],
        scratch_shapes=[pltpu.VMEM((HEADS, DH, DH), F32)],
        compiler_params=pltpu.CompilerParams(
            dimension_semantics=("arbitrary", "arbitrary"), vmem_limit_bytes=VMEM_LIMIT),
        name="gdn",
    )(qs, k, v, col, row, zs, gnorm)


def _tail_kernel(x_ref, og_ref, sg_ref, p_ref, wout_ref, gff_ref, wup_ref, wdown_ref,
                 gple_ref, wple_ref, wgate_ref, gfin_ref, y_ref):
    x1 = x_ref[...] + _dot(og_ref[...], wout_ref[0:D_GDN, :]) + _dot(sg_ref[...], wout_ref[D_GDN:, :])
    h2 = _rms(x1, gff_ref[...]).astype(BF16)
    y_ref[...] = x1
    ff_block = D_FF // 4
    for c in range(D_FF // ff_block):
        lo = c * ff_block
        hid = jnp.maximum(_dot(h2, wup_ref[:, lo:lo + ff_block]), 0.0)
        y_ref[...] += _dot((hid * hid).astype(BF16), wdown_ref[lo:lo + ff_block, :])
    acc = y_ref[...]
    h3 = _rms(acc, gple_ref[...]).astype(BF16)
    gate = _sigmoid(_dot(h3, wgate_ref[...]))
    x3 = acc + _dot(p_ref[...].astype(BF16), wple_ref[...]) * gate
    y_ref[...] = _rms(x3, gfin_ref[...])


def _tail_call(x, og, sg, p, wout, gff, wup, wdown, gple, wple, wgate, gfin, rows):
    n = x.shape[0]
    tile = lambda w: pl.BlockSpec((rows, w), lambda i: (i, 0))
    weights = [wout, gff, wup, wdown, gple, wple, wgate, gfin]
    return pl.pallas_call(
        _tail_kernel,
        grid=(n // rows,),
        in_specs=[tile(D_MODEL), tile(D_GDN), tile(D_SGU), tile(D_PLE)] + [_resident(w.shape) for w in weights],
        out_specs=tile(D_MODEL),
        out_shape=jax.ShapeDtypeStruct((n, D_MODEL), F32),
        compiler_params=pltpu.CompilerParams(
            dimension_semantics=("arbitrary",), vmem_limit_bytes=VMEM_LIMIT),
        name="tail",
    )(x, og, sg, p, *weights)


def _sample_proj_kernel(x_ref, hist_ref, gmix_ref, wqkv_ref, wz_ref, wab_ref, wuv_ref, wconv_ref,
                        alog_row_ref, dtb_row_ref, lng_ref, lnb_ref, ws_ref, bs_ref,
                        nconv_ref, vg_ref, sg_ref, zs_ref, v_ref, col_ref, qt_ref, kt_ref):
    n = x_ref.shape[0]
    h = _rms(x_ref[...], gmix_ref[...]).astype(BF16)
    pre = _dot(h, wqkv_ref[...])
    y = pre * wconv_ref[CONV_W - 1:CONV_W, :]
    for j in range(CONV_W - 1):
        y = y + hist_ref[:, j * D_QKV:(j + 1) * D_QKV] * wconv_ref[j:j + 1, :]
    y = _silu(y)
    nconv_ref[:, 0:(CONV_W - 2) * D_QKV] = hist_ref[:, D_QKV:]
    nconv_ref[:, (CONV_W - 2) * D_QKV:] = pre
    v_ref[...] = y[:, 2 * D_GDN:]
    blocks = n // SAMPLE_BLOCK
    for hh in range(HEADS):
        lo = hh * DH
        qt = jnp.transpose(_l2n(y[:, lo:lo + DH]) * (DH ** -0.5))
        kt = jnp.transpose(_l2n(y[:, D_GDN + lo:D_GDN + lo + DH]))
        for i in range(blocks):
            qt_ref[i, hh] = qt[:, i * SAMPLE_BLOCK:(i + 1) * SAMPLE_BLOCK]
            kt_ref[i, hh] = kt[:, i * SAMPLE_BLOCK:(i + 1) * SAMPLE_BLOCK]
    zs_ref[...] = _silu(_dot(h, wz_ref[...]))
    ab = _dot(h, wab_ref[...])
    lane = lax.broadcasted_iota(jnp.int32, ab.shape, 1)
    g_col = -jnp.exp(alog_row_ref[...]) * _softplus(ab + dtb_row_ref[...])
    col_ref[...] = jnp.where(lane < HEADS, _sigmoid(ab), g_col)
    uv = _gelu_tanh(_dot(h, wuv_ref[...]))
    vg = _layer_norm(uv[:, D_SGU:], lng_ref[...], lnb_ref[...])
    vg_ref[...] = vg
    for hh in range(HEADS):
        lo = hh * DH
        mix = ws_ref[hh, 0:1, 0:1] * vg[:, lo:lo + DH] + bs_ref[hh:hh + 1, 0:1]
        sg_ref[:, lo:lo + DH] = (uv[:, lo:lo + DH] * mix).astype(BF16)


def _sample_proj_call(x, hist, gmix, wqkv, wz, wab, wuv, wconv, alog_row, dtb_row, lng, lnb, ws, bs):
    n = x.shape[0]
    blocks = n // SAMPLE_BLOCK
    out_shape = [
        jax.ShapeDtypeStruct((n, (CONV_W - 1) * D_QKV), F32),
        jax.ShapeDtypeStruct((n, D_SGU), F32),
        jax.ShapeDtypeStruct((n, D_SGU), BF16),
        jax.ShapeDtypeStruct((n, D_GDN), F32),
        jax.ShapeDtypeStruct((n, D_GDN), F32),
        jax.ShapeDtypeStruct((n, LANES), F32),
        jax.ShapeDtypeStruct((blocks, HEADS, DH, SAMPLE_BLOCK), F32),
        jax.ShapeDtypeStruct((blocks, HEADS, DH, SAMPLE_BLOCK), F32),
    ]
    return pl.pallas_call(
        _sample_proj_kernel,
        out_shape=out_shape,
        compiler_params=pltpu.CompilerParams(vmem_limit_bytes=VMEM_LIMIT),
        name="sample_proj",
    )(x, hist, gmix, wqkv, wz, wab, wuv, wconv, alog_row, dtb_row, lng, lnb, ws, bs)


def _sample_gdn_kernel(s_ref, qt_ref, kt_ref, v_ref, col_ref, zs_ref, gnorm_ref, og_ref, snew_ref):
    gnorm = gnorm_ref[...]
    for bb in range(SAMPLE_BLOCK):
        for hh in range(HEADS):
            lo = hh * DH
            s0 = s_ref[bb, hh]
            kcol = kt_ref[hh, :, bb:bb + 1]
            qcol = qt_ref[hh, :, bb:bb + 1]
            beta = col_ref[bb:bb + 1, hh:hh + 1]
            eg = jnp.exp(col_ref[bb:bb + 1, HEADS + hh:HEADS + hh + 1])
            vrow = v_ref[bb:bb + 1, lo:lo + DH]
            sd = s0 * eg
            v_new = beta * (vrow - jnp.sum(sd * kcol, axis=0, keepdims=True))
            s1 = sd + kcol * v_new
            snew_ref[bb, hh] = s1
            o = jnp.sum(s1 * qcol, axis=0, keepdims=True)
            og = _rms(o, gnorm) * zs_ref[bb:bb + 1, lo:lo + DH]
            og_ref[bb:bb + 1, lo:lo + DH] = og.astype(BF16)


def _sample_gdn_call(state, qt, kt, v, col, zs, gnorm):
    n = state.shape[0]
    blk = SAMPLE_BLOCK
    rowtile = lambda w: pl.BlockSpec((blk, w), lambda i: (i, 0))
    return pl.pallas_call(
        _sample_gdn_kernel,
        grid=(n // blk,),
        in_specs=[
            pl.BlockSpec((blk, HEADS, DH, DH), lambda i: (i, 0, 0, 0)),
            pl.BlockSpec((None, HEADS, DH, blk), lambda i: (i, 0, 0, 0)),
            pl.BlockSpec((None, HEADS, DH, blk), lambda i: (i, 0, 0, 0)),
            rowtile(D_GDN), rowtile(LANES), rowtile(D_GDN),
            _resident(gnorm.shape),
        ],
        out_specs=[rowtile(D_GDN), pl.BlockSpec((blk, HEADS, DH, DH), lambda i: (i, 0, 0, 0))],
        out_shape=[jax.ShapeDtypeStruct((n, D_GDN), BF16),
                   jax.ShapeDtypeStruct((n, HEADS, DH, DH), F32)],
        compiler_params=pltpu.CompilerParams(
            dimension_semantics=("arbitrary",), vmem_limit_bytes=VMEM_LIMIT),
        name="sample_gdn",
    )(state, qt, kt, v, col, zs, gnorm)


def kernel(x_prompt, x_sample, state_conv, state_gdn, p_prompt, p_sample, g_mix, w_in, w_conv, a_log, dt_bias, gdn_norm, sgu_ln_g, sgu_ln_b, w_s, b_s, w_out, g_ff, w_up, w_down, g_ple, w_ple, w_ple_gate, g_final):
    depth = w_in.shape[0]
    assert depth == 1, "single-layer trunk only"
    b, l, d = x_prompt.shape
    n = x_sample.shape[0]
    assert x_sample.shape[1] == 1 and d == D_MODEL
    assert l % PROJ_ROWS == 0 and l % GDN_ROWS == 0 and (b * l) % TAIL_ROWS == 0 and n % SAMPLE_BLOCK == 0

    o0, o1, o2, o3 = D_QKV, D_QKV + D_GDN, D_QKV + D_GDN + HEADS, D_QKV + D_GDN + 2 * HEADS
    w_in0 = w_in[0]
    wqkv = w_in0[:, :o0].astype(BF16)
    wz = w_in0[:, o0:o1].astype(BF16)
    wab8 = w_in0[:, o1:o3]
    wab = jnp.pad(wab8, ((0, 0), (0, LANES - 2 * HEADS))).astype(BF16)
    wabt = jnp.pad(wab8.T, ((0, 2 * SUBLANES - 2 * HEADS), (0, 0))).astype(BF16)
    wuv = w_in0[:, o3:].astype(BF16)
    row2 = lambda a: a.reshape(1, -1)
    pad_lanes = lambda a: jnp.pad(a, (HEADS, LANES - 2 * HEADS)).reshape(1, LANES)
    pad_rows = lambda a: jnp.pad(a, (HEADS, 0)).reshape(2 * HEADS, 1)
    alog_row, dtb_row = pad_lanes(a_log[0]), pad_lanes(dt_bias[0])
    alog_col, dtb_col = pad_rows(a_log[0]), pad_rows(dt_bias[0])
    gmix, gff, gple, gfin = row2(g_mix[0]), row2(g_ff[0]), row2(g_ple[0]), row2(g_final)
    lng, lnb, gnorm = row2(sgu_ln_g[0]), row2(sgu_ln_b[0]), row2(gdn_norm[0])
    wout, wup, wdown = w_out[0].astype(BF16), w_up[0].astype(BF16), w_down[0].astype(BF16)
    wple, wgate = w_ple[0].astype(BF16), w_ple_gate[0].astype(BF16)
    tail_w = (wout, gff, wup, wdown, gple, wple, wgate, gfin)

    qs, k, v, zs, sg, col, row, conv_p = _proj_call(
        x_prompt, gmix, wqkv, wz, wab, wabt, wuv, w_conv[0], alog_row, dtb_row, alog_col, dtb_col,
        lng, lnb, w_s[0], b_s[0].T)
    og, gdn_p = _gdn_call(qs, k, v, col, row, zs, gnorm)
    y_prompt = _tail_call(x_prompt.reshape(b * l, d), og.reshape(b * l, D_GDN), sg.reshape(b * l, D_SGU),
                          p_prompt[0].reshape(b * l, D_PLE), *tail_w, rows=TAIL_ROWS).reshape(b, l, d)

    nconv, vg_s, sg_s, zs_s, v_s, col_s, qt, kt = _sample_proj_call(
        x_sample.reshape(n, d), state_conv[0].reshape(n, (CONV_W - 1) * D_QKV), gmix, wqkv, wz, wab, wuv,
        w_conv[0], alog_row, dtb_row, lng, lnb, w_s[0], b_s[0])
    og_s, gdn_s = _sample_gdn_call(state_gdn[0], qt, kt, v_s, col_s, zs_s, gnorm)
    y_sample = _tail_call(x_sample.reshape(n, d), og_s, sg_s, p_sample[0].reshape(n, D_PLE), *tail_w,
                          rows=n).reshape(n, 1, d)

    return (y_prompt, y_sample,
            conv_p[None], gdn_p[None],
            nconv.reshape(n, CONV_W - 1, D_QKV)[None], gdn_s[None],
            vg_s.reshape(n, 1, D_SGU)[None])
```

```python
import functools

import jax
import jax.numpy as jnp
from jax import lax
from jax.experimental import pallas as pl
from jax.experimental.pallas import tpu as pltpu

F32 = jnp.float32
BF16 = jnp.bfloat16

D_MODEL = 1024
D_GDN = 512
D_SGU = 512
HEADS = 4
DH = 128
D_QKV = 3 * D_GDN
D_FF = 4 * D_MODEL
D_PLE = 256
CONV_W = 4
GDN_CHUNK = 64
SGU_CHUNK = 128
EPS = 1e-6
LANES = 128
SUBLANES = 8
VMEM_LIMIT = 56 * 1024 * 1024

PROJ_ROWS = 512
GDN_ROWS = 256
TAIL_ROWS = 512
SAMPLE_BLOCK = 8


def _dot(a, b):
    return jnp.dot(a, b, preferred_element_type=F32)


def _dot_nt(a, b):
    return lax.dot_general(a, b, (((1,), (1,)), ((), ())), preferred_element_type=F32)


def _dot_tn(a, b):
    return lax.dot_general(a, b, (((0,), (0,)), ((), ())), preferred_element_type=F32)


def _split3(x):
    hi = x.astype(BF16)
    r1 = x - hi.astype(F32)
    mid = r1.astype(BF16)
    lo = (r1 - mid.astype(F32)).astype(BF16)
    return hi, mid, lo


def _rms(x, gain):
    return x * lax.rsqrt(jnp.mean(x * x, axis=-1, keepdims=True) + EPS) * gain


def _sigmoid(x):
    return 1.0 / (1.0 + jnp.exp(-x))


def _silu(x):
    return x * _sigmoid(x)


def _softplus(x):
    return jnp.maximum(x, 0.0) + jnp.log(1.0 + jnp.exp(-jnp.abs(x)))


def _gelu_tanh(x):
    c = 0.7978845608028654
    return 0.5 * x * (1.0 + jnp.tanh(c * (x + 0.044715 * (x * x * x))))


def _l2n(x):
    return x * lax.rsqrt(jnp.sum(x * x, axis=-1, keepdims=True) + 1e-6)


def _layer_norm(x, gain, bias):
    mu = jnp.mean(x, axis=-1, keepdims=True)
    xc = x - mu
    var = jnp.mean(xc * xc, axis=-1, keepdims=True)
    return xc * lax.rsqrt(var + EPS) * gain + bias


def _resident(shape):
    nd = len(shape)
    return pl.BlockSpec(shape, lambda *_: (0,) * nd, pipeline_mode=pl.Buffered(1))


def _proj_kernel(x_ref, gmix_ref, wqkv_ref, wz_ref, wab_ref, wabt_ref, wuv_ref, wconv_ref,
                 alog_row_ref, dtb_row_ref, alog_col_ref, dtb_col_ref, lng_ref, lnb_ref,
                 ws_ref, bst_ref,
                 qs_ref, k_ref, v_ref, zs_ref, sg_ref, col_ref, row_ref, cstate_ref,
                 cbuf):
    rows = x_ref.shape[0]
    t = pl.program_id(1)
    h = _rms(x_ref[...], gmix_ref[...]).astype(BF16)

    pre = _dot(h, wqkv_ref[...])

    @pl.when(t == 0)
    def _():
        cbuf[0:SUBLANES, :] = jnp.zeros((SUBLANES, D_QKV), F32)

    @pl.when(t > 0)
    def _():
        cbuf[0:SUBLANES, :] = cbuf[rows:rows + SUBLANES, :]

    cbuf[SUBLANES:SUBLANES + rows, :] = pre
    base = SUBLANES - (CONV_W - 1)
    y = cbuf[base:base + rows, :] * wconv_ref[0:1, :]
    for j in range(1, CONV_W):
        y = y + cbuf[base + j:base + j + rows, :] * wconv_ref[j:j + 1, :]
    y = _silu(y)

    @pl.when(t == pl.num_programs(1) - 1)
    def _():
        cstate_ref[...] = pre[rows - (CONV_W - 1):rows, :]

    for hh in range(HEADS):
        lo = hh * DH
        qs_ref[:, lo:lo + DH] = (_l2n(y[:, lo:lo + DH]) * (DH ** -0.5)).astype(BF16)
        k_ref[:, lo:lo + DH] = _l2n(y[:, D_GDN + lo:D_GDN + lo + DH]).astype(BF16)
    v_ref[...] = y[:, 2 * D_GDN:].astype(BF16)

    zs_ref[...] = _silu(_dot(h, wz_ref[...])).astype(BF16)

    ab = _dot(h, wab_ref[...])
    lane = lax.broadcasted_iota(jnp.int32, (GDN_CHUNK, LANES), 1)
    g_col = -jnp.exp(alog_row_ref[...]) * _softplus(ab + dtb_row_ref[...])
    abt = _dot_nt(wabt_ref[...], h)[0:2 * HEADS, :]
    sub = lax.broadcasted_iota(jnp.int32, (2 * HEADS, GDN_CHUNK), 0)
    g_row = -jnp.exp(alog_col_ref[...]) * _softplus(abt + dtb_col_ref[...])
    ri = lax.broadcasted_iota(jnp.int32, (GDN_CHUNK, GDN_CHUNK), 0)
    ci = lax.broadcasted_iota(jnp.int32, (GDN_CHUNK, GDN_CHUNK), 1)
    tri_l = (ri >= ci).astype(BF16)
    tri_u = (ri <= ci).astype(BF16)
    beta_col = _sigmoid(ab)
    beta_row = _sigmoid(abt)
    for c in range(rows // GDN_CHUNK):
        r0 = c * GDN_CHUNK
        p0, p1, p2 = _split3(g_col[r0:r0 + GDN_CHUNK, :])
        gc = _dot(tri_l, p0) + _dot(tri_l, p1) + _dot(tri_l, p2)
        col_ref[r0:r0 + GDN_CHUNK, :] = jnp.where(lane < HEADS, beta_col[r0:r0 + GDN_CHUNK, :], gc)
        p0, p1, p2 = _split3(g_row[:, r0:r0 + GDN_CHUNK])
        gr = _dot(p0, tri_u) + _dot(p1, tri_u) + _dot(p2, tri_u)
        row_ref[c] = jnp.where(sub < HEADS, beta_row[:, r0:r0 + GDN_CHUNK], gr)

    uv = _gelu_tanh(_dot(h, wuv_ref[...]))
    vg = _layer_norm(uv[:, D_SGU:], lng_ref[...], lnb_ref[...]).astype(BF16)
    ti = lax.broadcasted_iota(jnp.int32, (SGU_CHUNK, SGU_CHUNK), 0)
    si = lax.broadcasted_iota(jnp.int32, (SGU_CHUNK, SGU_CHUNK), 1)
    for hh in range(HEADS):
        lo = hh * DH
        w_tril = jnp.where(ti >= si, ws_ref[hh], 0.0).astype(BF16)
        bias = bst_ref[:, hh:hh + 1]
        for c in range(rows // SGU_CHUNK):
            r0 = c * SGU_CHUNK
            mix = _dot(w_tril, vg[r0:r0 + SGU_CHUNK, lo:lo + DH]) + bias
            sg_ref[r0:r0 + SGU_CHUNK, lo:lo + DH] = (uv[r0:r0 + SGU_CHUNK, lo:lo + DH] * mix).astype(BF16)


def _proj_call(x, gmix, wqkv, wz, wab, wabt, wuv, wconv, alog_row, dtb_row, alog_col, dtb_col,
               lng, lnb, ws, bst):
    b, l, _ = x.shape
    rows = PROJ_ROWS
    nt = l // rows
    nchunk = rows // GDN_CHUNK
    tile = lambda w: pl.BlockSpec((None, rows, w), lambda i, j: (i, j, 0))
    ins = [x, gmix, wqkv, wz, wab, wabt, wuv, wconv, alog_row, dtb_row, alog_col, dtb_col, lng, lnb, ws, bst]
    in_specs = [tile(D_MODEL)] + [_resident(a.shape) for a in ins[1:]]
    out_shape = [jax.ShapeDtypeStruct((b, l, D_GDN), BF16)] * 5 + [
        jax.ShapeDtypeStruct((b, l, LANES), F32),
        jax.ShapeDtypeStruct((b, l // GDN_CHUNK, 2 * HEADS, GDN_CHUNK), F32),
        jax.ShapeDtypeStruct((b, CONV_W - 1, D_QKV), F32),
    ]
    out_specs = [tile(D_GDN)] * 5 + [
        tile(LANES),
        pl.BlockSpec((None, nchunk, 2 * HEADS, GDN_CHUNK), lambda i, j: (i, j, 0, 0)),
        pl.BlockSpec((None, CONV_W - 1, D_QKV), lambda i, j: (i, 0, 0)),
    ]
    return pl.pallas_call(
        _proj_kernel,
        grid=(b, nt),
        in_specs=in_specs,
        out_specs=out_specs,
        out_shape=out_shape,
        scratch_shapes=[pltpu.VMEM((rows + SUBLANES, D_QKV), F32)],
        compiler_params=pltpu.CompilerParams(
            dimension_semantics=("arbitrary", "arbitrary"), vmem_limit_bytes=VMEM_LIMIT),
        name="proj",
    )(*ins)


def _neumann_many(a_list):
    c = a_list[0].shape[0]
    p = [-a for a in a_list]
    pb = [x.astype(BF16) for x in p]
    q = [_dot(x, x) for x in pb]
    power = 2
    while 2 * power < c:
        qb = [x.astype(BF16) for x in q]
        r = [_dot(jnp.concatenate([pi.astype(BF16), qi], axis=0), qi) for pi, qi in zip(p, qb)]
        p = [pi + qi + ri[:c] for pi, qi, ri in zip(p, q, r)]
        q = [ri[c:] for ri in r]
        power *= 2
    tail = [_dot(pi.astype(BF16), qi.astype(BF16)) for pi, qi in zip(p, q)]
    return [pi + qi + ti for pi, qi, ti in zip(p, q, tail)]


def _gdn_kernel(qs_ref, k_ref, v_ref, col_ref, row_ref, zs_ref, gnorm_ref,
                og_ref, state_ref, s_scr):
    rows = qs_ref.shape[0]
    c_ = GDN_CHUNK
    t = pl.program_id(1)

    @pl.when(t == 0)
    def _():
        s_scr[...] = jnp.zeros(s_scr.shape, F32)

    ri = lax.broadcasted_iota(jnp.int32, (c_, c_), 0)
    ci = lax.broadcasted_iota(jnp.int32, (c_, c_), 1)
    incl = ri >= ci
    strict = ri > ci
    gnorm = gnorm_ref[...]

    pairs = [(c, hh) for c in range(rows // c_) for hh in range(HEADS)]
    cols = [col_ref[c * c_:(c + 1) * c_, :] for c in range(rows // c_)]
    rws = [row_ref[c] for c in range(rows // c_)]
    beta = [cols[c][:, hh:hh + 1] for c, hh in pairs]
    gc = [cols[c][:, HEADS + hh:HEADS + hh + 1] for c, hh in pairs]
    gr = [rws[c][HEADS + hh:HEADS + hh + 1, :] for c, hh in pairs]
    glast = [cols[c][c_ - 1:c_, HEADS + hh:HEADS + hh + 1] for c, hh in pairs]
    kh = [k_ref[c * c_:(c + 1) * c_, hh * DH:(hh + 1) * DH] for c, hh in pairs]
    qh = [qs_ref[c * c_:(c + 1) * c_, hh * DH:(hh + 1) * DH] for c, hh in pairs]
    kf = [x.astype(F32) for x in kh]
    kb = [x * b for x, b in zip(kf, beta)]
    m1 = [_dot_nt(jnp.concatenate([x.astype(BF16), q], axis=0), k) for x, q, k in zip(kb, qh, kh)]
    decay = [jnp.exp(jnp.where(incl, a - b, -jnp.inf)) for a, b in zip(gc, gr)]
    attn = [(m[c_:] * d).astype(BF16) for m, d in zip(m1, decay)]
    p = _neumann_many([jnp.where(strict, m[:c_] * d, 0.0) for m, d in zip(m1, decay)])
    eg = [jnp.exp(x) for x in gc]
    rhs = [jnp.concatenate([x * e, v_ref[c * c_:(c + 1) * c_, hh * DH:(hh + 1) * DH].astype(F32) * b], axis=1)
           for x, e, b, (c, hh) in zip(kb, eg, beta, pairs)]
    wu = [(x + _dot(pi.astype(BF16), x.astype(BF16))).astype(BF16) for x, pi in zip(rhs, p)]
    ke_t = [jnp.transpose((x * jnp.exp(gl - g)).astype(BF16)) for x, gl, g in zip(kf, glast, gc)]
    r1 = [_dot(jnp.concatenate([kt, a], axis=0), x) for kt, a, x in zip(ke_t, attn, wu)]
    stage = {}
    for i, pair in enumerate(pairs):
        q_eff = qh[i].astype(F32) * eg[i] - r1[i][DH:, :DH]
        mq = jnp.concatenate([(-r1[i][:DH, :DH]).astype(BF16), q_eff.astype(BF16)], axis=0)
        stage[pair] = (mq, r1[i][:DH, DH:], r1[i][DH:, DH:], jnp.exp(glast[i]))

    state = [s_scr[hh] for hh in range(HEADS)]
    for c in range(rows // c_):
        r0 = c * c_
        for hh in range(HEADS):
            lo = hh * DH
            mq, n_mat, o_zero, e_last = stage[c, hh]
            r = _dot(mq, state[hh].astype(BF16))
            state[hh] = state[hh] * e_last + r[:DH] + n_mat
            og = _rms(r[DH:] + o_zero, gnorm) * zs_ref[r0:r0 + c_, lo:lo + DH].astype(F32)
            og_ref[r0:r0 + c_, lo:lo + DH] = og.astype(BF16)
    for hh in range(HEADS):
        s_scr[hh] = state[hh]

    @pl.when(t == pl.num_programs(1) - 1)
    def _():
        state_ref[...] = s_scr[...]


def _gdn_call(qs, k, v, col, row, zs, gnorm):
    b, l, _ = qs.shape
    rows = GDN_ROWS
    nt = l // rows
    nchunk = rows // GDN_CHUNK
    tile = lambda w: pl.BlockSpec((None, rows, w), lambda i, j: (i, j, 0))
    return pl.pallas_call(
        _gdn_kernel,
        grid=(b, nt),
        in_specs=[tile(D_GDN)] * 3 + [
            tile(LANES),
            pl.BlockSpec((None, nchunk, 2 * HEADS, GDN_CHUNK), lambda i, j: (i, j, 0, 0)),
            tile(D_GDN),
            _resident(gnorm.shape),
        ],
        out_specs=[tile(D_GDN), pl.BlockSpec((None, HEADS, DH, DH), lambda i, j: (i, 0, 0, 0))],
        out_shape=[jax.ShapeDtypeStruct((b, l, D_GDN), BF16),
                   jax.ShapeDtypeStruct((b, HEADS, DH, DH), F32)],
        scratch_shapes=[pltpu.VMEM((HEADS, DH, DH), F32)],
        compiler_params=pltpu.CompilerParams(
            dimension_semantics=("arbitrary", "arbitrary"), vmem_limit_bytes=VMEM_LIMIT),
        name="gdn",
    )(qs, k, v, col, row, zs, gnorm)


def _tail_kernel(x_ref, og_ref, sg_ref, p_ref, wout_ref, gff_ref, wup_ref, wdown_ref,
                 gple_ref, wple_ref, wgate_ref, gfin_ref, y_ref):
    x1 = x_ref[...] + _dot(og_ref[...], wout_ref[0:D_GDN, :]) + _dot(sg_ref[...], wout_ref[D_GDN:, :])
    h2 = _rms(x1, gff_ref[...]).astype(BF16)
    y_ref[...] = x1
    ff_block = D_FF // 4
    for c in range(D_FF // ff_block):
        lo = c * ff_block
        hid = jnp.maximum(_dot(h2, wup_ref[:, lo:lo + ff_block]), 0.0)
        y_ref[...] += _dot((hid * hid).astype(BF16), wdown_ref[lo:lo + ff_block, :])
    acc = y_ref[...]
    h3 = _rms(acc, gple_ref[...]).astype(BF16)
    gate = _sigmoid(_dot(h3, wgate_ref[...]))
    x3 = acc + _dot(p_ref[...].astype(BF16), wple_ref[...]) * gate
    y_ref[...] = _rms(x3, gfin_ref[...])


def _tail_call(x, og, sg, p, wout, gff, wup, wdown, gple, wple, wgate, gfin, rows):
    n = x.shape[0]
    tile = lambda w: pl.BlockSpec((rows, w), lambda i: (i, 0))
    weights = [wout, gff, wup, wdown, gple, wple, wgate, gfin]
    return pl.pallas_call(
        _tail_kernel,
        grid=(n // rows,),
        in_specs=[tile(D_MODEL), tile(D_GDN), tile(D_SGU), tile(D_PLE)] + [_resident(w.shape) for w in weights],
        out_specs=tile(D_MODEL),
        out_shape=jax.ShapeDtypeStruct((n, D_MODEL), F32),
        compiler_params=pltpu.CompilerParams(
            dimension_semantics=("arbitrary",), vmem_limit_bytes=VMEM_LIMIT),
        name="tail",
    )(x, og, sg, p, *weights)


def _sample_proj_kernel(x_ref, hist_ref, gmix_ref, wqkv_ref, wz_ref, wab_ref, wuv_ref, wconv_ref,
                        alog_row_ref, dtb_row_ref, lng_ref, lnb_ref, ws_ref, bs_ref,
                        nconv_ref, vg_ref, sg_ref, zs_ref, v_ref, col_ref, qt_ref, kt_ref):
    n = x_ref.shape[0]
    h = _rms(x_ref[...], gmix_ref[...]).astype(BF16)
    pre = _dot(h, wqkv_ref[...])
    y = pre * wconv_ref[CONV_W - 1:CONV_W, :]
    for j in range(CONV_W - 1):
        y = y + hist_ref[:, j * D_QKV:(j + 1) * D_QKV] * wconv_ref[j:j + 1, :]
    y = _silu(y)
    nconv_ref[:, 0:(CONV_W - 2) * D_QKV] = hist_ref[:, D_QKV:]
    nconv_ref[:, (CONV_W - 2) * D_QKV:] = pre
    v_ref[...] = y[:, 2 * D_GDN:]
    blocks = n // SAMPLE_BLOCK
    for hh in range(HEADS):
        lo = hh * DH
        qt = jnp.transpose(_l2n(y[:, lo:lo + DH]) * (DH ** -0.5))
        kt = jnp.transpose(_l2n(y[:, D_GDN + lo:D_GDN + lo + DH]))
        for i in range(blocks):
            qt_ref[i, hh] = qt[:, i * SAMPLE_BLOCK:(i + 1) * SAMPLE_BLOCK]
            kt_ref[i, hh] = kt[:, i * SAMPLE_BLOCK:(i + 1) * SAMPLE_BLOCK]
    zs_ref[...] = _silu(_dot(h, wz_ref[...]))
    ab = _dot(h, wab_ref[...])
    lane = lax.broadcasted_iota(jnp.int32, ab.shape, 1)
    g_col = -jnp.exp(alog_row_ref[...]) * _softplus(ab + dtb_row_ref[...])
    col_ref[...] = jnp.where(lane < HEADS, _sigmoid(ab), g_col)
    uv = _gelu_tanh(_dot(h, wuv_ref[...]))
    vg = _layer_norm(uv[:, D_SGU:], lng_ref[...], lnb_ref[...])
    vg_ref[...] = vg
    for hh in range(HEADS):
        lo = hh * DH
        mix = ws_ref[hh, 0:1, 0:1] * vg[:, lo:lo + DH] + bs_ref[hh:hh + 1, 0:1]
        sg_ref[:, lo:lo + DH] = (uv[:, lo:lo + DH] * mix).astype(BF16)


def _sample_proj_call(x, hist, gmix, wqkv, wz, wab, wuv, wconv, alog_row, dtb_row, lng, lnb, ws, bs):
    n = x.shape[0]
    blocks = n // SAMPLE_BLOCK
    out_shape = [
        jax.ShapeDtypeStruct((n, (CONV_W - 1) * D_QKV), F32),
        jax.ShapeDtypeStruct((n, D_SGU), F32),
        jax.ShapeDtypeStruct((n, D_SGU), BF16),
        jax.ShapeDtypeStruct((n, D_GDN), F32),
        jax.ShapeDtypeStruct((n, D_GDN), F32),
        jax.ShapeDtypeStruct((n, LANES), F32),
        jax.ShapeDtypeStruct((blocks, HEADS, DH, SAMPLE_BLOCK), F32),
        jax.ShapeDtypeStruct((blocks, HEADS, DH, SAMPLE_BLOCK), F32),
    ]
    return pl.pallas_call(
        _sample_proj_kernel,
        out_shape=out_shape,
        compiler_params=pltpu.CompilerParams(vmem_limit_bytes=VMEM_LIMIT),
        name="sample_proj",
    )(x, hist, gmix, wqkv, wz, wab, wuv, wconv, alog_row, dtb_row, lng, lnb, ws, bs)


def _sample_gdn_kernel(s_ref, qt_ref, kt_ref, v_ref, col_ref, zs_ref, gnorm_ref, og_ref, snew_ref):
    gnorm = gnorm_ref[...]
    for bb in range(SAMPLE_BLOCK):
        for hh in range(HEADS):
            lo = hh * DH
            s0 = s_ref[bb, hh]
            kcol = kt_ref[hh, :, bb:bb + 1]
            qcol = qt_ref[hh, :, bb:bb + 1]
            beta = col_ref[bb:bb + 1, hh:hh + 1]
            eg = jnp.exp(col_ref[bb:bb + 1, HEADS + hh:HEADS + hh + 1])
            vrow = v_ref[bb:bb + 1, lo:lo + DH]
            sd = s0 * eg
            v_new = beta * (vrow - jnp.sum(sd * kcol, axis=0, keepdims=True))
            s1 = sd + kcol * v_new
            snew_ref[bb, hh] = s1
            o = jnp.sum(s1 * qcol, axis=0, keepdims=True)
            og = _rms(o, gnorm) * zs_ref[bb:bb + 1, lo:lo + DH]
            og_ref[bb:bb + 1, lo:lo + DH] = og.astype(BF16)


def _sample_gdn_call(state, qt, kt, v, col, zs, gnorm):
    n = state.shape[0]
    blk = SAMPLE_BLOCK
    rowtile = lambda w: pl.BlockSpec((blk, w), lambda i: (i, 0))
    return pl.pallas_call(
        _sample_gdn_kernel,
        grid=(n // blk,),
        in_specs=[
            pl.BlockSpec((blk, HEADS, DH, DH), lambda i: (i, 0, 0, 0)),
            pl.BlockSpec((None, HEADS, DH, blk), lambda i: (i, 0, 0, 0)),
            pl.BlockSpec((None, HEADS, DH, blk), lambda i: (i, 0, 0, 0)),
            rowtile(D_GDN), rowtile(LANES), rowtile(D_GDN),
            _resident(gnorm.shape),
        ],
        out_specs=[rowtile(D_GDN), pl.BlockSpec((blk, HEADS, DH, DH), lambda i: (i, 0, 0, 0))],
        out_shape=[jax.ShapeDtypeStruct((n, D_GDN), BF16),
                   jax.ShapeDtypeStruct((n, HEADS, DH, DH), F32)],
        compiler_params=pltpu.CompilerParams(
            dimension_semantics=("arbitrary",), vmem_limit_bytes=VMEM_LIMIT),
        name="sample_gdn",
    )(state, qt, kt, v, col, zs, gnorm)


def kernel(x_prompt, x_sample, state_conv, state_gdn, p_prompt, p_sample, g_mix, w_in, w_conv, a_log, dt_bias, gdn_norm, sgu_ln_g, sgu_ln_b, w_s, b_s, w_out, g_ff, w_up, w_down, g_ple, w_ple, w_ple_gate, g_final):
    depth = w_in.shape[0]
    assert depth == 1, "single-layer trunk only"
    b, l, d = x_prompt.shape
    n = x_sample.shape[0]
    assert x_sample.shape[1] == 1 and d == D_MODEL
    assert l % PROJ_ROWS == 0 and l % GDN_ROWS == 0 and (b * l) % TAIL_ROWS == 0 and n % SAMPLE_BLOCK == 0

    o0, o1, o2, o3 = D_QKV, D_QKV + D_GDN, D_QKV + D_GDN + HEADS, D_QKV + D_GDN + 2 * HEADS
    w_in0 = w_in[0]
    wqkv = w_in0[:, :o0].astype(BF16)
    wz = w_in0[:, o0:o1].astype(BF16)
    wab8 = w_in0[:, o1:o3]
    wab = jnp.pad(wab8, ((0, 0), (0, LANES - 2 * HEADS))).astype(BF16)
    wabt = jnp.pad(wab8.T, ((0, 2 * SUBLANES - 2 * HEADS), (0, 0))).astype(BF16)
    wuv = w_in0[:, o3:].astype(BF16)
    row2 = lambda a: a.reshape(1, -1)
    pad_lanes = lambda a: jnp.pad(a, (HEADS, LANES - 2 * HEADS)).reshape(1, LANES)
    pad_rows = lambda a: jnp.pad(a, (HEADS, 0)).reshape(2 * HEADS, 1)
    alog_row, dtb_row = pad_lanes(a_log[0]), pad_lanes(dt_bias[0])
    alog_col, dtb_col = pad_rows(a_log[0]), pad_rows(dt_bias[0])
    gmix, gff, gple, gfin = row2(g_mix[0]), row2(g_ff[0]), row2(g_ple[0]), row2(g_final)
    lng, lnb, gnorm = row2(sgu_ln_g[0]), row2(sgu_ln_b[0]), row2(gdn_norm[0])
    wout, wup, wdown = w_out[0].astype(BF16), w_up[0].astype(BF16), w_down[0].astype(BF16)
    wple, wgate = w_ple[0].astype(BF16), w_ple_gate[0].astype(BF16)
    tail_w = (wout, gff, wup, wdown, gple, wple, wgate, gfin)

    qs, k, v, zs, sg, col, row, conv_p = _proj_call(
        x_prompt, gmix, wqkv, wz, wab, wabt, wuv, w_conv[0], alog_row, dtb_row, alog_col, dtb_col,
        lng, lnb, w_s[0], b_s[0].T)
    og, gdn_p = _gdn_call(qs, k, v, col, row, zs, gnorm)
    y_prompt = _tail_call(x_prompt.reshape(b * l, d), og.reshape(b * l, D_GDN), sg.reshape(b * l, D_SGU),
                          p_prompt[0].reshape(b * l, D_PLE), *tail_w, rows=TAIL_ROWS).reshape(b, l, d)

    nconv, vg_s, sg_s, zs_s, v_s, col_s, qt, kt = _sample_proj_call(
        x_sample.reshape(n, d), state_conv[0].reshape(n, (CONV_W - 1) * D_QKV), gmix, wqkv, wz, wab, wuv,
        w_conv[0], alog_row, dtb_row, lng, lnb, w_s[0], b_s[0])
    og_s, gdn_s = _sample_gdn_call(state_gdn[0], qt, kt, v_s, col_s, zs_s, gnorm)
    y_sample = _tail_call(x_sample.reshape(n, d), og_s, sg_s, p_sample[0].reshape(n, D_PLE), *tail_w,
                          rows=n).reshape(n, 1, d)

    return (y_prompt, y_sample,
            conv_p[None], gdn_p[None],
            nconv.reshape(n, CONV_W - 1, D_QKV)[None], gdn_s[None],
            vg_s.reshape(n, 1, D_SGU)[None])
```

```python
import functools

import jax
import jax.numpy as jnp
from jax import lax
from jax.experimental import pallas as pl
from jax.experimental.pallas import tpu as pltpu

F32 = jnp.float32
BF16 = jnp.bfloat16

D_MODEL = 1024
D_GDN = 512
D_SGU = 512
HEADS = 4
DH = 128
D_QKV = 3 * D_GDN
D_FF = 4 * D_MODEL
D_PLE = 256
CONV_W = 4
GDN_CHUNK = 64
SGU_CHUNK = 128
EPS = 1e-6
LANES = 128
SUBLANES = 8
VMEM_LIMIT = 56 * 1024 * 1024

COL_Q, COL_K, COL_V, COL_Z, COL_U, COL_VV, COL_AB = 0, 512, 1024, 1536, 2048, 2560, 3072
W_COLS = COL_AB + LANES

PROJ_ROWS = 512
CONV_BLOCK = 8
GDN_ROWS = 256
TAIL_ROWS = 512
SAMPLE_BLOCK = 8


def _dot(a, b):
    return jnp.dot(a, b, preferred_element_type=F32)


def _dot_nt(a, b):
    return lax.dot_general(a, b, (((1,), (1,)), ((), ())), preferred_element_type=F32)


def _dot_tn(a, b):
    return lax.dot_general(a, b, (((0,), (0,)), ((), ())), preferred_element_type=F32)


def _split3(x):
    hi = x.astype(BF16)
    r1 = x - hi.astype(F32)
    mid = r1.astype(BF16)
    lo = (r1 - mid.astype(F32)).astype(BF16)
    return hi, mid, lo


def _rms(x, gain):
    return x * lax.rsqrt(jnp.mean(x * x, axis=-1, keepdims=True) + EPS) * gain


def _sigmoid(x):
    return 1.0 / (1.0 + jnp.exp(-x))


def _silu(x):
    hx = 0.5 * x
    return hx + hx * jnp.tanh(hx)


def _softplus(x):
    return jnp.maximum(x, 0.0) + jnp.log(1.0 + jnp.exp(-jnp.abs(x)))


def _gelu_tanh(x):
    c = 0.7978845608028654
    hx = 0.5 * x
    return hx + hx * jnp.tanh(x * (c + (c * 0.044715) * (x * x)))


def _l2n(x):
    return x * lax.rsqrt(jnp.sum(x * x, axis=-1, keepdims=True) + 1e-6)


def _layer_norm(x, gain, bias):
    mu = jnp.mean(x, axis=-1, keepdims=True)
    xc = x - mu
    var = jnp.mean(xc * xc, axis=-1, keepdims=True)
    return xc * lax.rsqrt(var + EPS) * gain + bias


def _resident(shape):
    nd = len(shape)
    return pl.BlockSpec(shape, lambda *_: (0,) * nd, pipeline_mode=pl.Buffered(1))


def _proj_kernel(x_ref, gmix_ref, w_ref, wabt_ref, wconv_ref,
                 alog_row_ref, dtb_row_ref, alog_col_ref, dtb_col_ref, lng_ref, lnb_ref,
                 ws_ref, bst_ref,
                 qs_ref, k_ref, v_ref, zs_ref, sg_ref, col_ref, row_ref, cstate_ref,
                 cbuf):
    rows = x_ref.shape[0]
    first_tile = pl.program_id(1) == 0
    h = _rms(x_ref[...], gmix_ref[...]).astype(BF16)

    def project(lo, width=D_GDN):
        return _dot(h, w_ref[:, lo:lo + width])

    def conv_silu(pre, lo):
        cols = slice(lo, lo + D_GDN)
        groups = rows // SUBLANES
        tap = lambda j: wconv_ref[j:j + 1, cols].reshape(1, 1, D_GDN)
        carried = jnp.where(first_tile, 0.0, cbuf[:, cols])
        cbuf[:, cols] = pre[rows - SUBLANES:rows, :]
        cstate_ref[:, cols] = pre[rows - (CONV_W - 1):rows, :]
        cur = pre.reshape(groups, SUBLANES, D_GDN)
        last = carried.reshape(1, SUBLANES, D_GDN)
        sub = lax.broadcasted_iota(jnp.int32, (CONV_BLOCK, SUBLANES, D_GDN), 1)
        out = []
        for g0 in range(0, groups, CONV_BLOCK):
            blk = cur[g0:g0 + CONV_BLOCK]
            y = blk * tap(CONV_W - 1)
            for s in range(1, CONV_W):
                rolled = pltpu.roll(blk, s, axis=1)
                rolled_prev = jnp.concatenate([pltpu.roll(last, s, axis=1), rolled[:-1]], axis=0)
                y = y + jnp.where(sub >= s, rolled, rolled_prev) * tap(CONV_W - 1 - s)
            last = blk[CONV_BLOCK - 1:]
            out.append(_silu(y))
        return jnp.concatenate(out, axis=0).reshape(rows, D_GDN)

    pre_q = project(COL_Q)
    pre_k = project(COL_K)
    y = conv_silu(pre_q, COL_Q)
    for hh in range(HEADS):
        lo = hh * DH
        qs_ref[:, lo:lo + DH] = (_l2n(y[:, lo:lo + DH]) * (DH ** -0.5)).astype(BF16)
    pre_v = project(COL_V)
    y = conv_silu(pre_k, COL_K)
    for hh in range(HEADS):
        lo = hh * DH
        k_ref[:, lo:lo + DH] = _l2n(y[:, lo:lo + DH]).astype(BF16)
    pre_z = project(COL_Z)
    v_ref[...] = conv_silu(pre_v, COL_V).astype(BF16)
    pre_u = project(COL_U)
    zs_ref[...] = _silu(pre_z).astype(BF16)
    pre_vv = project(COL_VV)
    u_act = _gelu_tanh(pre_u)

    ab = project(COL_AB, LANES)
    lane = lax.broadcasted_iota(jnp.int32, (GDN_CHUNK, LANES), 1)
    g_col = -jnp.exp(alog_row_ref[...]) * _softplus(ab + dtb_row_ref[...])
    abt = _dot_nt(wabt_ref[...], h)[0:2 * HEADS, :]
    sub = lax.broadcasted_iota(jnp.int32, (2 * HEADS, GDN_CHUNK), 0)
    g_row = -jnp.exp(alog_col_ref[...]) * _softplus(abt + dtb_col_ref[...])
    ri = lax.broadcasted_iota(jnp.int32, (GDN_CHUNK, GDN_CHUNK), 0)
    ci = lax.broadcasted_iota(jnp.int32, (GDN_CHUNK, GDN_CHUNK), 1)
    tri_l = (ri >= ci).astype(BF16)
    tri_u = (ri <= ci).astype(BF16)
    beta_col = _sigmoid(ab)
    beta_row = _sigmoid(abt)
    for c in range(rows // GDN_CHUNK):
        r0 = c * GDN_CHUNK
        p0, p1, p2 = _split3(g_col[r0:r0 + GDN_CHUNK, :])
        gc = _dot(tri_l, p0) + _dot(tri_l, p1) + _dot(tri_l, p2)
        col_ref[r0:r0 + GDN_CHUNK, :] = jnp.where(lane < HEADS, beta_col[r0:r0 + GDN_CHUNK, :], gc)
        p0, p1, p2 = _split3(g_row[:, r0:r0 + GDN_CHUNK])
        gr = _dot(p0, tri_u) + _dot(p1, tri_u) + _dot(p2, tri_u)
        row_ref[c] = jnp.where(sub < HEADS, beta_row[:, r0:r0 + GDN_CHUNK], gr)

    vg = _layer_norm(_gelu_tanh(pre_vv), lng_ref[...], lnb_ref[...]).astype(BF16)
    ti = lax.broadcasted_iota(jnp.int32, (SGU_CHUNK, SGU_CHUNK), 0)
    si = lax.broadcasted_iota(jnp.int32, (SGU_CHUNK, SGU_CHUNK), 1)
    for hh in range(HEADS):
        lo = hh * DH
        w_tril = jnp.where(ti >= si, ws_ref[hh], 0.0).astype(BF16)
        bias = bst_ref[:, hh:hh + 1]
        for c in range(rows // SGU_CHUNK):
            r0 = c * SGU_CHUNK
            mix = _dot(w_tril, vg[r0:r0 + SGU_CHUNK, lo:lo + DH]) + bias
            sg_ref[r0:r0 + SGU_CHUNK, lo:lo + DH] = (u_act[r0:r0 + SGU_CHUNK, lo:lo + DH] * mix).astype(BF16)


def _proj_call(x, gmix, w_all, wabt, wconv, alog_row, dtb_row, alog_col, dtb_col, lng, lnb, ws, bst):
    b, l, _ = x.shape
    rows = PROJ_ROWS
    nt = l // rows
    nchunk = rows // GDN_CHUNK
    tile = lambda w: pl.BlockSpec((None, rows, w), lambda i, j: (i, j, 0))
    ins = [x, gmix, w_all, wabt, wconv, alog_row, dtb_row, alog_col, dtb_col, lng, lnb, ws, bst]
    in_specs = [tile(D_MODEL)] + [_resident(a.shape) for a in ins[1:]]
    out_shape = [jax.ShapeDtypeStruct((b, l, D_GDN), BF16)] * 5 + [
        jax.ShapeDtypeStruct((b, l, LANES), F32),
        jax.ShapeDtypeStruct((b, l // GDN_CHUNK, 2 * HEADS, GDN_CHUNK), F32),
        jax.ShapeDtypeStruct((b, CONV_W - 1, D_QKV), F32),
    ]
    out_specs = [tile(D_GDN)] * 5 + [
        tile(LANES),
        pl.BlockSpec((None, nchunk, 2 * HEADS, GDN_CHUNK), lambda i, j: (i, j, 0, 0)),
        pl.BlockSpec((None, CONV_W - 1, D_QKV), lambda i, j: (i, 0, 0)),
    ]
    return pl.pallas_call(
        _proj_kernel,
        grid=(b, nt),
        in_specs=in_specs,
        out_specs=out_specs,
        out_shape=out_shape,
        scratch_shapes=[pltpu.VMEM((SUBLANES, D_QKV), F32)],
        compiler_params=pltpu.CompilerParams(
            dimension_semantics=("arbitrary", "arbitrary"), vmem_limit_bytes=VMEM_LIMIT),
        name="proj",
    )(*ins)


def _neumann_many(a_list):
    c = a_list[0].shape[0]
    p = [-a for a in a_list]
    pb = [x.astype(BF16) for x in p]
    q = [_dot(x, x) for x in pb]
    power = 2
    while 2 * power < c:
        qb = [x.astype(BF16) for x in q]
        r = [_dot(jnp.concatenate([pi.astype(BF16), qi], axis=0), qi) for pi, qi in zip(p, qb)]
        p = [pi + qi + ri[:c] for pi, qi, ri in zip(p, q, r)]
        q = [ri[c:] for ri in r]
        power *= 2
    tail = [_dot(pi.astype(BF16), qi.astype(BF16)) for pi, qi in zip(p, q)]
    return [pi + qi + ti for pi, qi, ti in zip(p, q, tail)]


def _gdn_kernel(qs_ref, k_ref, v_ref, col_ref, row_ref, zs_ref, gnorm_ref,
                og_ref, state_ref, s_scr):
    rows = qs_ref.shape[0]
    c_ = GDN_CHUNK
    t = pl.program_id(1)

    @pl.when(t == 0)
    def _():
        s_scr[...] = jnp.zeros(s_scr.shape, F32)

    ri = lax.broadcasted_iota(jnp.int32, (c_, c_), 0)
    ci = lax.broadcasted_iota(jnp.int32, (c_, c_), 1)
    incl = ri >= ci
    strict = ri > ci
    gnorm = gnorm_ref[...]

    pairs = [(c, hh) for c in range(rows // c_) for hh in range(HEADS)]
    cols = [col_ref[c * c_:(c + 1) * c_, :] for c in range(rows // c_)]
    rws = [row_ref[c] for c in range(rows // c_)]
    beta = [cols[c][:, hh:hh + 1] for c, hh in pairs]
    gc = [cols[c][:, HEADS + hh:HEADS + hh + 1] for c, hh in pairs]
    gr = [rws[c][HEADS + hh:HEADS + hh + 1, :] for c, hh in pairs]
    glast = [cols[c][c_ - 1:c_, HEADS + hh:HEADS + hh + 1] for c, hh in pairs]
    kh = [k_ref[c * c_:(c + 1) * c_, hh * DH:(hh + 1) * DH] for c, hh in pairs]
    qh = [qs_ref[c * c_:(c + 1) * c_, hh * DH:(hh + 1) * DH] for c, hh in pairs]
    kf = [x.astype(F32) for x in kh]
    kb = [x * b for x, b in zip(kf, beta)]
    m1 = [_dot_nt(jnp.concatenate([x.astype(BF16), q], axis=0), k) for x, q, k in zip(kb, qh, kh)]
    decay = [jnp.exp(jnp.where(incl, a - b, -jnp.inf)) for a, b in zip(gc, gr)]
    attn = [(m[c_:] * d).astype(BF16) for m, d in zip(m1, decay)]
    p = _neumann_many([jnp.where(strict, m[:c_] * d, 0.0) for m, d in zip(m1, decay)])
    eg = [jnp.exp(x) for x in gc]
    rhs = [jnp.concatenate([x * e, v_ref[c * c_:(c + 1) * c_, hh * DH:(hh + 1) * DH].astype(F32) * b], axis=1)
           for x, e, b, (c, hh) in zip(kb, eg, beta, pairs)]
    wu = [(x + _dot(pi.astype(BF16), x.astype(BF16))).astype(BF16) for x, pi in zip(rhs, p)]
    ke_t = [jnp.transpose((x * jnp.exp(gl - g)).astype(BF16)) for x, gl, g in zip(kf, glast, gc)]
    r1 = [_dot(jnp.concatenate([kt, a], axis=0), x) for kt, a, x in zip(ke_t, attn, wu)]
    stage = {}
    for i, pair in enumerate(pairs):
        q_eff = qh[i].astype(F32) * eg[i] - r1[i][DH:, :DH]
        mq = jnp.concatenate([(-r1[i][:DH, :DH]).astype(BF16), q_eff.astype(BF16)], axis=0)
        stage[pair] = (mq, r1[i][:DH, DH:], r1[i][DH:, DH:], jnp.exp(glast[i]))

    state = [s_scr[hh] for hh in range(HEADS)]
    for c in range(rows // c_):
        r0 = c * c_
        for hh in range(HEADS):
            lo = hh * DH
            mq, n_mat, o_zero, e_last = stage[c, hh]
            r = _dot(mq, state[hh].astype(BF16))
            state[hh] = state[hh] * e_last + r[:DH] + n_mat
            og = _rms(r[DH:] + o_zero, gnorm) * zs_ref[r0:r0 + c_, lo:lo + DH].astype(F32)
            og_ref[r0:r0 + c_, lo:lo + DH] = og.astype(BF16)
    for hh in range(HEADS):
        s_scr[hh] = state[hh]

    @pl.when(t == pl.num_programs(1) - 1)
    def _():
        state_ref[...] = s_scr[...]


def _gdn_call(qs, k, v, col, row, zs, gnorm):
    b, l, _ = qs.shape
    rows = GDN_ROWS
    nt = l // rows
    nchunk = rows // GDN_CHUNK
    tile = lambda w: pl.BlockSpec((None, rows, w), lambda i, j: (i, j, 0))
    return pl.pallas_call(
        _gdn_kernel,
        grid=(b, nt),
        in_specs=[tile(D_GDN)] * 3 + [
            tile(LANES),
            pl.BlockSpec((None, nchunk, 2 * HEADS, GDN_CHUNK), lambda i, j: (i, j, 0, 0)),
            tile(D_GDN),
            _resident(gnorm.shape),
        ],
        out_specs=[tile(D_GDN), pl.BlockSpec((None, HEADS, DH, DH), lambda i, j: (i, 0, 0, 0))],
        out_shape=[jax.ShapeDtypeStruct((b, l, D_GDN), BF16),
                   jax.ShapeDtypeStruct((b, HEADS, DH, DH), F32)],
        scratch_shapes=[pltpu.VMEM((HEADS, DH, DH), F32)],
        compiler_params=pltpu.CompilerParams(
            dimension_semantics=("arbitrary", "arbitrary"), vmem_limit_bytes=VMEM_LIMIT),
        name="gdn",
    )(qs, k, v, col, row, zs, gnorm)


def _tail_kernel(x_ref, og_ref, sg_ref, p_ref, wout_ref, gff_ref, wup_ref, wdown_ref,
                 gple_ref, wple_ref, wgate_ref, gfin_ref, y_ref):
    x1 = x_ref[...] + _dot(og_ref[...], wout_ref[0:D_GDN, :]) + _dot(sg_ref[...], wout_ref[D_GDN:, :])
    h2 = _rms(x1, gff_ref[...]).astype(BF16)
    y_ref[...] = x1
    ff_block = D_FF // 4
    for c in range(D_FF // ff_block):
        lo = c * ff_block
        hid = jnp.maximum(_dot(h2, wup_ref[:, lo:lo + ff_block]), 0.0)
        y_ref[...] += _dot((hid * hid).astype(BF16), wdown_ref[lo:lo + ff_block, :])
    acc = y_ref[...]
    h3 = _rms(acc, gple_ref[...]).astype(BF16)
    gate = _sigmoid(_dot(h3, wgate_ref[...]))
    x3 = acc + _dot(p_ref[...].astype(BF16), wple_ref[...]) * gate
    y_ref[...] = _rms(x3, gfin_ref[...])


def _tail_call(x, og, sg, p, wout, gff, wup, wdown, gple, wple, wgate, gfin, rows):
    n = x.shape[0]
    tile = lambda w: pl.BlockSpec((rows, w), lambda i: (i, 0))
    weights = [wout, gff, wup, wdown, gple, wple, wgate, gfin]
    return pl.pallas_call(
        _tail_kernel,
        grid=(n // rows,),
        in_specs=[tile(D_MODEL), tile(D_GDN), tile(D_SGU), tile(D_PLE)] + [_resident(w.shape) for w in weights],
        out_specs=tile(D_MODEL),
        out_shape=jax.ShapeDtypeStruct((n, D_MODEL), F32),
        compiler_params=pltpu.CompilerParams(
            dimension_semantics=("arbitrary",), vmem_limit_bytes=VMEM_LIMIT),
        name="tail",
    )(x, og, sg, p, *weights)


def _sample_proj_kernel(x_ref, hist_ref, gmix_ref, w_ref, wconv_ref,
                        alog_row_ref, dtb_row_ref, lng_ref, lnb_ref, ws_ref, bs_ref,
                        nconv_ref, vg_ref, sg_ref, zs_ref, v_ref, col_ref, qt_ref, kt_ref):
    n = x_ref.shape[0]
    h = _rms(x_ref[...], gmix_ref[...]).astype(BF16)
    pre = _dot(h, w_ref[:, COL_Q:COL_Q + D_QKV])
    y = pre * wconv_ref[CONV_W - 1:CONV_W, :]
    for j in range(CONV_W - 1):
        y = y + hist_ref[j] * wconv_ref[j:j + 1, :]
        if j > 0:
            nconv_ref[j - 1] = hist_ref[j]
    nconv_ref[CONV_W - 2] = pre
    y = _silu(y)
    v_ref[...] = y[:, 2 * D_GDN:]
    blocks = n // SAMPLE_BLOCK
    for hh in range(HEADS):
        lo = hh * DH
        qt = jnp.transpose(_l2n(y[:, lo:lo + DH]) * (DH ** -0.5))
        kt = jnp.transpose(_l2n(y[:, D_GDN + lo:D_GDN + lo + DH]))
        for i in range(blocks):
            qt_ref[i, hh] = qt[:, i * SAMPLE_BLOCK:(i + 1) * SAMPLE_BLOCK]
            kt_ref[i, hh] = kt[:, i * SAMPLE_BLOCK:(i + 1) * SAMPLE_BLOCK]
    zs_ref[...] = _silu(_dot(h, w_ref[:, COL_Z:COL_Z + D_GDN]))
    ab = _dot(h, w_ref[:, COL_AB:COL_AB + LANES])
    lane = lax.broadcasted_iota(jnp.int32, ab.shape, 1)
    g_col = -jnp.exp(alog_row_ref[...]) * _softplus(ab + dtb_row_ref[...])
    col_ref[...] = jnp.where(lane < HEADS, _sigmoid(ab), g_col)
    uv = _gelu_tanh(_dot(h, w_ref[:, COL_U:COL_U + 2 * D_SGU]))
    vg = _layer_norm(uv[:, D_SGU:], lng_ref[...], lnb_ref[...])
    vg_ref[...] = vg
    for hh in range(HEADS):
        lo = hh * DH
        mix = ws_ref[hh, 0:1, 0:1] * vg[:, lo:lo + DH] + bs_ref[hh:hh + 1, 0:1]
        sg_ref[:, lo:lo + DH] = (uv[:, lo:lo + DH] * mix).astype(BF16)


def _sample_proj_call(x, hist, gmix, w_all, wconv, alog_row, dtb_row, lng, lnb, ws, bs):
    n = x.shape[0]
    blocks = n // SAMPLE_BLOCK
    out_shape = [
        jax.ShapeDtypeStruct((CONV_W - 1, n, D_QKV), F32),
        jax.ShapeDtypeStruct((n, D_SGU), F32),
        jax.ShapeDtypeStruct((n, D_SGU), BF16),
        jax.ShapeDtypeStruct((n, D_GDN), F32),
        jax.ShapeDtypeStruct((n, D_GDN), F32),
        jax.ShapeDtypeStruct((n, LANES), F32),
        jax.ShapeDtypeStruct((blocks, HEADS, DH, SAMPLE_BLOCK), F32),
        jax.ShapeDtypeStruct((blocks, HEADS, DH, SAMPLE_BLOCK), F32),
    ]
    return pl.pallas_call(
        _sample_proj_kernel,
        out_shape=out_shape,
        compiler_params=pltpu.CompilerParams(vmem_limit_bytes=VMEM_LIMIT),
        name="sample_proj",
    )(x, hist, gmix, w_all, wconv, alog_row, dtb_row, lng, lnb, ws, bs)


def _sample_gdn_kernel(s_ref, qt_ref, kt_ref, v_ref, col_ref, zs_ref, gnorm_ref, og_ref, snew_ref):
    gnorm = gnorm_ref[...]
    for bb in range(SAMPLE_BLOCK):
        for hh in range(HEADS):
            lo = hh * DH
            s0 = s_ref[bb, hh]
            kcol = kt_ref[hh, :, bb:bb + 1]
            qcol = qt_ref[hh, :, bb:bb + 1]
            beta = col_ref[bb:bb + 1, hh:hh + 1]
            eg = jnp.exp(col_ref[bb:bb + 1, HEADS + hh:HEADS + hh + 1])
            vrow = v_ref[bb:bb + 1, lo:lo + DH]
            sd = s0 * eg
            v_new = beta * (vrow - jnp.sum(sd * kcol, axis=0, keepdims=True))
            s1 = sd + kcol * v_new
            snew_ref[bb, hh] = s1
            o = jnp.sum(s1 * qcol, axis=0, keepdims=True)
            og = _rms(o, gnorm) * zs_ref[bb:bb + 1, lo:lo + DH]
            og_ref[bb:bb + 1, lo:lo + DH] = og.astype(BF16)


def _sample_gdn_call(state, qt, kt, v, col, zs, gnorm):
    n = state.shape[0]
    blk = SAMPLE_BLOCK
    rowtile = lambda w: pl.BlockSpec((blk, w), lambda i: (i, 0))
    return pl.pallas_call(
        _sample_gdn_kernel,
        grid=(n // blk,),
        in_specs=[
            pl.BlockSpec((blk, HEADS, DH, DH), lambda i: (i, 0, 0, 0)),
            pl.BlockSpec((None, HEADS, DH, blk), lambda i: (i, 0, 0, 0)),
            pl.BlockSpec((None, HEADS, DH, blk), lambda i: (i, 0, 0, 0)),
            rowtile(D_GDN), rowtile(LANES), rowtile(D_GDN),
            _resident(gnorm.shape),
        ],
        out_specs=[rowtile(D_GDN), pl.BlockSpec((blk, HEADS, DH, DH), lambda i: (i, 0, 0, 0))],
        out_shape=[jax.ShapeDtypeStruct((n, D_GDN), BF16),
                   jax.ShapeDtypeStruct((n, HEADS, DH, DH), F32)],
        compiler_params=pltpu.CompilerParams(
            dimension_semantics=("arbitrary",), vmem_limit_bytes=VMEM_LIMIT),
        name="sample_gdn",
    )(state, qt, kt, v, col, zs, gnorm)


def kernel(x_prompt, x_sample, state_conv, state_gdn, p_prompt, p_sample, g_mix, w_in, w_conv, a_log, dt_bias, gdn_norm, sgu_ln_g, sgu_ln_b, w_s, b_s, w_out, g_ff, w_up, w_down, g_ple, w_ple, w_ple_gate, g_final):
    depth = w_in.shape[0]
    assert depth == 1, "single-layer trunk only"
    b, l, d = x_prompt.shape
    n = x_sample.shape[0]
    assert x_sample.shape[1] == 1 and d == D_MODEL
    assert l % PROJ_ROWS == 0 and l % GDN_ROWS == 0 and (b * l) % TAIL_ROWS == 0 and n % SAMPLE_BLOCK == 0

    o1, o3 = D_QKV + D_GDN, D_QKV + D_GDN + 2 * HEADS
    w_in0 = w_in[0]
    wab8 = w_in0[:, o1:o3]
    w_all = jnp.concatenate(
        [w_in0[:, :o1], w_in0[:, o3:], wab8, jnp.zeros((d, LANES - 2 * HEADS), F32)], axis=1).astype(BF16)
    assert w_all.shape == (d, W_COLS)
    wabt = jnp.pad(wab8.T, ((0, 2 * SUBLANES - 2 * HEADS), (0, 0))).astype(BF16)
    row2 = lambda a: a.reshape(1, -1)
    pad_lanes = lambda a: jnp.pad(a, (HEADS, LANES - 2 * HEADS)).reshape(1, LANES)
    pad_rows = lambda a: jnp.pad(a, (HEADS, 0)).reshape(2 * HEADS, 1)
    alog_row, dtb_row = pad_lanes(a_log[0]), pad_lanes(dt_bias[0])
    alog_col, dtb_col = pad_rows(a_log[0]), pad_rows(dt_bias[0])
    gmix, gff, gple, gfin = row2(g_mix[0]), row2(g_ff[0]), row2(g_ple[0]), row2(g_final)
    lng, lnb, gnorm = row2(sgu_ln_g[0]), row2(sgu_ln_b[0]), row2(gdn_norm[0])
    wout, wup, wdown = w_out[0].astype(BF16), w_up[0].astype(BF16), w_down[0].astype(BF16)
    wple, wgate = w_ple[0].astype(BF16), w_ple_gate[0].astype(BF16)
    tail_w = (wout, gff, wup, wdown, gple, wple, wgate, gfin)

    qs, k, v, zs, sg, col, row, conv_p = _proj_call(
        x_prompt, gmix, w_all, wabt, w_conv[0], alog_row, dtb_row, alog_col, dtb_col,
        lng, lnb, w_s[0], b_s[0].T)
    og, gdn_p = _gdn_call(qs, k, v, col, row, zs, gnorm)
    y_prompt = _tail_call(x_prompt.reshape(b * l, d), og.reshape(b * l, D_GDN), sg.reshape(b * l, D_SGU),
                          p_prompt[0].reshape(b * l, D_PLE), *tail_w, rows=TAIL_ROWS).reshape(b, l, d)

    nconv, vg_s, sg_s, zs_s, v_s, col_s, qt, kt = _sample_proj_call(
        x_sample.reshape(n, d), jnp.transpose(state_conv[0], (1, 0, 2)), gmix, w_all,
        w_conv[0], alog_row, dtb_row, lng, lnb, w_s[0], b_s[0])
    og_s, gdn_s = _sample_gdn_call(state_gdn[0], qt, kt, v_s, col_s, zs_s, gnorm)
    y_sample = _tail_call(x_sample.reshape(n, d), og_s, sg_s, p_sample[0].reshape(n, D_PLE), *tail_w,
                          rows=n).reshape(n, 1, d)

    return (y_prompt, y_sample,
            conv_p[None], gdn_p[None],
            jnp.transpose(nconv, (1, 0, 2))[None], gdn_s[None],
            vg_s.reshape(n, 1, D_SGU)[None])
```

```python
import functools

import jax
import jax.numpy as jnp
from jax import lax
from jax.experimental import pallas as pl
from jax.experimental.pallas import tpu as pltpu

F32 = jnp.float32
BF16 = jnp.bfloat16

D_MODEL = 1024
D_GDN = 512
D_SGU = 512
HEADS = 4
DH = 128
D_QKV = 3 * D_GDN
D_FF = 4 * D_MODEL
D_PLE = 256
CONV_W = 4
GDN_CHUNK = 64
SGU_CHUNK = 128
EPS = 1e-6
LANES = 128
SUBLANES = 8
VMEM_LIMIT = 56 * 1024 * 1024

COL_Q, COL_K, COL_V, COL_Z, COL_U, COL_VV, COL_AB = 0, 512, 1024, 1536, 2048, 2560, 3072
W_COLS = COL_AB + LANES

PROJ_ROWS = 512
CONV_BLOCK = 8
GDN_ROWS = 256
TAIL_ROWS = 512
SAMPLE_BLOCK = 8
SAMPLE_GROUP = 2


def _dot(a, b):
    return jnp.dot(a, b, preferred_element_type=F32)


def _dot_nt(a, b):
    return lax.dot_general(a, b, (((1,), (1,)), ((), ())), preferred_element_type=F32)


def _dot_tn(a, b):
    return lax.dot_general(a, b, (((0,), (0,)), ((), ())), preferred_element_type=F32)


def _split3(x):
    hi = x.astype(BF16)
    r1 = x - hi.astype(F32)
    mid = r1.astype(BF16)
    lo = (r1 - mid.astype(F32)).astype(BF16)
    return hi, mid, lo


def _rms(x, gain):
    return x * lax.rsqrt(jnp.mean(x * x, axis=-1, keepdims=True) + EPS) * gain


def _sigmoid(x):
    return 1.0 / (1.0 + jnp.exp(-x))


def _silu(x):
    hx = 0.5 * x
    return hx + hx * jnp.tanh(hx)


def _softplus(x):
    return jnp.maximum(x, 0.0) + jnp.log(1.0 + jnp.exp(-jnp.abs(x)))


def _gelu_tanh(x):
    c = 0.7978845608028654
    hx = 0.5 * x
    return hx + hx * jnp.tanh(x * (c + (c * 0.044715) * (x * x)))


def _l2n(x):
    return x * lax.rsqrt(jnp.sum(x * x, axis=-1, keepdims=True) + 1e-6)


def _layer_norm(x, gain, bias):
    mu = jnp.mean(x, axis=-1, keepdims=True)
    xc = x - mu
    var = jnp.mean(xc * xc, axis=-1, keepdims=True)
    return xc * lax.rsqrt(var + EPS) * gain + bias


def _resident(shape):
    nd = len(shape)
    return pl.BlockSpec(shape, lambda *_: (0,) * nd, pipeline_mode=pl.Buffered(1))


def _proj_kernel(x_ref, gmix_ref, w_ref, wabt_ref, wconv_ref,
                 alog_row_ref, dtb_row_ref, alog_col_ref, dtb_col_ref, lng_ref, lnb_ref,
                 ws_ref, bst_ref,
                 qs_ref, k_ref, v_ref, zs_ref, sg_ref, col_ref, row_ref, cstate_ref,
                 cbuf):
    rows = x_ref.shape[0]
    first_tile = pl.program_id(1) == 0
    h = _rms(x_ref[...], gmix_ref[...]).astype(BF16)

    def project(lo, width=D_GDN):
        return _dot(h, w_ref[:, lo:lo + width])

    def conv_silu(pre, lo):
        cols = slice(lo, lo + D_GDN)
        groups = rows // SUBLANES
        tap = lambda j: wconv_ref[j:j + 1, cols].reshape(1, 1, D_GDN)
        shape3 = (1, SUBLANES, D_GDN)
        last_p = jnp.where(first_tile, 0.0, cbuf[0:SUBLANES, cols]).reshape(shape3)
        last_b = jnp.where(first_tile, 0.0, cbuf[SUBLANES:, cols]).reshape(shape3)
        cstate_ref[:, cols] = pre[rows - (CONV_W - 1):rows, :]
        cur = pre.reshape(groups, SUBLANES, D_GDN)
        sub = lax.broadcasted_iota(jnp.int32, (CONV_BLOCK, SUBLANES, D_GDN), 1)

        def delay(a, a_last, s):
            rolled = pltpu.roll(a, s, axis=1)
            rolled_prev = jnp.concatenate([pltpu.roll(a_last, s, axis=1), rolled[:-1]], axis=0)
            return jnp.where(sub >= s, rolled, rolled_prev)

        assert CONV_W == 4
        out = []
        for g0 in range(0, groups, CONV_BLOCK):
            p = cur[g0:g0 + CONV_BLOCK]
            dp = delay(p, last_p, 1)
            b = p * tap(1) + dp * tap(0)
            y = p * tap(3) + dp * tap(2) + delay(b, last_b, 2)
            last_p, last_b = p[CONV_BLOCK - 1:], b[CONV_BLOCK - 1:]
            out.append(_silu(y))
        cbuf[0:SUBLANES, cols] = last_p.reshape(SUBLANES, D_GDN)
        cbuf[SUBLANES:, cols] = last_b.reshape(SUBLANES, D_GDN)
        return jnp.concatenate(out, axis=0).reshape(rows, D_GDN)

    pre_q = project(COL_Q)
    pre_k = project(COL_K)
    y = conv_silu(pre_q, COL_Q)
    for hh in range(HEADS):
        lo = hh * DH
        qs_ref[:, lo:lo + DH] = (_l2n(y[:, lo:lo + DH]) * (DH ** -0.5)).astype(BF16)
    pre_v = project(COL_V)
    y = conv_silu(pre_k, COL_K)
    for hh in range(HEADS):
        lo = hh * DH
        k_ref[:, lo:lo + DH] = _l2n(y[:, lo:lo + DH]).astype(BF16)
    pre_z = project(COL_Z)
    v_ref[...] = conv_silu(pre_v, COL_V).astype(BF16)
    pre_u = project(COL_U)
    zs_ref[...] = _silu(pre_z).astype(BF16)
    pre_vv = project(COL_VV)
    u_act = _gelu_tanh(pre_u)

    ab = project(COL_AB, LANES)
    lane = lax.broadcasted_iota(jnp.int32, (GDN_CHUNK, LANES), 1)
    g_col = -jnp.exp(alog_row_ref[...]) * _softplus(ab + dtb_row_ref[...])
    abt = _dot_nt(wabt_ref[...], h)[0:2 * HEADS, :]
    sub = lax.broadcasted_iota(jnp.int32, (2 * HEADS, GDN_CHUNK), 0)
    g_row = -jnp.exp(alog_col_ref[...]) * _softplus(abt + dtb_col_ref[...])
    ri = lax.broadcasted_iota(jnp.int32, (GDN_CHUNK, GDN_CHUNK), 0)
    ci = lax.broadcasted_iota(jnp.int32, (GDN_CHUNK, GDN_CHUNK), 1)
    tri_l = (ri >= ci).astype(BF16)
    tri_u = (ri <= ci).astype(BF16)
    beta_col = _sigmoid(ab)
    beta_row = _sigmoid(abt)
    for c in range(rows // GDN_CHUNK):
        r0 = c * GDN_CHUNK
        p0, p1, p2 = _split3(g_col[r0:r0 + GDN_CHUNK, :])
        gc = _dot(tri_l, p0) + _dot(tri_l, p1) + _dot(tri_l, p2)
        col_ref[r0:r0 + GDN_CHUNK, :] = jnp.where(lane < HEADS, beta_col[r0:r0 + GDN_CHUNK, :], gc)
        p0, p1, p2 = _split3(g_row[:, r0:r0 + GDN_CHUNK])
        gr = _dot(p0, tri_u) + _dot(p1, tri_u) + _dot(p2, tri_u)
        row_ref[c] = jnp.where(sub < HEADS, beta_row[:, r0:r0 + GDN_CHUNK], gr)

    vg = _layer_norm(_gelu_tanh(pre_vv), lng_ref[...], lnb_ref[...]).astype(BF16)
    ti = lax.broadcasted_iota(jnp.int32, (SGU_CHUNK, SGU_CHUNK), 0)
    si = lax.broadcasted_iota(jnp.int32, (SGU_CHUNK, SGU_CHUNK), 1)
    for hh in range(HEADS):
        lo = hh * DH
        w_tril = jnp.where(ti >= si, ws_ref[hh], 0.0).astype(BF16)
        bias = bst_ref[:, hh:hh + 1]
        for c in range(rows // SGU_CHUNK):
            r0 = c * SGU_CHUNK
            mix = _dot(w_tril, vg[r0:r0 + SGU_CHUNK, lo:lo + DH]) + bias
            sg_ref[r0:r0 + SGU_CHUNK, lo:lo + DH] = (u_act[r0:r0 + SGU_CHUNK, lo:lo + DH] * mix).astype(BF16)


def _proj_call(x, gmix, w_all, wabt, wconv, alog_row, dtb_row, alog_col, dtb_col, lng, lnb, ws, bst):
    b, l, _ = x.shape
    rows = PROJ_ROWS
    nt = l // rows
    nchunk = rows // GDN_CHUNK
    tile = lambda w: pl.BlockSpec((None, rows, w), lambda i, j: (i, j, 0))
    ins = [x, gmix, w_all, wabt, wconv, alog_row, dtb_row, alog_col, dtb_col, lng, lnb, ws, bst]
    in_specs = [tile(D_MODEL)] + [_resident(a.shape) for a in ins[1:]]
    out_shape = [jax.ShapeDtypeStruct((b, l, D_GDN), BF16)] * 5 + [
        jax.ShapeDtypeStruct((b, l, LANES), F32),
        jax.ShapeDtypeStruct((b, l // GDN_CHUNK, 2 * HEADS, GDN_CHUNK), F32),
        jax.ShapeDtypeStruct((b, CONV_W - 1, D_QKV), F32),
    ]
    out_specs = [tile(D_GDN)] * 5 + [
        tile(LANES),
        pl.BlockSpec((None, nchunk, 2 * HEADS, GDN_CHUNK), lambda i, j: (i, j, 0, 0)),
        pl.BlockSpec((None, CONV_W - 1, D_QKV), lambda i, j: (i, 0, 0)),
    ]
    return pl.pallas_call(
        _proj_kernel,
        grid=(b, nt),
        in_specs=in_specs,
        out_specs=out_specs,
        out_shape=out_shape,
        scratch_shapes=[pltpu.VMEM((2 * SUBLANES, D_QKV), F32)],
        compiler_params=pltpu.CompilerParams(
            dimension_semantics=("arbitrary", "arbitrary"), vmem_limit_bytes=VMEM_LIMIT),
        name="proj",
    )(*ins)


def _neumann_many(a_list):
    c = a_list[0].shape[0]
    p = [-a for a in a_list]
    pb = [x.astype(BF16) for x in p]
    q = [_dot(x, x) for x in pb]
    power = 2
    while 2 * power < c:
        qb = [x.astype(BF16) for x in q]
        r = [_dot(jnp.concatenate([pi.astype(BF16), qi], axis=0), qi) for pi, qi in zip(p, qb)]
        p = [pi + qi + ri[:c] for pi, qi, ri in zip(p, q, r)]
        q = [ri[c:] for ri in r]
        power *= 2
    tail = [_dot(pi.astype(BF16), qi.astype(BF16)) for pi, qi in zip(p, q)]
    return [pi + qi + ti for pi, qi, ti in zip(p, q, tail)]


def _gdn_kernel(qs_ref, k_ref, v_ref, col_ref, row_ref, zs_ref, gnorm_ref,
                og_ref, state_ref, s_scr):
    rows = qs_ref.shape[0]
    c_ = GDN_CHUNK
    t = pl.program_id(1)

    @pl.when(t == 0)
    def _():
        s_scr[...] = jnp.zeros(s_scr.shape, F32)

    ri = lax.broadcasted_iota(jnp.int32, (c_, c_), 0)
    ci = lax.broadcasted_iota(jnp.int32, (c_, c_), 1)
    incl = ri >= ci
    strict = ri > ci
    gnorm = gnorm_ref[...]

    pairs = [(c, hh) for c in range(rows // c_) for hh in range(HEADS)]
    cols = [col_ref[c * c_:(c + 1) * c_, :] for c in range(rows // c_)]
    rws = [row_ref[c] for c in range(rows // c_)]
    beta = [cols[c][:, hh:hh + 1] for c, hh in pairs]
    gc = [cols[c][:, HEADS + hh:HEADS + hh + 1] for c, hh in pairs]
    gr = [rws[c][HEADS + hh:HEADS + hh + 1, :] for c, hh in pairs]
    glast = [cols[c][c_ - 1:c_, HEADS + hh:HEADS + hh + 1] for c, hh in pairs]
    kh = [k_ref[c * c_:(c + 1) * c_, hh * DH:(hh + 1) * DH] for c, hh in pairs]
    qh = [qs_ref[c * c_:(c + 1) * c_, hh * DH:(hh + 1) * DH] for c, hh in pairs]
    kf = [x.astype(F32) for x in kh]
    kb = [x * b for x, b in zip(kf, beta)]
    m1 = [_dot_nt(jnp.concatenate([x.astype(BF16), q], axis=0), k) for x, q, k in zip(kb, qh, kh)]
    decay = [jnp.exp(jnp.where(incl, a - b, -jnp.inf)) for a, b in zip(gc, gr)]
    attn = [(m[c_:] * d).astype(BF16) for m, d in zip(m1, decay)]
    p = _neumann_many([jnp.where(strict, m[:c_] * d, 0.0) for m, d in zip(m1, decay)])
    eg = [jnp.exp(x) for x in gc]
    rhs = [jnp.concatenate([x * e, v_ref[c * c_:(c + 1) * c_, hh * DH:(hh + 1) * DH].astype(F32) * b], axis=1)
           for x, e, b, (c, hh) in zip(kb, eg, beta, pairs)]
    wu = [(x + _dot(pi.astype(BF16), x.astype(BF16))).astype(BF16) for x, pi in zip(rhs, p)]
    ke_t = [jnp.transpose((x * jnp.exp(gl - g)).astype(BF16)) for x, gl, g in zip(kf, glast, gc)]
    r1 = [_dot(jnp.concatenate([kt, a], axis=0), x) for kt, a, x in zip(ke_t, attn, wu)]
    stage = {}
    for i, pair in enumerate(pairs):
        q_eff = qh[i].astype(F32) * eg[i] - r1[i][DH:, :DH]
        mq = jnp.concatenate([(-r1[i][:DH, :DH]).astype(BF16), q_eff.astype(BF16)], axis=0)
        stage[pair] = (mq, r1[i][:DH, DH:], r1[i][DH:, DH:], jnp.exp(glast[i]))

    state = [s_scr[hh] for hh in range(HEADS)]
    for c in range(rows // c_):
        r0 = c * c_
        for hh in range(HEADS):
            lo = hh * DH
            mq, n_mat, o_zero, e_last = stage[c, hh]
            r = _dot(mq, state[hh].astype(BF16))
            state[hh] = state[hh] * e_last + r[:DH] + n_mat
            og = _rms(r[DH:] + o_zero, gnorm) * zs_ref[r0:r0 + c_, lo:lo + DH].astype(F32)
            og_ref[r0:r0 + c_, lo:lo + DH] = og.astype(BF16)
    for hh in range(HEADS):
        s_scr[hh] = state[hh]

    @pl.when(t == pl.num_programs(1) - 1)
    def _():
        state_ref[...] = s_scr[...]


def _gdn_call(qs, k, v, col, row, zs, gnorm):
    b, l, _ = qs.shape
    rows = GDN_ROWS
    nt = l // rows
    nchunk = rows // GDN_CHUNK
    tile = lambda w: pl.BlockSpec((None, rows, w), lambda i, j: (i, j, 0))
    return pl.pallas_call(
        _gdn_kernel,
        grid=(b, nt),
        in_specs=[tile(D_GDN)] * 3 + [
            tile(LANES),
            pl.BlockSpec((None, nchunk, 2 * HEADS, GDN_CHUNK), lambda i, j: (i, j, 0, 0)),
            tile(D_GDN),
            _resident(gnorm.shape),
        ],
        out_specs=[tile(D_GDN), pl.BlockSpec((None, HEADS, DH, DH), lambda i, j: (i, 0, 0, 0))],
        out_shape=[jax.ShapeDtypeStruct((b, l, D_GDN), BF16),
                   jax.ShapeDtypeStruct((b, HEADS, DH, DH), F32)],
        scratch_shapes=[pltpu.VMEM((HEADS, DH, DH), F32)],
        compiler_params=pltpu.CompilerParams(
            dimension_semantics=("arbitrary", "arbitrary"), vmem_limit_bytes=VMEM_LIMIT),
        name="gdn",
    )(qs, k, v, col, row, zs, gnorm)


def _tail_kernel(x_ref, og_ref, sg_ref, p_ref, wout_ref, gff_ref, wup_ref, wdown_ref,
                 gple_ref, wple_ref, wgate_ref, gfin_ref, y_ref):
    x1 = x_ref[...] + _dot(og_ref[...], wout_ref[0:D_GDN, :]) + _dot(sg_ref[...], wout_ref[D_GDN:, :])
    h2 = _rms(x1, gff_ref[...]).astype(BF16)
    y_ref[...] = x1
    ff_block = D_FF // 4
    for c in range(D_FF // ff_block):
        lo = c * ff_block
        hid = jnp.maximum(_dot(h2, wup_ref[:, lo:lo + ff_block]), 0.0)
        y_ref[...] += _dot((hid * hid).astype(BF16), wdown_ref[lo:lo + ff_block, :])
    acc = y_ref[...]
    h3 = _rms(acc, gple_ref[...]).astype(BF16)
    gate = _sigmoid(_dot(h3, wgate_ref[...]))
    x3 = acc + _dot(p_ref[...].astype(BF16), wple_ref[...]) * gate
    y_ref[...] = _rms(x3, gfin_ref[...])


def _tail_call(x, og, sg, p, wout, gff, wup, wdown, gple, wple, wgate, gfin, rows):
    n = x.shape[0]
    tile = lambda w: pl.BlockSpec((rows, w), lambda i: (i, 0))
    weights = [wout, gff, wup, wdown, gple, wple, wgate, gfin]
    return pl.pallas_call(
        _tail_kernel,
        grid=(n // rows,),
        in_specs=[tile(D_MODEL), tile(D_GDN), tile(D_SGU), tile(D_PLE)] + [_resident(w.shape) for w in weights],
        out_specs=tile(D_MODEL),
        out_shape=jax.ShapeDtypeStruct((n, D_MODEL), F32),
        compiler_params=pltpu.CompilerParams(
            dimension_semantics=("arbitrary",), vmem_limit_bytes=VMEM_LIMIT),
        name="tail",
    )(x, og, sg, p, *weights)


def _sample_proj_kernel(x_ref, hist_ref, gmix_ref, w_ref, wconv_ref,
                        alog_row_ref, dtb_row_ref, lng_ref, lnb_ref, ws_ref, bs_ref,
                        nconv_ref, vg_ref, sg_ref, zs_ref, v_ref, col_ref, qt_ref, kt_ref):
    n = x_ref.shape[0]
    h = _rms(x_ref[...], gmix_ref[...]).astype(BF16)
    pre = _dot(h, w_ref[:, COL_Q:COL_Q + D_QKV])
    y = pre * wconv_ref[CONV_W - 1:CONV_W, :]
    for j in range(CONV_W - 1):
        y = y + hist_ref[j] * wconv_ref[j:j + 1, :]
        if j > 0:
            nconv_ref[j - 1] = hist_ref[j]
    nconv_ref[CONV_W - 2] = pre
    y = _silu(y)
    v_ref[...] = y[:, 2 * D_GDN:]
    blocks = n // SAMPLE_BLOCK
    for hh in range(HEADS):
        lo = hh * DH
        qt = jnp.transpose(_l2n(y[:, lo:lo + DH]) * (DH ** -0.5))
        kt = jnp.transpose(_l2n(y[:, D_GDN + lo:D_GDN + lo + DH]))
        for i in range(blocks):
            qt_ref[i, hh] = qt[:, i * SAMPLE_BLOCK:(i + 1) * SAMPLE_BLOCK]
            kt_ref[i, hh] = kt[:, i * SAMPLE_BLOCK:(i + 1) * SAMPLE_BLOCK]
    zs_ref[...] = _silu(_dot(h, w_ref[:, COL_Z:COL_Z + D_GDN]))
    ab = _dot(h, w_ref[:, COL_AB:COL_AB + LANES])
    lane = lax.broadcasted_iota(jnp.int32, ab.shape, 1)
    g_col = -jnp.exp(alog_row_ref[...]) * _softplus(ab + dtb_row_ref[...])
    col_ref[...] = jnp.where(lane < HEADS, _sigmoid(ab), g_col)
    uv = _gelu_tanh(_dot(h, w_ref[:, COL_U:COL_U + 2 * D_SGU]))
    vg = _layer_norm(uv[:, D_SGU:], lng_ref[...], lnb_ref[...])
    vg_ref[...] = vg
    for hh in range(HEADS):
        lo = hh * DH
        mix = ws_ref[hh, 0:1, 0:1] * vg[:, lo:lo + DH] + bs_ref[hh:hh + 1, 0:1]
        sg_ref[:, lo:lo + DH] = (uv[:, lo:lo + DH] * mix).astype(BF16)


def _sample_proj_call(x, hist, gmix, w_all, wconv, alog_row, dtb_row, lng, lnb, ws, bs):
    n = x.shape[0]
    blocks = n // SAMPLE_BLOCK
    out_shape = [
        jax.ShapeDtypeStruct((CONV_W - 1, n, D_QKV), F32),
        jax.ShapeDtypeStruct((n, D_SGU), F32),
        jax.ShapeDtypeStruct((n, D_SGU), BF16),
        jax.ShapeDtypeStruct((n, D_GDN), F32),
        jax.ShapeDtypeStruct((n, D_GDN), F32),
        jax.ShapeDtypeStruct((n, LANES), F32),
        jax.ShapeDtypeStruct((blocks, HEADS, DH, SAMPLE_BLOCK), F32),
        jax.ShapeDtypeStruct((blocks, HEADS, DH, SAMPLE_BLOCK), F32),
    ]
    return pl.pallas_call(
        _sample_proj_kernel,
        out_shape=out_shape,
        compiler_params=pltpu.CompilerParams(vmem_limit_bytes=VMEM_LIMIT),
        name="sample_proj",
    )(x, hist, gmix, w_all, wconv, alog_row, dtb_row, lng, lnb, ws, bs)


def _sample_gdn_kernel(s_ref, qt_ref, kt_ref, v_ref, col_ref, zs_ref, gnorm_ref, og_ref, snew_ref):
    gnorm = gnorm_ref[...]
    for b0 in range(0, SAMPLE_BLOCK, SAMPLE_GROUP):
        pairs = [(bb, hh) for bb in range(b0, b0 + SAMPLE_GROUP) for hh in range(HEADS)]
        kcol = [kt_ref[hh, :, bb:bb + 1] for bb, hh in pairs]
        qcol = [qt_ref[hh, :, bb:bb + 1] for bb, hh in pairs]
        beta = [col_ref[bb:bb + 1, hh:hh + 1] for bb, hh in pairs]
        eg = [jnp.exp(col_ref[bb:bb + 1, HEADS + hh:HEADS + hh + 1]) for bb, hh in pairs]
        sd = [s_ref[bb, hh] * e for (bb, hh), e in zip(pairs, eg)]
        sk = [jnp.sum(s * kc, axis=0, keepdims=True) for s, kc in zip(sd, kcol)]
        v_new = [b * (v_ref[bb:bb + 1, hh * DH:(hh + 1) * DH] - x) for (bb, hh), b, x in zip(pairs, beta, sk)]
        s1 = [s + kc * vn for s, kc, vn in zip(sd, kcol, v_new)]
        for (bb, hh), s in zip(pairs, s1):
            snew_ref[bb, hh] = s
        o = [jnp.sum(s * qc, axis=0, keepdims=True) for s, qc in zip(s1, qcol)]
        for (bb, hh), x in zip(pairs, o):
            lo = hh * DH
            og = _rms(x, gnorm) * zs_ref[bb:bb + 1, lo:lo + DH]
            og_ref[bb:bb + 1, lo:lo + DH] = og.astype(BF16)


def _sample_gdn_call(state, qt, kt, v, col, zs, gnorm):
    n = state.shape[0]
    blk = SAMPLE_BLOCK
    rowtile = lambda w: pl.BlockSpec((blk, w), lambda i: (i, 0))
    return pl.pallas_call(
        _sample_gdn_kernel,
        grid=(n // blk,),
        in_specs=[
            pl.BlockSpec((blk, HEADS, DH, DH), lambda i: (i, 0, 0, 0)),
            pl.BlockSpec((None, HEADS, DH, blk), lambda i: (i, 0, 0, 0)),
            pl.BlockSpec((None, HEADS, DH, blk), lambda i: (i, 0, 0, 0)),
            rowtile(D_GDN), rowtile(LANES), rowtile(D_GDN),
            _resident(gnorm.shape),
        ],
        out_specs=[rowtile(D_GDN), pl.BlockSpec((blk, HEADS, DH, DH), lambda i: (i, 0, 0, 0))],
        out_shape=[jax.ShapeDtypeStruct((n, D_GDN), BF16),
                   jax.ShapeDtypeStruct((n, HEADS, DH, DH), F32)],
        compiler_params=pltpu.CompilerParams(
            dimension_semantics=("arbitrary",), vmem_limit_bytes=VMEM_LIMIT),
        name="sample_gdn",
    )(state, qt, kt, v, col, zs, gnorm)


def kernel(x_prompt, x_sample, state_conv, state_gdn, p_prompt, p_sample, g_mix, w_in, w_conv, a_log, dt_bias, gdn_norm, sgu_ln_g, sgu_ln_b, w_s, b_s, w_out, g_ff, w_up, w_down, g_ple, w_ple, w_ple_gate, g_final):
    depth = w_in.shape[0]
    assert depth == 1, "single-layer trunk only"
    b, l, d = x_prompt.shape
    n = x_sample.shape[0]
    assert x_sample.shape[1] == 1 and d == D_MODEL
    assert l % PROJ_ROWS == 0 and l % GDN_ROWS == 0 and (b * l) % TAIL_ROWS == 0 and n % SAMPLE_BLOCK == 0

    o1, o3 = D_QKV + D_GDN, D_QKV + D_GDN + 2 * HEADS
    w_in0 = w_in[0]
    wab8 = w_in0[:, o1:o3]
    w_all = jnp.concatenate(
        [w_in0[:, :o1], w_in0[:, o3:], wab8, jnp.zeros((d, LANES - 2 * HEADS), F32)], axis=1).astype(BF16)
    assert w_all.shape == (d, W_COLS)
    wabt = jnp.pad(wab8.T, ((0, 2 * SUBLANES - 2 * HEADS), (0, 0))).astype(BF16)
    row2 = lambda a: a.reshape(1, -1)
    pad_lanes = lambda a: jnp.pad(a, (HEADS, LANES - 2 * HEADS)).reshape(1, LANES)
    pad_rows = lambda a: jnp.pad(a, (HEADS, 0)).reshape(2 * HEADS, 1)
    alog_row, dtb_row = pad_lanes(a_log[0]), pad_lanes(dt_bias[0])
    alog_col, dtb_col = pad_rows(a_log[0]), pad_rows(dt_bias[0])
    gmix, gff, gple, gfin = row2(g_mix[0]), row2(g_ff[0]), row2(g_ple[0]), row2(g_final)
    lng, lnb, gnorm = row2(sgu_ln_g[0]), row2(sgu_ln_b[0]), row2(gdn_norm[0])
    wout, wup, wdown = w_out[0].astype(BF16), w_up[0].astype(BF16), w_down[0].astype(BF16)
    wple, wgate = w_ple[0].astype(BF16), w_ple_gate[0].astype(BF16)
    tail_w = (wout, gff, wup, wdown, gple, wple, wgate, gfin)

    qs, k, v, zs, sg, col, row, conv_p = _proj_call(
        x_prompt, gmix, w_all, wabt, w_conv[0], alog_row, dtb_row, alog_col, dtb_col,
        lng, lnb, w_s[0], b_s[0].T)
    og, gdn_p = _gdn_call(qs, k, v, col, row, zs, gnorm)
    y_prompt = _tail_call(x_prompt.reshape(b * l, d), og.reshape(b * l, D_GDN), sg.reshape(b * l, D_SGU),
                          p_prompt[0].reshape(b * l, D_PLE), *tail_w, rows=TAIL_ROWS).reshape(b, l, d)

    nconv, vg_s, sg_s, zs_s, v_s, col_s, qt, kt = _sample_proj_call(
        x_sample.reshape(n, d), jnp.transpose(state_conv[0], (1, 0, 2)), gmix, w_all,
        w_conv[0], alog_row, dtb_row, lng, lnb, w_s[0], b_s[0])
    og_s, gdn_s = _sample_gdn_call(state_gdn[0], qt, kt, v_s, col_s, zs_s, gnorm)
    y_sample = _tail_call(x_sample.reshape(n, d), og_s, sg_s, p_sample[0].reshape(n, D_PLE), *tail_w,
                          rows=n).reshape(n, 1, d)

    return (y_prompt, y_sample,
            conv_p[None], gdn_p[None],
            jnp.transpose(nconv, (1, 0, 2))[None], gdn_s[None],
            vg_s.reshape(n, 1, D_SGU)[None])
```

```python
import functools

import jax
import jax.numpy as jnp
from jax import lax
from jax.experimental import pallas as pl
from jax.experimental.pallas import tpu as pltpu

F32 = jnp.float32
BF16 = jnp.bfloat16

D_MODEL = 1024
D_GDN = 512
D_SGU = 512
HEADS = 4
DH = 128
D_QKV = 3 * D_GDN
D_FF = 4 * D_MODEL
D_PLE = 256
CONV_W = 4
GDN_CHUNK = 64
SGU_CHUNK = 128
EPS = 1e-6
LANES = 128
SUBLANES = 8
VMEM_LIMIT = 56 * 1024 * 1024

COL_Q, COL_K, COL_V, COL_Z, COL_U, COL_VV, COL_AB = 0, 512, 1024, 1536, 2048, 2560, 3072
W_COLS = COL_AB + LANES

PROJ_ROWS = 512
CONV_BLOCK = 8
GDN_ROWS = 1024
GDN_GROUP = 4
TAIL_ROWS = 512
SAMPLE_BLOCK = 8
SAMPLE_GROUP = 2


def _dot(a, b):
    return jnp.dot(a, b, preferred_element_type=F32)


def _dot_nt(a, b):
    return lax.dot_general(a, b, (((1,), (1,)), ((), ())), preferred_element_type=F32)


def _dot_tn(a, b):
    return lax.dot_general(a, b, (((0,), (0,)), ((), ())), preferred_element_type=F32)


def _split3(x):
    hi = x.astype(BF16)
    r1 = x - hi.astype(F32)
    mid = r1.astype(BF16)
    lo = (r1 - mid.astype(F32)).astype(BF16)
    return hi, mid, lo


def _rms(x, gain):
    return x * lax.rsqrt(jnp.mean(x * x, axis=-1, keepdims=True) + EPS) * gain


def _sigmoid(x):
    return 1.0 / (1.0 + jnp.exp(-x))


def _silu(x):
    hx = 0.5 * x
    return hx + hx * jnp.tanh(hx)


def _softplus(x):
    return jnp.maximum(x, 0.0) + jnp.log(1.0 + jnp.exp(-jnp.abs(x)))


def _gelu_tanh(x):
    c = 0.7978845608028654
    hx = 0.5 * x
    return hx + hx * jnp.tanh(x * (c + (c * 0.044715) * (x * x)))


def _l2n(x):
    return x * lax.rsqrt(jnp.sum(x * x, axis=-1, keepdims=True) + 1e-6)


def _layer_norm(x, gain, bias):
    mu = jnp.mean(x, axis=-1, keepdims=True)
    xc = x - mu
    var = jnp.mean(xc * xc, axis=-1, keepdims=True)
    return xc * lax.rsqrt(var + EPS) * gain + bias


def _resident(shape):
    nd = len(shape)
    return pl.BlockSpec(shape, lambda *_: (0,) * nd, pipeline_mode=pl.Buffered(1))


def _proj_kernel(x_ref, gmix_ref, w_ref, wabt_ref, wconv_ref,
                 alog_row_ref, dtb_row_ref, alog_col_ref, dtb_col_ref, lng_ref, lnb_ref,
                 ws_ref, bst_ref,
                 qs_ref, k_ref, v_ref, zs_ref, sg_ref, col_ref, row_ref, cstate_ref,
                 cbuf):
    rows = x_ref.shape[0]
    first_tile = pl.program_id(1) == 0
    h = _rms(x_ref[...], gmix_ref[...]).astype(BF16)

    def project(lo, width=D_GDN):
        return _dot(h, w_ref[:, lo:lo + width])

    def conv_silu(pre, lo):
        cols = slice(lo, lo + D_GDN)
        groups = rows // SUBLANES
        tap = lambda j: wconv_ref[j:j + 1, cols].reshape(1, 1, D_GDN)
        shape3 = (1, SUBLANES, D_GDN)
        last_p = jnp.where(first_tile, 0.0, cbuf[0:SUBLANES, cols]).reshape(shape3)
        last_b = jnp.where(first_tile, 0.0, cbuf[SUBLANES:, cols]).reshape(shape3)
        cstate_ref[:, cols] = pre[rows - (CONV_W - 1):rows, :]
        cur = pre.reshape(groups, SUBLANES, D_GDN)
        sub = lax.broadcasted_iota(jnp.int32, (CONV_BLOCK, SUBLANES, D_GDN), 1)

        def delay(a, a_last, s):
            rolled = pltpu.roll(a, s, axis=1)
            rolled_prev = jnp.concatenate([pltpu.roll(a_last, s, axis=1), rolled[:-1]], axis=0)
            return jnp.where(sub >= s, rolled, rolled_prev)

        assert CONV_W == 4
        out = []
        for g0 in range(0, groups, CONV_BLOCK):
            p = cur[g0:g0 + CONV_BLOCK]
            dp = delay(p, last_p, 1)
            b = p * tap(1) + dp * tap(0)
            y = p * tap(3) + dp * tap(2) + delay(b, last_b, 2)
            last_p, last_b = p[CONV_BLOCK - 1:], b[CONV_BLOCK - 1:]
            out.append(_silu(y))
        cbuf[0:SUBLANES, cols] = last_p.reshape(SUBLANES, D_GDN)
        cbuf[SUBLANES:, cols] = last_b.reshape(SUBLANES, D_GDN)
        return jnp.concatenate(out, axis=0).reshape(rows, D_GDN)

    pre_q = project(COL_Q)
    pre_k = project(COL_K)
    y = conv_silu(pre_q, COL_Q)
    for hh in range(HEADS):
        lo = hh * DH
        qs_ref[:, lo:lo + DH] = (_l2n(y[:, lo:lo + DH]) * (DH ** -0.5)).astype(BF16)
    pre_v = project(COL_V)
    y = conv_silu(pre_k, COL_K)
    for hh in range(HEADS):
        lo = hh * DH
        k_ref[:, lo:lo + DH] = _l2n(y[:, lo:lo + DH]).astype(BF16)
    pre_z = project(COL_Z)
    v_ref[...] = conv_silu(pre_v, COL_V).astype(BF16)
    pre_u = project(COL_U)
    zs_ref[...] = _silu(pre_z).astype(BF16)
    pre_vv = project(COL_VV)
    u_act = _gelu_tanh(pre_u)

    ab = project(COL_AB, LANES)
    lane = lax.broadcasted_iota(jnp.int32, (GDN_CHUNK, LANES), 1)
    g_col = -jnp.exp(alog_row_ref[...]) * _softplus(ab + dtb_row_ref[...])
    abt = _dot_nt(wabt_ref[...], h)[0:2 * HEADS, :]
    sub = lax.broadcasted_iota(jnp.int32, (2 * HEADS, GDN_CHUNK), 0)
    g_row = -jnp.exp(alog_col_ref[...]) * _softplus(abt + dtb_col_ref[...])
    ri = lax.broadcasted_iota(jnp.int32, (GDN_CHUNK, GDN_CHUNK), 0)
    ci = lax.broadcasted_iota(jnp.int32, (GDN_CHUNK, GDN_CHUNK), 1)
    tri_l = (ri >= ci).astype(BF16)
    tri_u = (ri <= ci).astype(BF16)
    beta_col = _sigmoid(ab)
    beta_row = _sigmoid(abt)
    for c in range(rows // GDN_CHUNK):
        r0 = c * GDN_CHUNK
        p0, p1, p2 = _split3(g_col[r0:r0 + GDN_CHUNK, :])
        gc = _dot(tri_l, p0) + _dot(tri_l, p1) + _dot(tri_l, p2)
        col_ref[r0:r0 + GDN_CHUNK, :] = jnp.where(lane < HEADS, beta_col[r0:r0 + GDN_CHUNK, :], gc)
        p0, p1, p2 = _split3(g_row[:, r0:r0 + GDN_CHUNK])
        gr = _dot(p0, tri_u) + _dot(p1, tri_u) + _dot(p2, tri_u)
        row_ref[c] = jnp.where(sub < HEADS, beta_row[:, r0:r0 + GDN_CHUNK], gr)

    vg = _layer_norm(_gelu_tanh(pre_vv), lng_ref[...], lnb_ref[...]).astype(BF16)
    ti = lax.broadcasted_iota(jnp.int32, (SGU_CHUNK, SGU_CHUNK), 0)
    si = lax.broadcasted_iota(jnp.int32, (SGU_CHUNK, SGU_CHUNK), 1)
    for hh in range(HEADS):
        lo = hh * DH
        w_tril = jnp.where(ti >= si, ws_ref[hh], 0.0).astype(BF16)
        bias = bst_ref[:, hh:hh + 1]
        for c in range(rows // SGU_CHUNK):
            r0 = c * SGU_CHUNK
            mix = _dot(w_tril, vg[r0:r0 + SGU_CHUNK, lo:lo + DH]) + bias
            sg_ref[r0:r0 + SGU_CHUNK, lo:lo + DH] = (u_act[r0:r0 + SGU_CHUNK, lo:lo + DH] * mix).astype(BF16)


def _proj_call(x, gmix, w_all, wabt, wconv, alog_row, dtb_row, alog_col, dtb_col, lng, lnb, ws, bst):
    b, l, _ = x.shape
    rows = PROJ_ROWS
    nt = l // rows
    nchunk = rows // GDN_CHUNK
    tile = lambda w: pl.BlockSpec((None, rows, w), lambda i, j: (i, j, 0))
    ins = [x, gmix, w_all, wabt, wconv, alog_row, dtb_row, alog_col, dtb_col, lng, lnb, ws, bst]
    in_specs = [tile(D_MODEL)] + [_resident(a.shape) for a in ins[1:]]
    out_shape = [jax.ShapeDtypeStruct((b, l, D_GDN), BF16)] * 5 + [
        jax.ShapeDtypeStruct((b, l, LANES), F32),
        jax.ShapeDtypeStruct((b, l // GDN_CHUNK, 2 * HEADS, GDN_CHUNK), F32),
        jax.ShapeDtypeStruct((b, CONV_W - 1, D_QKV), F32),
    ]
    out_specs = [tile(D_GDN)] * 5 + [
        tile(LANES),
        pl.BlockSpec((None, nchunk, 2 * HEADS, GDN_CHUNK), lambda i, j: (i, j, 0, 0)),
        pl.BlockSpec((None, CONV_W - 1, D_QKV), lambda i, j: (i, 0, 0)),
    ]
    return pl.pallas_call(
        _proj_kernel,
        grid=(b, nt),
        in_specs=in_specs,
        out_specs=out_specs,
        out_shape=out_shape,
        scratch_shapes=[pltpu.VMEM((2 * SUBLANES, D_QKV), F32)],
        compiler_params=pltpu.CompilerParams(
            dimension_semantics=("arbitrary", "arbitrary"), vmem_limit_bytes=VMEM_LIMIT),
        name="proj",
    )(*ins)


def _neumann_many(a_list):
    c = a_list[0].shape[0]
    p = [-a for a in a_list]
    pb = [x.astype(BF16) for x in p]
    q = [_dot(x, x) for x in pb]
    yield
    power = 2
    while 2 * power < c:
        qb = [x.astype(BF16) for x in q]
        r = [_dot(jnp.concatenate([pi.astype(BF16), qi], axis=0), qi) for pi, qi in zip(p, qb)]
        yield
        p = [pi + qi + ri[:c] for pi, qi, ri in zip(p, q, r)]
        q = [ri[c:] for ri in r]
        power *= 2
    tail = [_dot(pi.astype(BF16), qi.astype(BF16)) for pi, qi in zip(p, q)]
    yield
    return [pi + qi + ti for pi, qi, ti in zip(p, q, tail)]


def _interleave(steps, fillers):
    fillers = list(fillers)
    count = 0
    while True:
        try:
            next(steps)
        except StopIteration as stop:
            for f in fillers:
                f()
            return stop.value
        count += 1
        if fillers and count % 2 == 0:
            fillers.pop(0)()


def _gdn_kernel(qs_ref, k_ref, v_ref, col_ref, row_ref, zs_ref, gnorm_ref,
                og_ref, state_ref, s_scr):
    rows = qs_ref.shape[0]
    c_ = GDN_CHUNK
    t = pl.program_id(1)

    @pl.when(t == 0)
    def _():
        s_scr[...] = jnp.zeros(s_scr.shape, F32)

    ri = lax.broadcasted_iota(jnp.int32, (c_, c_), 0)
    ci = lax.broadcasted_iota(jnp.int32, (c_, c_), 1)
    incl = ri >= ci
    strict = ri > ci
    gnorm = gnorm_ref[...]

    def prepare(chunks):
        pairs = [(c, hh) for c in chunks for hh in range(HEADS)]
        cols = {c: col_ref[c * c_:(c + 1) * c_, :] for c in chunks}
        rws = {c: row_ref[c] for c in chunks}
        beta = [cols[c][:, hh:hh + 1] for c, hh in pairs]
        gc = [cols[c][:, HEADS + hh:HEADS + hh + 1] for c, hh in pairs]
        gr = [rws[c][HEADS + hh:HEADS + hh + 1, :] for c, hh in pairs]
        glast = [cols[c][c_ - 1:c_, HEADS + hh:HEADS + hh + 1] for c, hh in pairs]
        kh = [k_ref[c * c_:(c + 1) * c_, hh * DH:(hh + 1) * DH] for c, hh in pairs]
        qh = [qs_ref[c * c_:(c + 1) * c_, hh * DH:(hh + 1) * DH] for c, hh in pairs]
        kf = [x.astype(F32) for x in kh]
        kb = [x * b for x, b in zip(kf, beta)]
        m1 = [_dot_nt(jnp.concatenate([x.astype(BF16), q], axis=0), k) for x, q, k in zip(kb, qh, kh)]
        yield
        decay = [jnp.exp(jnp.where(incl, a - b, -jnp.inf)) for a, b in zip(gc, gr)]
        attn = [(m[c_:] * d).astype(BF16) for m, d in zip(m1, decay)]
        p = yield from _neumann_many([jnp.where(strict, m[:c_] * d, 0.0) for m, d in zip(m1, decay)])
        eg = [jnp.exp(x) for x in gc]
        rhs = [jnp.concatenate([x * e, v_ref[c * c_:(c + 1) * c_, hh * DH:(hh + 1) * DH].astype(F32) * b], axis=1)
               for x, e, b, (c, hh) in zip(kb, eg, beta, pairs)]
        wu = [(x + _dot(pi.astype(BF16), x.astype(BF16))).astype(BF16) for x, pi in zip(rhs, p)]
        yield
        ke_t = [jnp.transpose((x * jnp.exp(gl - g)).astype(BF16)) for x, gl, g in zip(kf, glast, gc)]
        r1 = [_dot(jnp.concatenate([kt, a], axis=0), x) for kt, a, x in zip(ke_t, attn, wu)]
        yield
        staged = {}
        for i, pair in enumerate(pairs):
            q_eff = qh[i].astype(F32) * eg[i] - r1[i][DH:, :DH]
            mq = jnp.concatenate([(-r1[i][:DH, :DH]).astype(BF16), q_eff.astype(BF16)], axis=0)
            staged[pair] = (mq, r1[i][:DH, DH:], r1[i][DH:, DH:], jnp.exp(glast[i]))
        return staged

    state = [s_scr[hh] for hh in range(HEADS)]

    def advance(c, staged):
        r0 = c * c_
        for hh in range(HEADS):
            lo = hh * DH
            mq, n_mat, o_zero, e_last = staged[c, hh]
            r = _dot(mq, state[hh].astype(BF16))
            state[hh] = state[hh] * e_last + r[:DH] + n_mat
            og = _rms(r[DH:] + o_zero, gnorm) * zs_ref[r0:r0 + c_, lo:lo + DH].astype(F32)
            og_ref[r0:r0 + c_, lo:lo + DH] = og.astype(BF16)

    nchunk = rows // c_
    fillers = []
    for g0 in range(0, nchunk, GDN_GROUP):
        group = list(range(g0, g0 + GDN_GROUP))
        staged = _interleave(prepare(group), fillers)
        fillers = [functools.partial(advance, c, staged) for c in group]
    for f in fillers:
        f()
    for hh in range(HEADS):
        s_scr[hh] = state[hh]

    @pl.when(t == pl.num_programs(1) - 1)
    def _():
        state_ref[...] = s_scr[...]


def _gdn_call(qs, k, v, col, row, zs, gnorm):
    b, l, _ = qs.shape
    rows = GDN_ROWS
    nt = l // rows
    nchunk = rows // GDN_CHUNK
    tile = lambda w: pl.BlockSpec((None, rows, w), lambda i, j: (i, j, 0))
    return pl.pallas_call(
        _gdn_kernel,
        grid=(b, nt),
        in_specs=[tile(D_GDN)] * 3 + [
            tile(LANES),
            pl.BlockSpec((None, nchunk, 2 * HEADS, GDN_CHUNK), lambda i, j: (i, j, 0, 0)),
            tile(D_GDN),
            _resident(gnorm.shape),
        ],
        out_specs=[tile(D_GDN), pl.BlockSpec((None, HEADS, DH, DH), lambda i, j: (i, 0, 0, 0))],
        out_shape=[jax.ShapeDtypeStruct((b, l, D_GDN), BF16),
                   jax.ShapeDtypeStruct((b, HEADS, DH, DH), F32)],
        scratch_shapes=[pltpu.VMEM((HEADS, DH, DH), F32)],
        compiler_params=pltpu.CompilerParams(
            dimension_semantics=("arbitrary", "arbitrary"), vmem_limit_bytes=VMEM_LIMIT),
        name="gdn",
    )(qs, k, v, col, row, zs, gnorm)


def _tail_kernel(x_ref, og_ref, sg_ref, p_ref, wout_ref, gff_ref, wup_ref, wdown_ref,
                 gple_ref, wple_ref, wgate_ref, gfin_ref, y_ref):
    x1 = x_ref[...] + _dot(og_ref[...], wout_ref[0:D_GDN, :]) + _dot(sg_ref[...], wout_ref[D_GDN:, :])
    h2 = _rms(x1, gff_ref[...]).astype(BF16)
    y_ref[...] = x1
    ff_block = D_FF // 4
    for c in range(D_FF // ff_block):
        lo = c * ff_block
        hid = jnp.maximum(_dot(h2, wup_ref[:, lo:lo + ff_block]), 0.0)
        y_ref[...] += _dot((hid * hid).astype(BF16), wdown_ref[lo:lo + ff_block, :])
    acc = y_ref[...]
    h3 = _rms(acc, gple_ref[...]).astype(BF16)
    gate = _sigmoid(_dot(h3, wgate_ref[...]))
    x3 = acc + _dot(p_ref[...].astype(BF16), wple_ref[...]) * gate
    y_ref[...] = _rms(x3, gfin_ref[...])


def _tail_call(x, og, sg, p, wout, gff, wup, wdown, gple, wple, wgate, gfin, rows):
    n = x.shape[0]
    tile = lambda w: pl.BlockSpec((rows, w), lambda i: (i, 0))
    weights = [wout, gff, wup, wdown, gple, wple, wgate, gfin]
    return pl.pallas_call(
        _tail_kernel,
        grid=(n // rows,),
        in_specs=[tile(D_MODEL), tile(D_GDN), tile(D_SGU), tile(D_PLE)] + [_resident(w.shape) for w in weights],
        out_specs=tile(D_MODEL),
        out_shape=jax.ShapeDtypeStruct((n, D_MODEL), F32),
        compiler_params=pltpu.CompilerParams(
            dimension_semantics=("arbitrary",), vmem_limit_bytes=VMEM_LIMIT),
        name="tail",
    )(x, og, sg, p, *weights)


def _sample_proj_kernel(x_ref, hist_ref, gmix_ref, w_ref, wconv_ref,
                        alog_row_ref, dtb_row_ref, lng_ref, lnb_ref, ws_ref, bs_ref,
                        nconv_ref, vg_ref, sg_ref, zs_ref, v_ref, col_ref, qt_ref, kt_ref):
    n = x_ref.shape[0]
    h = _rms(x_ref[...], gmix_ref[...]).astype(BF16)
    pre = _dot(h, w_ref[:, COL_Q:COL_Q + D_QKV])
    y = pre * wconv_ref[CONV_W - 1:CONV_W, :]
    for j in range(CONV_W - 1):
        y = y + hist_ref[j] * wconv_ref[j:j + 1, :]
        if j > 0:
            nconv_ref[j - 1] = hist_ref[j]
    nconv_ref[CONV_W - 2] = pre
    y = _silu(y)
    v_ref[...] = y[:, 2 * D_GDN:]
    blocks = n // SAMPLE_BLOCK
    for hh in range(HEADS):
        lo = hh * DH
        qt = jnp.transpose(_l2n(y[:, lo:lo + DH]) * (DH ** -0.5))
        kt = jnp.transpose(_l2n(y[:, D_GDN + lo:D_GDN + lo + DH]))
        for i in range(blocks):
            qt_ref[i, hh] = qt[:, i * SAMPLE_BLOCK:(i + 1) * SAMPLE_BLOCK]
            kt_ref[i, hh] = kt[:, i * SAMPLE_BLOCK:(i + 1) * SAMPLE_BLOCK]
    zs_ref[...] = _silu(_dot(h, w_ref[:, COL_Z:COL_Z + D_GDN]))
    ab = _dot(h, w_ref[:, COL_AB:COL_AB + LANES])
    lane = lax.broadcasted_iota(jnp.int32, ab.shape, 1)
    g_col = -jnp.exp(alog_row_ref[...]) * _softplus(ab + dtb_row_ref[...])
    col_ref[...] = jnp.where(lane < HEADS, _sigmoid(ab), g_col)
    uv = _gelu_tanh(_dot(h, w_ref[:, COL_U:COL_U + 2 * D_SGU]))
    vg = _layer_norm(uv[:, D_SGU:], lng_ref[...], lnb_ref[...])
    vg_ref[...] = vg
    for hh in range(HEADS):
        lo = hh * DH
        mix = ws_ref[hh, 0:1, 0:1] * vg[:, lo:lo + DH] + bs_ref[hh:hh + 1, 0:1]
        sg_ref[:, lo:lo + DH] = (uv[:, lo:lo + DH] * mix).astype(BF16)


def _sample_proj_call(x, hist, gmix, w_all, wconv, alog_row, dtb_row, lng, lnb, ws, bs):
    n = x.shape[0]
    blocks = n // SAMPLE_BLOCK
    out_shape = [
        jax.ShapeDtypeStruct((CONV_W - 1, n, D_QKV), F32),
        jax.ShapeDtypeStruct((n, D_SGU), F32),
        jax.ShapeDtypeStruct((n, D_SGU), BF16),
        jax.ShapeDtypeStruct((n, D_GDN), F32),
        jax.ShapeDtypeStruct((n, D_GDN), F32),
        jax.ShapeDtypeStruct((n, LANES), F32),
        jax.ShapeDtypeStruct((blocks, HEADS, DH, SAMPLE_BLOCK), F32),
        jax.ShapeDtypeStruct((blocks, HEADS, DH, SAMPLE_BLOCK), F32),
    ]
    return pl.pallas_call(
        _sample_proj_kernel,
        out_shape=out_shape,
        compiler_params=pltpu.CompilerParams(vmem_limit_bytes=VMEM_LIMIT),
        name="sample_proj",
    )(x, hist, gmix, w_all, wconv, alog_row, dtb_row, lng, lnb, ws, bs)


def _sample_gdn_kernel(s_ref, qt_ref, kt_ref, v_ref, col_ref, zs_ref, gnorm_ref, og_ref, snew_ref):
    gnorm = gnorm_ref[...]
    for b0 in range(0, SAMPLE_BLOCK, SAMPLE_GROUP):
        pairs = [(bb, hh) for bb in range(b0, b0 + SAMPLE_GROUP) for hh in range(HEADS)]
        kcol = [kt_ref[hh, :, bb:bb + 1] for bb, hh in pairs]
        qcol = [qt_ref[hh, :, bb:bb + 1] for bb, hh in pairs]
        beta = [col_ref[bb:bb + 1, hh:hh + 1] for bb, hh in pairs]
        eg = [jnp.exp(col_ref[bb:bb + 1, HEADS + hh:HEADS + hh + 1]) for bb, hh in pairs]
        sd = [s_ref[bb, hh] * e for (bb, hh), e in zip(pairs, eg)]
        sk = [jnp.sum(s * kc, axis=0, keepdims=True) for s, kc in zip(sd, kcol)]
        v_new = [b * (v_ref[bb:bb + 1, hh * DH:(hh + 1) * DH] - x) for (bb, hh), b, x in zip(pairs, beta, sk)]
        s1 = [s + kc * vn for s, kc, vn in zip(sd, kcol, v_new)]
        for (bb, hh), s in zip(pairs, s1):
            snew_ref[bb, hh] = s
        o = [jnp.sum(s * qc, axis=0, keepdims=True) for s, qc in zip(s1, qcol)]
        for (bb, hh), x in zip(pairs, o):
            lo = hh * DH
            og = _rms(x, gnorm) * zs_ref[bb:bb + 1, lo:lo + DH]
            og_ref[bb:bb + 1, lo:lo + DH] = og.astype(BF16)


def _sample_gdn_call(state, qt, kt, v, col, zs, gnorm):
    n = state.shape[0]
    blk = SAMPLE_BLOCK
    rowtile = lambda w: pl.BlockSpec((blk, w), lambda i: (i, 0))
    return pl.pallas_call(
        _sample_gdn_kernel,
        grid=(n // blk,),
        in_specs=[
            pl.BlockSpec((blk, HEADS, DH, DH), lambda i: (i, 0, 0, 0)),
            pl.BlockSpec((None, HEADS, DH, blk), lambda i: (i, 0, 0, 0)),
            pl.BlockSpec((None, HEADS, DH, blk), lambda i: (i, 0, 0, 0)),
            rowtile(D_GDN), rowtile(LANES), rowtile(D_GDN),
            _resident(gnorm.shape),
        ],
        out_specs=[rowtile(D_GDN), pl.BlockSpec((blk, HEADS, DH, DH), lambda i: (i, 0, 0, 0))],
        out_shape=[jax.ShapeDtypeStruct((n, D_GDN), BF16),
                   jax.ShapeDtypeStruct((n, HEADS, DH, DH), F32)],
        compiler_params=pltpu.CompilerParams(
            dimension_semantics=("arbitrary",), vmem_limit_bytes=VMEM_LIMIT),
        name="sample_gdn",
    )(state, qt, kt, v, col, zs, gnorm)


def kernel(x_prompt, x_sample, state_conv, state_gdn, p_prompt, p_sample, g_mix, w_in, w_conv, a_log, dt_bias, gdn_norm, sgu_ln_g, sgu_ln_b, w_s, b_s, w_out, g_ff, w_up, w_down, g_ple, w_ple, w_ple_gate, g_final):
    depth = w_in.shape[0]
    assert depth == 1, "single-layer trunk only"
    b, l, d = x_prompt.shape
    n = x_sample.shape[0]
    assert x_sample.shape[1] == 1 and d == D_MODEL
    assert l % PROJ_ROWS == 0 and l % GDN_ROWS == 0 and (b * l) % TAIL_ROWS == 0 and n % SAMPLE_BLOCK == 0

    o1, o3 = D_QKV + D_GDN, D_QKV + D_GDN + 2 * HEADS
    w_in0 = w_in[0]
    wab8 = w_in0[:, o1:o3]
    w_all = jnp.concatenate(
        [w_in0[:, :o1], w_in0[:, o3:], wab8, jnp.zeros((d, LANES - 2 * HEADS), F32)], axis=1).astype(BF16)
    assert w_all.shape == (d, W_COLS)
    wabt = jnp.pad(wab8.T, ((0, 2 * SUBLANES - 2 * HEADS), (0, 0))).astype(BF16)
    row2 = lambda a: a.reshape(1, -1)
    pad_lanes = lambda a: jnp.pad(a, (HEADS, LANES - 2 * HEADS)).reshape(1, LANES)
    pad_rows = lambda a: jnp.pad(a, (HEADS, 0)).reshape(2 * HEADS, 1)
    alog_row, dtb_row = pad_lanes(a_log[0]), pad_lanes(dt_bias[0])
    alog_col, dtb_col = pad_rows(a_log[0]), pad_rows(dt_bias[0])
    gmix, gff, gple, gfin = row2(g_mix[0]), row2(g_ff[0]), row2(g_ple[0]), row2(g_final)
    lng, lnb, gnorm = row2(sgu_ln_g[0]), row2(sgu_ln_b[0]), row2(gdn_norm[0])
    wout, wup, wdown = w_out[0].astype(BF16), w_up[0].astype(BF16), w_down[0].astype(BF16)
    wple, wgate = w_ple[0].astype(BF16), w_ple_gate[0].astype(BF16)
    tail_w = (wout, gff, wup, wdown, gple, wple, wgate, gfin)

    qs, k, v, zs, sg, col, row, conv_p = _proj_call(
        x_prompt, gmix, w_all, wabt, w_conv[0], alog_row, dtb_row, alog_col, dtb_col,
        lng, lnb, w_s[0], b_s[0].T)
    og, gdn_p = _gdn_call(qs, k, v, col, row, zs, gnorm)
    y_prompt = _tail_call(x_prompt.reshape(b * l, d), og.reshape(b * l, D_GDN), sg.reshape(b * l, D_SGU),
                          p_prompt[0].reshape(b * l, D_PLE), *tail_w, rows=TAIL_ROWS).reshape(b, l, d)

    nconv, vg_s, sg_s, zs_s, v_s, col_s, qt, kt = _sample_proj_call(
        x_sample.reshape(n, d), jnp.transpose(state_conv[0], (1, 0, 2)), gmix, w_all,
        w_conv[0], alog_row, dtb_row, lng, lnb, w_s[0], b_s[0])
    og_s, gdn_s = _sample_gdn_call(state_gdn[0], qt, kt, v_s, col_s, zs_s, gnorm)
    y_sample = _tail_call(x_sample.reshape(n, d), og_s, sg_s, p_sample[0].reshape(n, D_PLE), *tail_w,
                          rows=n).reshape(n, 1, d)

    return (y_prompt, y_sample,
            conv_p[None], gdn_p[None],
            jnp.transpose(nconv, (1, 0, 2))[None], gdn_s[None],
            vg_s.reshape(n, 1, D_SGU)[None])
```

```python
import functools

import jax
import jax.numpy as jnp
from jax import lax
from jax.experimental import pallas as pl
from jax.experimental.pallas import tpu as pltpu

F32 = jnp.float32
BF16 = jnp.bfloat16

D_MODEL = 1024
D_GDN = 512
D_SGU = 512
HEADS = 4
DH = 128
D_QKV = 3 * D_GDN
D_FF = 4 * D_MODEL
D_PLE = 256
CONV_W = 4
GDN_CHUNK = 64
SGU_CHUNK = 128
EPS = 1e-6
LANES = 128
SUBLANES = 8
VMEM_LIMIT = 56 * 1024 * 1024

COL_Q, COL_K, COL_V, COL_Z, COL_U, COL_VV, COL_AB = 0, 512, 1024, 1536, 2048, 2560, 3072
W_COLS = COL_AB + LANES

PROJ_ROWS = 512
CONV_BLOCK = 8
GDN_ROWS = 1024
GDN_GROUP = 4
TAIL_ROWS = 512
SAMPLE_BLOCK = 8
SAMPLE_GROUP = 2
PREP_STEPS = 8


def _dot(a, b):
    return jnp.dot(a, b, preferred_element_type=F32)


def _dot_nt(a, b):
    return lax.dot_general(a, b, (((1,), (1,)), ((), ())), preferred_element_type=F32)


def _dot_tn(a, b):
    return lax.dot_general(a, b, (((0,), (0,)), ((), ())), preferred_element_type=F32)


def _split3(x):
    hi = x.astype(BF16)
    r1 = x - hi.astype(F32)
    mid = r1.astype(BF16)
    lo = (r1 - mid.astype(F32)).astype(BF16)
    return hi, mid, lo


def _rms(x, gain):
    return x * lax.rsqrt(jnp.mean(x * x, axis=-1, keepdims=True) + EPS) * gain


def _sigmoid(x):
    return 1.0 / (1.0 + jnp.exp(-x))


def _silu(x):
    hx = 0.5 * x
    return hx + hx * jnp.tanh(hx)


def _softplus(x):
    return jnp.maximum(x, 0.0) + jnp.log(1.0 + jnp.exp(-jnp.abs(x)))


def _gelu_tanh(x):
    c = 0.7978845608028654
    hx = 0.5 * x
    return hx + hx * jnp.tanh(x * (c + (c * 0.044715) * (x * x)))


def _l2n(x):
    return x * lax.rsqrt(jnp.sum(x * x, axis=-1, keepdims=True) + 1e-6)


def _layer_norm(x, gain, bias):
    mu = jnp.mean(x, axis=-1, keepdims=True)
    xc = x - mu
    var = jnp.mean(xc * xc, axis=-1, keepdims=True)
    return xc * lax.rsqrt(var + EPS) * gain + bias


def _resident(shape):
    nd = len(shape)
    return pl.BlockSpec(shape, lambda *_: (0,) * nd, pipeline_mode=pl.Buffered(1))


def _prep_kernel(win_ref, wout_ref, wup_ref, wdown_ref, wple_ref, wgate_ref,
                 wall_o, wout_o, wup_o, wdown_o, wple_o, wgate_o):
    o1 = D_QKV + D_GDN
    o3 = o1 + 2 * HEADS
    wall_o[:, 0:o1] = win_ref[:, 0:o1].astype(BF16)
    wall_o[:, COL_U:COL_AB] = win_ref[:, o3:o3 + 2 * D_SGU].astype(BF16)
    lane = lax.broadcasted_iota(jnp.int32, (win_ref.shape[0], LANES), 1)
    wall_o[:, COL_AB:] = jnp.where(lane < 2 * HEADS, win_ref[:, o1:o1 + LANES], 0.0).astype(BF16)
    wout_o[...] = wout_ref[...].astype(BF16)
    wup_o[...] = wup_ref[...].astype(BF16)
    wdown_o[...] = wdown_ref[...].astype(BF16)
    wple_o[...] = wple_ref[...].astype(BF16)
    wgate_o[...] = wgate_ref[...].astype(BF16)


def _prep_call(w_in, w_out, w_up, w_down, w_ple, w_ple_gate):
    steps = PREP_STEPS
    ins = [w_in, w_out, w_up, w_down, w_ple, w_ple_gate]
    out_cols = [W_COLS] + [w.shape[2] for w in ins[1:]]
    block = lambda w, cols: pl.BlockSpec((None, w.shape[1] // steps, cols), lambda i: (0, i, 0))
    out_block = lambda w, cols: pl.BlockSpec((w.shape[1] // steps, cols), lambda i: (i, 0))
    return pl.pallas_call(
        _prep_kernel,
        grid=(steps,),
        in_specs=[block(w, w.shape[2]) for w in ins],
        out_specs=[out_block(w, c) for w, c in zip(ins, out_cols)],
        out_shape=[jax.ShapeDtypeStruct((w.shape[1], c), BF16) for w, c in zip(ins, out_cols)],
        compiler_params=pltpu.CompilerParams(
            dimension_semantics=("arbitrary",), vmem_limit_bytes=VMEM_LIMIT),
        name="prep",
    )(*ins)


def _proj_kernel(x_ref, gmix_ref, w_ref, wabt_ref, wconv_ref,
                 alog_row_ref, dtb_row_ref, alog_col_ref, dtb_col_ref, lng_ref, lnb_ref,
                 ws_ref, bst_ref,
                 qs_ref, k_ref, v_ref, zs_ref, sg_ref, col_ref, row_ref, cstate_ref,
                 cbuf):
    rows = x_ref.shape[0]
    first_tile = pl.program_id(1) == 0
    h = _rms(x_ref[...], gmix_ref[...]).astype(BF16)

    def project(lo, width=D_GDN):
        return _dot(h, w_ref[:, lo:lo + width])

    def conv_silu(pre, lo):
        cols = slice(lo, lo + D_GDN)
        groups = rows // SUBLANES
        tap = lambda j: wconv_ref[j:j + 1, cols].reshape(1, 1, D_GDN)
        shape3 = (1, SUBLANES, D_GDN)
        last_p = jnp.where(first_tile, 0.0, cbuf[0:SUBLANES, cols]).reshape(shape3)
        last_b = jnp.where(first_tile, 0.0, cbuf[SUBLANES:, cols]).reshape(shape3)
        cstate_ref[:, cols] = pre[rows - (CONV_W - 1):rows, :]
        cur = pre.reshape(groups, SUBLANES, D_GDN)
        sub = lax.broadcasted_iota(jnp.int32, (CONV_BLOCK, SUBLANES, D_GDN), 1)

        def delay(a, a_last, s):
            rolled = pltpu.roll(a, s, axis=1)
            rolled_prev = jnp.concatenate([pltpu.roll(a_last, s, axis=1), rolled[:-1]], axis=0)
            return jnp.where(sub >= s, rolled, rolled_prev)

        assert CONV_W == 4
        out = []
        for g0 in range(0, groups, CONV_BLOCK):
            p = cur[g0:g0 + CONV_BLOCK]
            dp = delay(p, last_p, 1)
            b = p * tap(1) + dp * tap(0)
            y = p * tap(3) + dp * tap(2) + delay(b, last_b, 2)
            last_p, last_b = p[CONV_BLOCK - 1:], b[CONV_BLOCK - 1:]
            out.append(_silu(y))
        cbuf[0:SUBLANES, cols] = last_p.reshape(SUBLANES, D_GDN)
        cbuf[SUBLANES:, cols] = last_b.reshape(SUBLANES, D_GDN)
        return jnp.concatenate(out, axis=0).reshape(rows, D_GDN)

    pre_q = project(COL_Q)
    pre_k = project(COL_K)
    y = conv_silu(pre_q, COL_Q)
    for hh in range(HEADS):
        lo = hh * DH
        qs_ref[:, lo:lo + DH] = (_l2n(y[:, lo:lo + DH]) * (DH ** -0.5)).astype(BF16)
    pre_v = project(COL_V)
    y = conv_silu(pre_k, COL_K)
    for hh in range(HEADS):
        lo = hh * DH
        k_ref[:, lo:lo + DH] = _l2n(y[:, lo:lo + DH]).astype(BF16)
    pre_z = project(COL_Z)
    v_ref[...] = conv_silu(pre_v, COL_V).astype(BF16)
    pre_u = project(COL_U)
    zs_ref[...] = _silu(pre_z).astype(BF16)
    pre_vv = project(COL_VV)
    u_act = _gelu_tanh(pre_u)

    ab = project(COL_AB, LANES)
    lane = lax.broadcasted_iota(jnp.int32, (GDN_CHUNK, LANES), 1)
    g_col = -jnp.exp(alog_row_ref[...]) * _softplus(ab + dtb_row_ref[...])
    abt = _dot_nt(wabt_ref[...], h)[0:2 * HEADS, :]
    sub = lax.broadcasted_iota(jnp.int32, (2 * HEADS, GDN_CHUNK), 0)
    g_row = -jnp.exp(alog_col_ref[...]) * _softplus(abt + dtb_col_ref[...])
    ri = lax.broadcasted_iota(jnp.int32, (GDN_CHUNK, GDN_CHUNK), 0)
    ci = lax.broadcasted_iota(jnp.int32, (GDN_CHUNK, GDN_CHUNK), 1)
    tri_l = (ri >= ci).astype(BF16)
    tri_u = (ri <= ci).astype(BF16)
    beta_col = _sigmoid(ab)
    beta_row = _sigmoid(abt)
    for c in range(rows // GDN_CHUNK):
        r0 = c * GDN_CHUNK
        p0, p1, p2 = _split3(g_col[r0:r0 + GDN_CHUNK, :])
        gc = _dot(tri_l, p0) + _dot(tri_l, p1) + _dot(tri_l, p2)
        col_ref[r0:r0 + GDN_CHUNK, :] = jnp.where(lane < HEADS, beta_col[r0:r0 + GDN_CHUNK, :], gc)
        p0, p1, p2 = _split3(g_row[:, r0:r0 + GDN_CHUNK])
        gr = _dot(p0, tri_u) + _dot(p1, tri_u) + _dot(p2, tri_u)
        row_ref[c] = jnp.where(sub < HEADS, beta_row[:, r0:r0 + GDN_CHUNK], gr)

    vg = _layer_norm(_gelu_tanh(pre_vv), lng_ref[...], lnb_ref[...]).astype(BF16)
    ti = lax.broadcasted_iota(jnp.int32, (SGU_CHUNK, SGU_CHUNK), 0)
    si = lax.broadcasted_iota(jnp.int32, (SGU_CHUNK, SGU_CHUNK), 1)
    for hh in range(HEADS):
        lo = hh * DH
        w_tril = jnp.where(ti >= si, ws_ref[hh], 0.0).astype(BF16)
        bias = bst_ref[:, hh:hh + 1]
        for c in range(rows // SGU_CHUNK):
            r0 = c * SGU_CHUNK
            mix = _dot(w_tril, vg[r0:r0 + SGU_CHUNK, lo:lo + DH]) + bias
            sg_ref[r0:r0 + SGU_CHUNK, lo:lo + DH] = (u_act[r0:r0 + SGU_CHUNK, lo:lo + DH] * mix).astype(BF16)


def _proj_call(x, gmix, w_all, wabt, wconv, alog_row, dtb_row, alog_col, dtb_col, lng, lnb, ws, bst):
    b, l, _ = x.shape
    rows = PROJ_ROWS
    nt = l // rows
    nchunk = rows // GDN_CHUNK
    tile = lambda w: pl.BlockSpec((None, rows, w), lambda i, j: (i, j, 0))
    ins = [x, gmix, w_all, wabt, wconv, alog_row, dtb_row, alog_col, dtb_col, lng, lnb, ws, bst]
    in_specs = [tile(D_MODEL)] + [_resident(a.shape) for a in ins[1:]]
    out_shape = [jax.ShapeDtypeStruct((b, l, D_GDN), BF16)] * 5 + [
        jax.ShapeDtypeStruct((b, l, LANES), F32),
        jax.ShapeDtypeStruct((b, l // GDN_CHUNK, 2 * HEADS, GDN_CHUNK), F32),
        jax.ShapeDtypeStruct((b, CONV_W - 1, D_QKV), F32),
    ]
    out_specs = [tile(D_GDN)] * 5 + [
        tile(LANES),
        pl.BlockSpec((None, nchunk, 2 * HEADS, GDN_CHUNK), lambda i, j: (i, j, 0, 0)),
        pl.BlockSpec((None, CONV_W - 1, D_QKV), lambda i, j: (i, 0, 0)),
    ]
    return pl.pallas_call(
        _proj_kernel,
        grid=(b, nt),
        in_specs=in_specs,
        out_specs=out_specs,
        out_shape=out_shape,
        scratch_shapes=[pltpu.VMEM((2 * SUBLANES, D_QKV), F32)],
        compiler_params=pltpu.CompilerParams(
            dimension_semantics=("arbitrary", "arbitrary"), vmem_limit_bytes=VMEM_LIMIT),
        name="proj",
    )(*ins)


def _neumann_many(a_list):
    c = a_list[0].shape[0]
    p = [-a for a in a_list]
    pb = [x.astype(BF16) for x in p]
    q = [_dot(x, x) for x in pb]
    yield
    power = 2
    while 2 * power < c:
        qb = [x.astype(BF16) for x in q]
        r = [_dot(jnp.concatenate([pi.astype(BF16), qi], axis=0), qi) for pi, qi in zip(p, qb)]
        yield
        p = [pi + qi + ri[:c] for pi, qi, ri in zip(p, q, r)]
        q = [ri[c:] for ri in r]
        power *= 2
    tail = [_dot(pi.astype(BF16), qi.astype(BF16)) for pi, qi in zip(p, q)]
    yield
    return [pi + qi + ti for pi, qi, ti in zip(p, q, tail)]


def _interleave(steps, fillers):
    fillers = list(fillers)
    count = 0
    while True:
        try:
            next(steps)
        except StopIteration as stop:
            for f in fillers:
                f()
            return stop.value
        count += 1
        if fillers and count % 2 == 0:
            fillers.pop(0)()


def _gdn_kernel(qs_ref, k_ref, v_ref, col_ref, row_ref, zs_ref, gnorm_ref,
                og_ref, state_ref, s_scr):
    rows = qs_ref.shape[0]
    c_ = GDN_CHUNK
    t = pl.program_id(1)

    @pl.when(t == 0)
    def _():
        s_scr[...] = jnp.zeros(s_scr.shape, F32)

    ri = lax.broadcasted_iota(jnp.int32, (c_, c_), 0)
    ci = lax.broadcasted_iota(jnp.int32, (c_, c_), 1)
    incl = ri >= ci
    strict = ri > ci
    gnorm = gnorm_ref[...]

    def prepare(chunks):
        pairs = [(c, hh) for c in chunks for hh in range(HEADS)]
        cols = {c: col_ref[c * c_:(c + 1) * c_, :] for c in chunks}
        rws = {c: row_ref[c] for c in chunks}
        beta = [cols[c][:, hh:hh + 1] for c, hh in pairs]
        gc = [cols[c][:, HEADS + hh:HEADS + hh + 1] for c, hh in pairs]
        gr = [rws[c][HEADS + hh:HEADS + hh + 1, :] for c, hh in pairs]
        glast = [cols[c][c_ - 1:c_, HEADS + hh:HEADS + hh + 1] for c, hh in pairs]
        kh = [k_ref[c * c_:(c + 1) * c_, hh * DH:(hh + 1) * DH] for c, hh in pairs]
        qh = [qs_ref[c * c_:(c + 1) * c_, hh * DH:(hh + 1) * DH] for c, hh in pairs]
        kf = [x.astype(F32) for x in kh]
        kb = [x * b for x, b in zip(kf, beta)]
        m1 = [_dot_nt(jnp.concatenate([x.astype(BF16), q], axis=0), k) for x, q, k in zip(kb, qh, kh)]
        yield
        decay = [jnp.exp(jnp.where(incl, a - b, -jnp.inf)) for a, b in zip(gc, gr)]
        attn = [(m[c_:] * d).astype(BF16) for m, d in zip(m1, decay)]
        p = yield from _neumann_many([jnp.where(strict, m[:c_] * d, 0.0) for m, d in zip(m1, decay)])
        eg = [jnp.exp(x) for x in gc]
        rhs = [jnp.concatenate([x * e, v_ref[c * c_:(c + 1) * c_, hh * DH:(hh + 1) * DH].astype(F32) * b], axis=1)
               for x, e, b, (c, hh) in zip(kb, eg, beta, pairs)]
        wu = [(x + _dot(pi.astype(BF16), x.astype(BF16))).astype(BF16) for x, pi in zip(rhs, p)]
        yield
        ke_t = [jnp.transpose((x * jnp.exp(gl - g)).astype(BF16)) for x, gl, g in zip(kf, glast, gc)]
        r1 = [_dot(jnp.concatenate([kt, a], axis=0), x) for kt, a, x in zip(ke_t, attn, wu)]
        yield
        staged = {}
        for i, pair in enumerate(pairs):
            q_eff = qh[i].astype(F32) * eg[i] - r1[i][DH:, :DH]
            mq = jnp.concatenate([(-r1[i][:DH, :DH]).astype(BF16), q_eff.astype(BF16)], axis=0)
            staged[pair] = (mq, r1[i][:DH, DH:], r1[i][DH:, DH:], jnp.exp(glast[i]))
        return staged

    state = [s_scr[hh] for hh in range(HEADS)]

    def advance(c, staged):
        r0 = c * c_
        for hh in range(HEADS):
            lo = hh * DH
            mq, n_mat, o_zero, e_last = staged[c, hh]
            r = _dot(mq, state[hh].astype(BF16))
            state[hh] = state[hh] * e_last + r[:DH] + n_mat
            og = _rms(r[DH:] + o_zero, gnorm) * zs_ref[r0:r0 + c_, lo:lo + DH].astype(F32)
            og_ref[r0:r0 + c_, lo:lo + DH] = og.astype(BF16)

    nchunk = rows // c_
    fillers = []
    for g0 in range(0, nchunk, GDN_GROUP):
        group = list(range(g0, g0 + GDN_GROUP))
        staged = _interleave(prepare(group), fillers)
        fillers = [functools.partial(advance, c, staged) for c in group]
    for f in fillers:
        f()
    for hh in range(HEADS):
        s_scr[hh] = state[hh]

    @pl.when(t == pl.num_programs(1) - 1)
    def _():
        state_ref[...] = s_scr[...]


def _gdn_call(qs, k, v, col, row, zs, gnorm):
    b, l, _ = qs.shape
    rows = GDN_ROWS
    nt = l // rows
    nchunk = rows // GDN_CHUNK
    tile = lambda w: pl.BlockSpec((None, rows, w), lambda i, j: (i, j, 0))
    return pl.pallas_call(
        _gdn_kernel,
        grid=(b, nt),
        in_specs=[tile(D_GDN)] * 3 + [
            tile(LANES),
            pl.BlockSpec((None, nchunk, 2 * HEADS, GDN_CHUNK), lambda i, j: (i, j, 0, 0)),
            tile(D_GDN),
            _resident(gnorm.shape),
        ],
        out_specs=[tile(D_GDN), pl.BlockSpec((None, HEADS, DH, DH), lambda i, j: (i, 0, 0, 0))],
        out_shape=[jax.ShapeDtypeStruct((b, l, D_GDN), BF16),
                   jax.ShapeDtypeStruct((b, HEADS, DH, DH), F32)],
        scratch_shapes=[pltpu.VMEM((HEADS, DH, DH), F32)],
        compiler_params=pltpu.CompilerParams(
            dimension_semantics=("arbitrary", "arbitrary"), vmem_limit_bytes=VMEM_LIMIT),
        name="gdn",
    )(qs, k, v, col, row, zs, gnorm)


def _tail_kernel(x_ref, og_ref, sg_ref, p_ref, wout_ref, gff_ref, wup_ref, wdown_ref,
                 gple_ref, wple_ref, wgate_ref, gfin_ref, y_ref):
    x1 = x_ref[...] + _dot(og_ref[...], wout_ref[0:D_GDN, :]) + _dot(sg_ref[...], wout_ref[D_GDN:, :])
    h2 = _rms(x1, gff_ref[...]).astype(BF16)
    y_ref[...] = x1
    ff_block = D_FF // 4
    for c in range(D_FF // ff_block):
        lo = c * ff_block
        hid = jnp.maximum(_dot(h2, wup_ref[:, lo:lo + ff_block]), 0.0)
        y_ref[...] += _dot((hid * hid).astype(BF16), wdown_ref[lo:lo + ff_block, :])
    acc = y_ref[...]
    h3 = _rms(acc, gple_ref[...]).astype(BF16)
    gate = _sigmoid(_dot(h3, wgate_ref[...]))
    x3 = acc + _dot(p_ref[...].astype(BF16), wple_ref[...]) * gate
    y_ref[...] = _rms(x3, gfin_ref[...])


def _tail_call(x, og, sg, p, wout, gff, wup, wdown, gple, wple, wgate, gfin, rows):
    n = x.shape[0]
    tile = lambda w: pl.BlockSpec((rows, w), lambda i: (i, 0))
    weights = [wout, gff, wup, wdown, gple, wple, wgate, gfin]
    return pl.pallas_call(
        _tail_kernel,
        grid=(n // rows,),
        in_specs=[tile(D_MODEL), tile(D_GDN), tile(D_SGU), tile(D_PLE)] + [_resident(w.shape) for w in weights],
        out_specs=tile(D_MODEL),
        out_shape=jax.ShapeDtypeStruct((n, D_MODEL), F32),
        compiler_params=pltpu.CompilerParams(
            dimension_semantics=("arbitrary",), vmem_limit_bytes=VMEM_LIMIT),
        name="tail",
    )(x, og, sg, p, *weights)


def _sample_proj_kernel(x_ref, hist_ref, gmix_ref, w_ref, wconv_ref,
                        alog_row_ref, dtb_row_ref, lng_ref, lnb_ref, ws_ref, bs_ref,
                        nconv_ref, vg_ref, sg_ref, zs_ref, v_ref, col_ref, qt_ref, kt_ref):
    n = x_ref.shape[0]
    h = _rms(x_ref[...], gmix_ref[...]).astype(BF16)
    pre = _dot(h, w_ref[:, COL_Q:COL_Q + D_QKV])
    y = pre * wconv_ref[CONV_W - 1:CONV_W, :]
    for j in range(CONV_W - 1):
        y = y + hist_ref[j] * wconv_ref[j:j + 1, :]
        if j > 0:
            nconv_ref[j - 1] = hist_ref[j]
    nconv_ref[CONV_W - 2] = pre
    y = _silu(y)
    v_ref[...] = y[:, 2 * D_GDN:]
    blocks = n // SAMPLE_BLOCK
    for hh in range(HEADS):
        lo = hh * DH
        qt = jnp.transpose(_l2n(y[:, lo:lo + DH]) * (DH ** -0.5))
        kt = jnp.transpose(_l2n(y[:, D_GDN + lo:D_GDN + lo + DH]))
        for i in range(blocks):
            qt_ref[i, hh] = qt[:, i * SAMPLE_BLOCK:(i + 1) * SAMPLE_BLOCK]
            kt_ref[i, hh] = kt[:, i * SAMPLE_BLOCK:(i + 1) * SAMPLE_BLOCK]
    zs_ref[...] = _silu(_dot(h, w_ref[:, COL_Z:COL_Z + D_GDN]))
    ab = _dot(h, w_ref[:, COL_AB:COL_AB + LANES])
    lane = lax.broadcasted_iota(jnp.int32, ab.shape, 1)
    g_col = -jnp.exp(alog_row_ref[...]) * _softplus(ab + dtb_row_ref[...])
    col_ref[...] = jnp.where(lane < HEADS, _sigmoid(ab), g_col)
    uv = _gelu_tanh(_dot(h, w_ref[:, COL_U:COL_U + 2 * D_SGU]))
    vg = _layer_norm(uv[:, D_SGU:], lng_ref[...], lnb_ref[...])
    vg_ref[...] = vg
    for hh in range(HEADS):
        lo = hh * DH
        mix = ws_ref[hh, 0:1, 0:1] * vg[:, lo:lo + DH] + bs_ref[hh:hh + 1, 0:1]
        sg_ref[:, lo:lo + DH] = (uv[:, lo:lo + DH] * mix).astype(BF16)


def _sample_proj_call(x, hist, gmix, w_all, wconv, alog_row, dtb_row, lng, lnb, ws, bs):
    n = x.shape[0]
    blocks = n // SAMPLE_BLOCK
    out_shape = [
        jax.ShapeDtypeStruct((CONV_W - 1, n, D_QKV), F32),
        jax.ShapeDtypeStruct((n, D_SGU), F32),
        jax.ShapeDtypeStruct((n, D_SGU), BF16),
        jax.ShapeDtypeStruct((n, D_GDN), F32),
        jax.ShapeDtypeStruct((n, D_GDN), F32),
        jax.ShapeDtypeStruct((n, LANES), F32),
        jax.ShapeDtypeStruct((blocks, HEADS, DH, SAMPLE_BLOCK), F32),
        jax.ShapeDtypeStruct((blocks, HEADS, DH, SAMPLE_BLOCK), F32),
    ]
    return pl.pallas_call(
        _sample_proj_kernel,
        out_shape=out_shape,
        compiler_params=pltpu.CompilerParams(vmem_limit_bytes=VMEM_LIMIT),
        name="sample_proj",
    )(x, hist, gmix, w_all, wconv, alog_row, dtb_row, lng, lnb, ws, bs)


def _sample_gdn_kernel(s_ref, qt_ref, kt_ref, v_ref, col_ref, zs_ref, gnorm_ref, og_ref, snew_ref):
    gnorm = gnorm_ref[...]
    for b0 in range(0, SAMPLE_BLOCK, SAMPLE_GROUP):
        pairs = [(bb, hh) for bb in range(b0, b0 + SAMPLE_GROUP) for hh in range(HEADS)]
        kcol = [kt_ref[hh, :, bb:bb + 1] for bb, hh in pairs]
        qcol = [qt_ref[hh, :, bb:bb + 1] for bb, hh in pairs]
        beta = [col_ref[bb:bb + 1, hh:hh + 1] for bb, hh in pairs]
        eg = [jnp.exp(col_ref[bb:bb + 1, HEADS + hh:HEADS + hh + 1]) for bb, hh in pairs]
        sd = [s_ref[bb, hh] * e for (bb, hh), e in zip(pairs, eg)]
        sk = [jnp.sum(s * kc, axis=0, keepdims=True) for s, kc in zip(sd, kcol)]
        v_new = [b * (v_ref[bb:bb + 1, hh * DH:(hh + 1) * DH] - x) for (bb, hh), b, x in zip(pairs, beta, sk)]
        s1 = [s + kc * vn for s, kc, vn in zip(sd, kcol, v_new)]
        for (bb, hh), s in zip(pairs, s1):
            snew_ref[bb, hh] = s
        o = [jnp.sum(s * qc, axis=0, keepdims=True) for s, qc in zip(s1, qcol)]
        for (bb, hh), x in zip(pairs, o):
            lo = hh * DH
            og = _rms(x, gnorm) * zs_ref[bb:bb + 1, lo:lo + DH]
            og_ref[bb:bb + 1, lo:lo + DH] = og.astype(BF16)


def _sample_gdn_call(state, qt, kt, v, col, zs, gnorm):
    n = state.shape[0]
    blk = SAMPLE_BLOCK
    rowtile = lambda w: pl.BlockSpec((blk, w), lambda i: (i, 0))
    return pl.pallas_call(
        _sample_gdn_kernel,
        grid=(n // blk,),
        in_specs=[
            pl.BlockSpec((blk, HEADS, DH, DH), lambda i: (i, 0, 0, 0)),
            pl.BlockSpec((None, HEADS, DH, blk), lambda i: (i, 0, 0, 0)),
            pl.BlockSpec((None, HEADS, DH, blk), lambda i: (i, 0, 0, 0)),
            rowtile(D_GDN), rowtile(LANES), rowtile(D_GDN),
            _resident(gnorm.shape),
        ],
        out_specs=[rowtile(D_GDN), pl.BlockSpec((blk, HEADS, DH, DH), lambda i: (i, 0, 0, 0))],
        out_shape=[jax.ShapeDtypeStruct((n, D_GDN), BF16),
                   jax.ShapeDtypeStruct((n, HEADS, DH, DH), F32)],
        compiler_params=pltpu.CompilerParams(
            dimension_semantics=("arbitrary",), vmem_limit_bytes=VMEM_LIMIT),
        name="sample_gdn",
    )(state, qt, kt, v, col, zs, gnorm)


def kernel(x_prompt, x_sample, state_conv, state_gdn, p_prompt, p_sample, g_mix, w_in, w_conv, a_log, dt_bias, gdn_norm, sgu_ln_g, sgu_ln_b, w_s, b_s, w_out, g_ff, w_up, w_down, g_ple, w_ple, w_ple_gate, g_final):
    depth = w_in.shape[0]
    assert depth == 1, "single-layer trunk only"
    b, l, d = x_prompt.shape
    n = x_sample.shape[0]
    assert x_sample.shape[1] == 1 and d == D_MODEL
    assert l % PROJ_ROWS == 0 and l % GDN_ROWS == 0 and (b * l) % TAIL_ROWS == 0 and n % SAMPLE_BLOCK == 0

    o1, o3 = D_QKV + D_GDN, D_QKV + D_GDN + 2 * HEADS
    w_all, wout, wup, wdown, wple, wgate = _prep_call(w_in, w_out, w_up, w_down, w_ple, w_ple_gate)
    wabt = jnp.pad(w_in[0][:, o1:o3].T, ((0, 2 * SUBLANES - 2 * HEADS), (0, 0))).astype(BF16)
    row2 = lambda a: a.reshape(1, -1)
    pad_lanes = lambda a: jnp.pad(a, (HEADS, LANES - 2 * HEADS)).reshape(1, LANES)
    pad_rows = lambda a: jnp.pad(a, (HEADS, 0)).reshape(2 * HEADS, 1)
    alog_row, dtb_row = pad_lanes(a_log[0]), pad_lanes(dt_bias[0])
    alog_col, dtb_col = pad_rows(a_log[0]), pad_rows(dt_bias[0])
    gmix, gff, gple, gfin = row2(g_mix[0]), row2(g_ff[0]), row2(g_ple[0]), row2(g_final)
    lng, lnb, gnorm = row2(sgu_ln_g[0]), row2(sgu_ln_b[0]), row2(gdn_norm[0])
    tail_w = (wout, gff, wup, wdown, gple, wple, wgate, gfin)

    qs, k, v, zs, sg, col, row, conv_p = _proj_call(
        x_prompt, gmix, w_all, wabt, w_conv[0], alog_row, dtb_row, alog_col, dtb_col,
        lng, lnb, w_s[0], b_s[0].T)
    og, gdn_p = _gdn_call(qs, k, v, col, row, zs, gnorm)
    y_prompt = _tail_call(x_prompt.reshape(b * l, d), og.reshape(b * l, D_GDN), sg.reshape(b * l, D_SGU),
                          p_prompt[0].reshape(b * l, D_PLE), *tail_w, rows=TAIL_ROWS).reshape(b, l, d)

    nconv, vg_s, sg_s, zs_s, v_s, col_s, qt, kt = _sample_proj_call(
        x_sample.reshape(n, d), jnp.transpose(state_conv[0], (1, 0, 2)), gmix, w_all,
        w_conv[0], alog_row, dtb_row, lng, lnb, w_s[0], b_s[0])
    og_s, gdn_s = _sample_gdn_call(state_gdn[0], qt, kt, v_s, col_s, zs_s, gnorm)
    y_sample = _tail_call(x_sample.reshape(n, d), og_s, sg_s, p_sample[0].reshape(n, D_PLE), *tail_w,
                          rows=n).reshape(n, 1, d)

    return (y_prompt, y_sample,
            conv_p[None], gdn_p[None],
            jnp.transpose(nconv, (1, 0, 2))[None], gdn_s[None],
            vg_s.reshape(n, 1, D_SGU)[None])
```

```python
import functools

import jax
import jax.numpy as jnp
from jax import lax
from jax.experimental import pallas as pl
from jax.experimental.pallas import tpu as pltpu

F32 = jnp.float32
BF16 = jnp.bfloat16

D_MODEL = 1024
D_GDN = 512
D_SGU = 512
HEADS = 4
DH = 128
D_QKV = 3 * D_GDN
D_FF = 4 * D_MODEL
D_PLE = 256
CONV_W = 4
GDN_CHUNK = 64
SGU_CHUNK = 128
EPS = 1e-6
LANES = 128
SUBLANES = 8
VMEM_LIMIT = 56 * 1024 * 1024

COL_Q, COL_K, COL_V, COL_Z, COL_U, COL_VV, COL_AB = 0, 512, 1024, 1536, 2048, 2560, 3072
W_COLS = COL_AB + LANES

PROJ_ROWS = 512
CONV_BLOCK = 8
GDN_ROWS = 1024
GDN_GROUP = 4
TAIL_ROWS = 512
SAMPLE_BLOCK = 8
SAMPLE_GROUP = 4
PREP_STEPS = 8


def _dot(a, b):
    return jnp.dot(a, b, preferred_element_type=F32)


def _dot_nt(a, b):
    return lax.dot_general(a, b, (((1,), (1,)), ((), ())), preferred_element_type=F32)


def _dot_tn(a, b):
    return lax.dot_general(a, b, (((0,), (0,)), ((), ())), preferred_element_type=F32)


def _split3(x):
    hi = x.astype(BF16)
    r1 = x - hi.astype(F32)
    mid = r1.astype(BF16)
    lo = (r1 - mid.astype(F32)).astype(BF16)
    return hi, mid, lo


def _rms(x, gain):
    return x * lax.rsqrt(jnp.mean(x * x, axis=-1, keepdims=True) + EPS) * gain


def _sigmoid(x):
    return 1.0 / (1.0 + jnp.exp(-x))


def _silu(x):
    hx = 0.5 * x
    return hx + hx * jnp.tanh(hx)


def _softplus(x):
    return jnp.maximum(x, 0.0) + jnp.log(1.0 + jnp.exp(-jnp.abs(x)))


def _gelu_tanh(x):
    c = 0.7978845608028654
    hx = 0.5 * x
    return hx + hx * jnp.tanh(x * (c + (c * 0.044715) * (x * x)))


def _l2n(x):
    return x * lax.rsqrt(jnp.sum(x * x, axis=-1, keepdims=True) + 1e-6)


def _layer_norm(x, gain, bias):
    mu = jnp.mean(x, axis=-1, keepdims=True)
    xc = x - mu
    var = jnp.mean(xc * xc, axis=-1, keepdims=True)
    return xc * lax.rsqrt(var + EPS) * gain + bias


def _resident(shape):
    nd = len(shape)
    return pl.BlockSpec(shape, lambda *_: (0,) * nd, pipeline_mode=pl.Buffered(1))


def _prep_kernel(win_ref, wout_ref, wup_ref, wdown_ref, wple_ref, wgate_ref,
                 wall_o, wout_o, wup_o, wdown_o, wple_o, wgate_o, wabt_o):
    o1 = D_QKV + D_GDN
    o3 = o1 + 2 * HEADS
    wall_o[:, 0:o1] = win_ref[:, 0:o1].astype(BF16)
    wall_o[:, COL_U:COL_AB] = win_ref[:, o3:o3 + 2 * D_SGU].astype(BF16)
    lane = lax.broadcasted_iota(jnp.int32, (win_ref.shape[0], LANES), 1)
    ab = jnp.where(lane < 2 * HEADS, win_ref[:, o1:o1 + LANES], 0.0)
    wall_o[:, COL_AB:] = ab.astype(BF16)
    wabt_o[...] = jnp.transpose(ab)[0:2 * SUBLANES, :].astype(BF16)
    wout_o[...] = wout_ref[...].astype(BF16)
    wup_o[...] = wup_ref[...].astype(BF16)
    wdown_o[...] = wdown_ref[...].astype(BF16)
    wple_o[...] = wple_ref[...].astype(BF16)
    wgate_o[...] = wgate_ref[...].astype(BF16)


def _prep_call(w_in, w_out, w_up, w_down, w_ple, w_ple_gate):
    steps = PREP_STEPS
    ins = [w_in, w_out, w_up, w_down, w_ple, w_ple_gate]
    out_cols = [W_COLS] + [w.shape[2] for w in ins[1:]]
    block = lambda w, cols: pl.BlockSpec((None, w.shape[1] // steps, cols), lambda i: (0, i, 0))
    out_block = lambda w, cols: pl.BlockSpec((w.shape[1] // steps, cols), lambda i: (i, 0))
    d = w_in.shape[1]
    assert d // steps == LANES
    return pl.pallas_call(
        _prep_kernel,
        grid=(steps,),
        in_specs=[block(w, w.shape[2]) for w in ins],
        out_specs=[out_block(w, c) for w, c in zip(ins, out_cols)]
        + [pl.BlockSpec((2 * SUBLANES, LANES), lambda i: (0, i))],
        out_shape=[jax.ShapeDtypeStruct((w.shape[1], c), BF16) for w, c in zip(ins, out_cols)]
        + [jax.ShapeDtypeStruct((2 * SUBLANES, d), BF16)],
        compiler_params=pltpu.CompilerParams(
            dimension_semantics=("arbitrary",), vmem_limit_bytes=VMEM_LIMIT),
        name="prep",
    )(*ins)


def _proj_kernel(x_ref, gmix_ref, w_ref, wabt_ref, wconv_ref,
                 alog_row_ref, dtb_row_ref, alog_col_ref, dtb_col_ref, lng_ref, lnb_ref,
                 ws_ref, bst_ref,
                 qs_ref, k_ref, v_ref, zs_ref, sg_ref, col_ref, row_ref, cstate_ref,
                 cbuf):
    rows = x_ref.shape[0]
    first_tile = pl.program_id(1) == 0
    h = _rms(x_ref[...], gmix_ref[...]).astype(BF16)

    def project(lo, width=D_GDN):
        return _dot(h, w_ref[:, lo:lo + width])

    def conv_silu(pre, lo):
        cols = slice(lo, lo + D_GDN)
        groups = rows // SUBLANES
        tap = lambda j: wconv_ref[j:j + 1, cols].reshape(1, 1, D_GDN)
        shape3 = (1, SUBLANES, D_GDN)
        last_p = jnp.where(first_tile, 0.0, cbuf[0:SUBLANES, cols]).reshape(shape3)
        last_b = jnp.where(first_tile, 0.0, cbuf[SUBLANES:, cols]).reshape(shape3)
        cstate_ref[:, cols] = pre[rows - (CONV_W - 1):rows, :]
        cur = pre.reshape(groups, SUBLANES, D_GDN)
        sub = lax.broadcasted_iota(jnp.int32, (CONV_BLOCK, SUBLANES, D_GDN), 1)

        def delay(a, a_last, s):
            rolled = pltpu.roll(a, s, axis=1)
            rolled_prev = jnp.concatenate([pltpu.roll(a_last, s, axis=1), rolled[:-1]], axis=0)
            return jnp.where(sub >= s, rolled, rolled_prev)

        assert CONV_W == 4
        out = []
        for g0 in range(0, groups, CONV_BLOCK):
            p = cur[g0:g0 + CONV_BLOCK]
            dp = delay(p, last_p, 1)
            b = p * tap(1) + dp * tap(0)
            y = p * tap(3) + dp * tap(2) + delay(b, last_b, 2)
            last_p, last_b = p[CONV_BLOCK - 1:], b[CONV_BLOCK - 1:]
            out.append(_silu(y))
        cbuf[0:SUBLANES, cols] = last_p.reshape(SUBLANES, D_GDN)
        cbuf[SUBLANES:, cols] = last_b.reshape(SUBLANES, D_GDN)
        return jnp.concatenate(out, axis=0).reshape(rows, D_GDN)

    pre_vv = project(COL_VV)
    pre_u = project(COL_U)
    vg = _layer_norm(_gelu_tanh(pre_vv), lng_ref[...], lnb_ref[...]).astype(BF16)
    ab = project(COL_AB, LANES)
    pre_q = project(COL_Q)
    u_act = _gelu_tanh(pre_u)
    ti = lax.broadcasted_iota(jnp.int32, (SGU_CHUNK, SGU_CHUNK), 0)
    si = lax.broadcasted_iota(jnp.int32, (SGU_CHUNK, SGU_CHUNK), 1)
    for hh in range(HEADS):
        lo = hh * DH
        w_tril = jnp.where(ti >= si, ws_ref[hh], 0.0).astype(BF16)
        bias = bst_ref[:, hh:hh + 1]
        for c in range(rows // SGU_CHUNK):
            r0 = c * SGU_CHUNK
            mix = _dot(w_tril, vg[r0:r0 + SGU_CHUNK, lo:lo + DH]) + bias
            sg_ref[r0:r0 + SGU_CHUNK, lo:lo + DH] = (u_act[r0:r0 + SGU_CHUNK, lo:lo + DH] * mix).astype(BF16)
    pre_k = project(COL_K)

    lane = lax.broadcasted_iota(jnp.int32, (GDN_CHUNK, LANES), 1)
    g_col = -jnp.exp(alog_row_ref[...]) * _softplus(ab + dtb_row_ref[...])
    abt = _dot_nt(wabt_ref[...], h)[0:2 * HEADS, :]
    sub = lax.broadcasted_iota(jnp.int32, (2 * HEADS, GDN_CHUNK), 0)
    g_row = -jnp.exp(alog_col_ref[...]) * _softplus(abt + dtb_col_ref[...])
    ri = lax.broadcasted_iota(jnp.int32, (GDN_CHUNK, GDN_CHUNK), 0)
    ci = lax.broadcasted_iota(jnp.int32, (GDN_CHUNK, GDN_CHUNK), 1)
    tri_l = (ri >= ci).astype(BF16)
    tri_u = (ri <= ci).astype(BF16)
    beta_col = _sigmoid(ab)
    beta_row = _sigmoid(abt)
    for c in range(rows // GDN_CHUNK):
        r0 = c * GDN_CHUNK
        p0, p1, p2 = _split3(g_col[r0:r0 + GDN_CHUNK, :])
        gc = _dot(tri_l, p0) + _dot(tri_l, p1) + _dot(tri_l, p2)
        col_ref[r0:r0 + GDN_CHUNK, :] = jnp.where(lane < HEADS, beta_col[r0:r0 + GDN_CHUNK, :], gc)
        p0, p1, p2 = _split3(g_row[:, r0:r0 + GDN_CHUNK])
        gr = _dot(p0, tri_u) + _dot(p1, tri_u) + _dot(p2, tri_u)
        row_ref[c] = jnp.where(sub < HEADS, beta_row[:, r0:r0 + GDN_CHUNK], gr)

    y = conv_silu(pre_q, COL_Q)
    for hh in range(HEADS):
        lo = hh * DH
        qs_ref[:, lo:lo + DH] = (_l2n(y[:, lo:lo + DH]) * (DH ** -0.5)).astype(BF16)
    pre_v = project(COL_V)
    y = conv_silu(pre_k, COL_K)
    for hh in range(HEADS):
        lo = hh * DH
        k_ref[:, lo:lo + DH] = _l2n(y[:, lo:lo + DH]).astype(BF16)
    pre_z = project(COL_Z)
    v_ref[...] = conv_silu(pre_v, COL_V).astype(BF16)
    zs_ref[...] = _silu(pre_z).astype(BF16)


def _proj_call(x, gmix, w_all, wabt, wconv, alog_row, dtb_row, alog_col, dtb_col, lng, lnb, ws, bst):
    b, l, _ = x.shape
    rows = PROJ_ROWS
    nt = l // rows
    nchunk = rows // GDN_CHUNK
    tile = lambda w: pl.BlockSpec((None, rows, w), lambda i, j: (i, j, 0))
    ins = [x, gmix, w_all, wabt, wconv, alog_row, dtb_row, alog_col, dtb_col, lng, lnb, ws, bst]
    in_specs = [tile(D_MODEL)] + [_resident(a.shape) for a in ins[1:]]
    out_shape = [jax.ShapeDtypeStruct((b, l, D_GDN), BF16)] * 5 + [
        jax.ShapeDtypeStruct((b, l, LANES), F32),
        jax.ShapeDtypeStruct((b, l // GDN_CHUNK, 2 * HEADS, GDN_CHUNK), F32),
        jax.ShapeDtypeStruct((b, CONV_W - 1, D_QKV), F32),
    ]
    out_specs = [tile(D_GDN)] * 5 + [
        tile(LANES),
        pl.BlockSpec((None, nchunk, 2 * HEADS, GDN_CHUNK), lambda i, j: (i, j, 0, 0)),
        pl.BlockSpec((None, CONV_W - 1, D_QKV), lambda i, j: (i, 0, 0)),
    ]
    return pl.pallas_call(
        _proj_kernel,
        grid=(b, nt),
        in_specs=in_specs,
        out_specs=out_specs,
        out_shape=out_shape,
        scratch_shapes=[pltpu.VMEM((2 * SUBLANES, D_QKV), F32)],
        compiler_params=pltpu.CompilerParams(
            dimension_semantics=("arbitrary", "arbitrary"), vmem_limit_bytes=VMEM_LIMIT),
        name="proj",
    )(*ins)


def _neumann_many(a_list):
    c = a_list[0].shape[0]
    p = [-a for a in a_list]
    pb = [x.astype(BF16) for x in p]
    q = [_dot(x, x) for x in pb]
    yield
    power = 2
    while 2 * power < c:
        qb = [x.astype(BF16) for x in q]
        r = [_dot(jnp.concatenate([pi.astype(BF16), qi], axis=0), qi) for pi, qi in zip(p, qb)]
        yield
        p = [pi + qi + ri[:c] for pi, qi, ri in zip(p, q, r)]
        q = [ri[c:] for ri in r]
        power *= 2
    tail = [_dot(pi.astype(BF16), qi.astype(BF16)) for pi, qi in zip(p, q)]
    yield
    return [pi + qi + ti for pi, qi, ti in zip(p, q, tail)]


def _interleave(steps, fillers):
    fillers = list(fillers)
    count = 0
    while True:
        try:
            next(steps)
        except StopIteration as stop:
            for f in fillers:
                f()
            return stop.value
        count += 1
        if fillers and count % 2 == 0:
            fillers.pop(0)()


def _gdn_kernel(qs_ref, k_ref, v_ref, col_ref, row_ref, zs_ref, gnorm_ref,
                og_ref, state_ref, s_scr):
    rows = qs_ref.shape[0]
    c_ = GDN_CHUNK
    t = pl.program_id(1)

    @pl.when(t == 0)
    def _():
        s_scr[...] = jnp.zeros(s_scr.shape, F32)

    ri = lax.broadcasted_iota(jnp.int32, (c_, c_), 0)
    ci = lax.broadcasted_iota(jnp.int32, (c_, c_), 1)
    incl = ri >= ci
    strict = ri > ci
    gnorm = gnorm_ref[...]

    def prepare(chunks):
        pairs = [(c, hh) for c in chunks for hh in range(HEADS)]
        cols = {c: col_ref[c * c_:(c + 1) * c_, :] for c in chunks}
        rws = {c: row_ref[c] for c in chunks}
        beta = [cols[c][:, hh:hh + 1] for c, hh in pairs]
        gc = [cols[c][:, HEADS + hh:HEADS + hh + 1] for c, hh in pairs]
        gr = [rws[c][HEADS + hh:HEADS + hh + 1, :] for c, hh in pairs]
        glast = [cols[c][c_ - 1:c_, HEADS + hh:HEADS + hh + 1] for c, hh in pairs]
        kh = [k_ref[c * c_:(c + 1) * c_, hh * DH:(hh + 1) * DH] for c, hh in pairs]
        qh = [qs_ref[c * c_:(c + 1) * c_, hh * DH:(hh + 1) * DH] for c, hh in pairs]
        kf = [x.astype(F32) for x in kh]
        kb = [x * b for x, b in zip(kf, beta)]
        m1 = [_dot_nt(jnp.concatenate([x.astype(BF16), q], axis=0), k) for x, q, k in zip(kb, qh, kh)]
        yield
        decay = [jnp.exp(jnp.where(incl, a - b, -jnp.inf)) for a, b in zip(gc, gr)]
        attn = [(m[c_:] * d).astype(BF16) for m, d in zip(m1, decay)]
        p = yield from _neumann_many([jnp.where(strict, m[:c_] * d, 0.0) for m, d in zip(m1, decay)])
        eg = [jnp.exp(x) for x in gc]
        rhs = [jnp.concatenate([x * e, v_ref[c * c_:(c + 1) * c_, hh * DH:(hh + 1) * DH].astype(F32) * b], axis=1)
               for x, e, b, (c, hh) in zip(kb, eg, beta, pairs)]
        wu = [(x + _dot(pi.astype(BF16), x.astype(BF16))).astype(BF16) for x, pi in zip(rhs, p)]
        yield
        ke_t = [jnp.transpose((x * jnp.exp(gl - g)).astype(BF16)) for x, gl, g in zip(kf, glast, gc)]
        r1 = [_dot(jnp.concatenate([kt, a], axis=0), x) for kt, a, x in zip(ke_t, attn, wu)]
        yield
        staged = {}
        for i, pair in enumerate(pairs):
            q_eff = qh[i].astype(F32) * eg[i] - r1[i][DH:, :DH]
            mq = jnp.concatenate([(-r1[i][:DH, :DH]).astype(BF16), q_eff.astype(BF16)], axis=0)
            staged[pair] = (mq, r1[i][:DH, DH:], r1[i][DH:, DH:], jnp.exp(glast[i]))
        return staged

    state = [s_scr[hh] for hh in range(HEADS)]

    def advance(c, staged):
        r0 = c * c_
        for hh in range(HEADS):
            lo = hh * DH
            mq, n_mat, o_zero, e_last = staged[c, hh]
            r = _dot(mq, state[hh].astype(BF16))
            state[hh] = state[hh] * e_last + r[:DH] + n_mat
            og = _rms(r[DH:] + o_zero, gnorm) * zs_ref[r0:r0 + c_, lo:lo + DH].astype(F32)
            og_ref[r0:r0 + c_, lo:lo + DH] = og.astype(BF16)

    nchunk = rows // c_
    fillers = []
    for g0 in range(0, nchunk, GDN_GROUP):
        group = list(range(g0, g0 + GDN_GROUP))
        staged = _interleave(prepare(group), fillers)
        fillers = [functools.partial(advance, c, staged) for c in group]
    for f in fillers:
        f()
    for hh in range(HEADS):
        s_scr[hh] = state[hh]

    @pl.when(t == pl.num_programs(1) - 1)
    def _():
        state_ref[...] = s_scr[...]


def _gdn_call(qs, k, v, col, row, zs, gnorm):
    b, l, _ = qs.shape
    rows = GDN_ROWS
    nt = l // rows
    nchunk = rows // GDN_CHUNK
    tile = lambda w: pl.BlockSpec((None, rows, w), lambda i, j: (i, j, 0))
    return pl.pallas_call(
        _gdn_kernel,
        grid=(b, nt),
        in_specs=[tile(D_GDN)] * 3 + [
            tile(LANES),
            pl.BlockSpec((None, nchunk, 2 * HEADS, GDN_CHUNK), lambda i, j: (i, j, 0, 0)),
            tile(D_GDN),
            _resident(gnorm.shape),
        ],
        out_specs=[tile(D_GDN), pl.BlockSpec((None, HEADS, DH, DH), lambda i, j: (i, 0, 0, 0))],
        out_shape=[jax.ShapeDtypeStruct((b, l, D_GDN), BF16),
                   jax.ShapeDtypeStruct((b, HEADS, DH, DH), F32)],
        scratch_shapes=[pltpu.VMEM((HEADS, DH, DH), F32)],
        compiler_params=pltpu.CompilerParams(
            dimension_semantics=("arbitrary", "arbitrary"), vmem_limit_bytes=VMEM_LIMIT),
        name="gdn",
    )(qs, k, v, col, row, zs, gnorm)


def _tail_kernel(x_ref, og_ref, sg_ref, p_ref, wout_ref, gff_ref, wup_ref, wdown_ref,
                 gple_ref, wple_ref, wgate_ref, gfin_ref, y_ref):
    x1 = x_ref[...] + _dot(og_ref[...], wout_ref[0:D_GDN, :]) + _dot(sg_ref[...], wout_ref[D_GDN:, :])
    h2 = _rms(x1, gff_ref[...]).astype(BF16)
    y_ref[...] = x1
    ff_block = D_FF // 4
    for c in range(D_FF // ff_block):
        lo = c * ff_block
        hid = jnp.maximum(_dot(h2, wup_ref[:, lo:lo + ff_block]), 0.0)
        y_ref[...] += _dot((hid * hid).astype(BF16), wdown_ref[lo:lo + ff_block, :])
    acc = y_ref[...]
    h3 = _rms(acc, gple_ref[...]).astype(BF16)
    gate = _sigmoid(_dot(h3, wgate_ref[...]))
    x3 = acc + _dot(p_ref[...].astype(BF16), wple_ref[...]) * gate
    y_ref[...] = _rms(x3, gfin_ref[...])


def _tail_call(x, og, sg, p, wout, gff, wup, wdown, gple, wple, wgate, gfin, rows):
    n = x.shape[0]
    tile = lambda w: pl.BlockSpec((rows, w), lambda i: (i, 0))
    weights = [wout, gff, wup, wdown, gple, wple, wgate, gfin]
    return pl.pallas_call(
        _tail_kernel,
        grid=(n // rows,),
        in_specs=[tile(D_MODEL), tile(D_GDN), tile(D_SGU), tile(D_PLE)] + [_resident(w.shape) for w in weights],
        out_specs=tile(D_MODEL),
        out_shape=jax.ShapeDtypeStruct((n, D_MODEL), F32),
        compiler_params=pltpu.CompilerParams(
            dimension_semantics=("arbitrary",), vmem_limit_bytes=VMEM_LIMIT),
        name="tail",
    )(x, og, sg, p, *weights)


def _sample_proj_kernel(x_ref, hist_ref, gmix_ref, w_ref, wconv_ref,
                        alog_row_ref, dtb_row_ref, lng_ref, lnb_ref, ws_ref, bs_ref,
                        nconv_ref, vg_ref, sg_ref, zs_ref, v_ref, col_ref, qt_ref, kt_ref):
    n = x_ref.shape[0]
    h = _rms(x_ref[...], gmix_ref[...]).astype(BF16)
    pre = _dot(h, w_ref[:, COL_Q:COL_Q + D_QKV])
    y = pre * wconv_ref[CONV_W - 1:CONV_W, :]
    for j in range(CONV_W - 1):
        y = y + hist_ref[j] * wconv_ref[j:j + 1, :]
        if j > 0:
            nconv_ref[j - 1] = hist_ref[j]
    nconv_ref[CONV_W - 2] = pre
    y = _silu(y)
    v_ref[...] = y[:, 2 * D_GDN:]
    blocks = n // SAMPLE_BLOCK
    for hh in range(HEADS):
        lo = hh * DH
        qt = jnp.transpose(_l2n(y[:, lo:lo + DH]) * (DH ** -0.5))
        kt = jnp.transpose(_l2n(y[:, D_GDN + lo:D_GDN + lo + DH]))
        for i in range(blocks):
            qt_ref[i, hh] = qt[:, i * SAMPLE_BLOCK:(i + 1) * SAMPLE_BLOCK]
            kt_ref[i, hh] = kt[:, i * SAMPLE_BLOCK:(i + 1) * SAMPLE_BLOCK]
    zs_ref[...] = _silu(_dot(h, w_ref[:, COL_Z:COL_Z + D_GDN]))
    ab = _dot(h, w_ref[:, COL_AB:COL_AB + LANES])
    lane = lax.broadcasted_iota(jnp.int32, ab.shape, 1)
    g_col = -jnp.exp(alog_row_ref[...]) * _softplus(ab + dtb_row_ref[...])
    col_ref[...] = jnp.where(lane < HEADS, _sigmoid(ab), g_col)
    uv = _gelu_tanh(_dot(h, w_ref[:, COL_U:COL_U + 2 * D_SGU]))
    vg = _layer_norm(uv[:, D_SGU:], lng_ref[...], lnb_ref[...])
    vg_ref[...] = vg
    for hh in range(HEADS):
        lo = hh * DH
        mix = ws_ref[hh, 0:1, 0:1] * vg[:, lo:lo + DH] + bs_ref[hh:hh + 1, 0:1]
        sg_ref[:, lo:lo + DH] = (uv[:, lo:lo + DH] * mix).astype(BF16)


def _sample_proj_call(x, hist, gmix, w_all, wconv, alog_row, dtb_row, lng, lnb, ws, bs):
    n = x.shape[0]
    blocks = n // SAMPLE_BLOCK
    out_shape = [
        jax.ShapeDtypeStruct((CONV_W - 1, n, D_QKV), F32),
        jax.ShapeDtypeStruct((n, D_SGU), F32),
        jax.ShapeDtypeStruct((n, D_SGU), BF16),
        jax.ShapeDtypeStruct((n, D_GDN), F32),
        jax.ShapeDtypeStruct((n, D_GDN), F32),
        jax.ShapeDtypeStruct((n, LANES), F32),
        jax.ShapeDtypeStruct((blocks, HEADS, DH, SAMPLE_BLOCK), F32),
        jax.ShapeDtypeStruct((blocks, HEADS, DH, SAMPLE_BLOCK), F32),
    ]
    return pl.pallas_call(
        _sample_proj_kernel,
        out_shape=out_shape,
        compiler_params=pltpu.CompilerParams(vmem_limit_bytes=VMEM_LIMIT),
        name="sample_proj",
    )(x, hist, gmix, w_all, wconv, alog_row, dtb_row, lng, lnb, ws, bs)


def _sample_gdn_kernel(s_ref, qt_ref, kt_ref, v_ref, col_ref, zs_ref, gnorm_ref, og_ref, snew_ref):
    gnorm = gnorm_ref[...]
    for b0 in range(0, SAMPLE_BLOCK, SAMPLE_GROUP):
        pairs = [(bb, hh) for bb in range(b0, b0 + SAMPLE_GROUP) for hh in range(HEADS)]
        kcol = [kt_ref[hh, :, bb:bb + 1] for bb, hh in pairs]
        qcol = [qt_ref[hh, :, bb:bb + 1] for bb, hh in pairs]
        beta = [col_ref[bb:bb + 1, hh:hh + 1] for bb, hh in pairs]
        eg = [jnp.exp(col_ref[bb:bb + 1, HEADS + hh:HEADS + hh + 1]) for bb, hh in pairs]
        sd = [s_ref[bb, hh] * e for (bb, hh), e in zip(pairs, eg)]
        sk = [jnp.sum(s * kc, axis=0, keepdims=True) for s, kc in zip(sd, kcol)]
        v_new = [b * (v_ref[bb:bb + 1, hh * DH:(hh + 1) * DH] - x) for (bb, hh), b, x in zip(pairs, beta, sk)]
        s1 = [s + kc * vn for s, kc, vn in zip(sd, kcol, v_new)]
        for (bb, hh), s in zip(pairs, s1):
            snew_ref[bb, hh] = s
        o = [jnp.sum(s * qc, axis=0, keepdims=True) for s, qc in zip(s1, qcol)]
        for (bb, hh), x in zip(pairs, o):
            lo = hh * DH
            og = _rms(x, gnorm) * zs_ref[bb:bb + 1, lo:lo + DH]
            og_ref[bb:bb + 1, lo:lo + DH] = og.astype(BF16)


def _sample_gdn_call(state, qt, kt, v, col, zs, gnorm):
    n = state.shape[0]
    blk = SAMPLE_BLOCK
    rowtile = lambda w: pl.BlockSpec((blk, w), lambda i: (i, 0))
    return pl.pallas_call(
        _sample_gdn_kernel,
        grid=(n // blk,),
        in_specs=[
            pl.BlockSpec((blk, HEADS, DH, DH), lambda i: (i, 0, 0, 0)),
            pl.BlockSpec((None, HEADS, DH, blk), lambda i: (i, 0, 0, 0)),
            pl.BlockSpec((None, HEADS, DH, blk), lambda i: (i, 0, 0, 0)),
            rowtile(D_GDN), rowtile(LANES), rowtile(D_GDN),
            _resident(gnorm.shape),
        ],
        out_specs=[rowtile(D_GDN), pl.BlockSpec((blk, HEADS, DH, DH), lambda i: (i, 0, 0, 0))],
        out_shape=[jax.ShapeDtypeStruct((n, D_GDN), BF16),
                   jax.ShapeDtypeStruct((n, HEADS, DH, DH), F32)],
        compiler_params=pltpu.CompilerParams(
            dimension_semantics=("arbitrary",), vmem_limit_bytes=VMEM_LIMIT),
        name="sample_gdn",
    )(state, qt, kt, v, col, zs, gnorm)


def kernel(x_prompt, x_sample, state_conv, state_gdn, p_prompt, p_sample, g_mix, w_in, w_conv, a_log, dt_bias, gdn_norm, sgu_ln_g, sgu_ln_b, w_s, b_s, w_out, g_ff, w_up, w_down, g_ple, w_ple, w_ple_gate, g_final):
    depth = w_in.shape[0]
    assert depth == 1, "single-layer trunk only"
    b, l, d = x_prompt.shape
    n = x_sample.shape[0]
    assert x_sample.shape[1] == 1 and d == D_MODEL
    assert l % PROJ_ROWS == 0 and l % GDN_ROWS == 0 and (b * l) % TAIL_ROWS == 0 and n % SAMPLE_BLOCK == 0

    w_all, wout, wup, wdown, wple, wgate, wabt = _prep_call(w_in, w_out, w_up, w_down, w_ple, w_ple_gate)
    row2 = lambda a: a.reshape(1, -1)
    pad_lanes = lambda a: jnp.pad(a, (HEADS, LANES - 2 * HEADS)).reshape(1, LANES)
    pad_rows = lambda a: jnp.pad(a, (HEADS, 0)).reshape(2 * HEADS, 1)
    alog_row, dtb_row = pad_lanes(a_log[0]), pad_lanes(dt_bias[0])
    alog_col, dtb_col = pad_rows(a_log[0]), pad_rows(dt_bias[0])
    gmix, gff, gple, gfin = row2(g_mix[0]), row2(g_ff[0]), row2(g_ple[0]), row2(g_final)
    lng, lnb, gnorm = row2(sgu_ln_g[0]), row2(sgu_ln_b[0]), row2(gdn_norm[0])
    tail_w = (wout, gff, wup, wdown, gple, wple, wgate, gfin)

    qs, k, v, zs, sg, col, row, conv_p = _proj_call(
        x_prompt, gmix, w_all, wabt, w_conv[0], alog_row, dtb_row, alog_col, dtb_col,
        lng, lnb, w_s[0], b_s[0].T)
    og, gdn_p = _gdn_call(qs, k, v, col, row, zs, gnorm)
    y_prompt = _tail_call(x_prompt.reshape(b * l, d), og.reshape(b * l, D_GDN), sg.reshape(b * l, D_SGU),
                          p_prompt[0].reshape(b * l, D_PLE), *tail_w, rows=TAIL_ROWS).reshape(b, l, d)

    nconv, vg_s, sg_s, zs_s, v_s, col_s, qt, kt = _sample_proj_call(
        x_sample.reshape(n, d), jnp.transpose(state_conv[0], (1, 0, 2)), gmix, w_all,
        w_conv[0], alog_row, dtb_row, lng, lnb, w_s[0], b_s[0])
    og_s, gdn_s = _sample_gdn_call(state_gdn[0], qt, kt, v_s, col_s, zs_s, gnorm)
    y_sample = _tail_call(x_sample.reshape(n, d), og_s, sg_s, p_sample[0].reshape(n, D_PLE), *tail_w,
                          rows=n).reshape(n, 1, d)

    return (y_prompt, y_sample,
            conv_p[None], gdn_p[None],
            jnp.transpose(nconv, (1, 0, 2))[None], gdn_s[None],
            vg_s.reshape(n, 1, D_SGU)[None])
```

```python
import functools

import jax
import jax.numpy as jnp
from jax import lax
from jax.experimental import pallas as pl
from jax.experimental.pallas import tpu as pltpu

F32 = jnp.float32
BF16 = jnp.bfloat16

D_MODEL = 1024
D_GDN = 512
D_SGU = 512
HEADS = 4
DH = 128
D_QKV = 3 * D_GDN
D_FF = 4 * D_MODEL
D_PLE = 256
CONV_W = 4
GDN_CHUNK = 64
SGU_CHUNK = 128
EPS = 1e-6
LANES = 128
SUBLANES = 8
VMEM_LIMIT = 56 * 1024 * 1024

COL_Q, COL_K, COL_V, COL_Z, COL_U, COL_VV, COL_AB = 0, 512, 1024, 1536, 2048, 2560, 3072
W_COLS = COL_AB + LANES

PROJ_ROWS = 512
CONV_BLOCK = 8
GDN_ROWS = 1024
GDN_GROUP = 4
TAIL_ROWS = 512
SAMPLE_BLOCK = 8
SAMPLE_GROUP = 4
PREP_STEPS = 8


def _dot(a, b):
    return jnp.dot(a, b, preferred_element_type=F32)


def _dot_nt(a, b):
    return lax.dot_general(a, b, (((1,), (1,)), ((), ())), preferred_element_type=F32)


def _dot_tn(a, b):
    return lax.dot_general(a, b, (((0,), (0,)), ((), ())), preferred_element_type=F32)


def _split3(x):
    hi = x.astype(BF16)
    r1 = x - hi.astype(F32)
    mid = r1.astype(BF16)
    lo = (r1 - mid.astype(F32)).astype(BF16)
    return hi, mid, lo


def _rms(x, gain):
    return x * lax.rsqrt(jnp.mean(x * x, axis=-1, keepdims=True) + EPS) * gain


def _sigmoid(x):
    return 1.0 / (1.0 + jnp.exp(-x))


def _silu(x):
    hx = 0.5 * x
    return hx + hx * jnp.tanh(hx)


def _softplus(x):
    return jnp.maximum(x, 0.0) + jnp.log(1.0 + jnp.exp(-jnp.abs(x)))


def _gelu_tanh(x):
    c = 0.7978845608028654
    hx = 0.5 * x
    return hx + hx * jnp.tanh(x * (c + (c * 0.044715) * (x * x)))


def _l2n(x):
    return x * lax.rsqrt(jnp.sum(x * x, axis=-1, keepdims=True) + 1e-6)


def _layer_norm(x, gain, bias):
    mu = jnp.mean(x, axis=-1, keepdims=True)
    xc = x - mu
    var = jnp.mean(xc * xc, axis=-1, keepdims=True)
    return xc * lax.rsqrt(var + EPS) * gain + bias


def _resident(shape):
    nd = len(shape)
    return pl.BlockSpec(shape, lambda *_: (0,) * nd, pipeline_mode=pl.Buffered(1))


def _prep_kernel(win_ref, wout_ref, wup_ref, wdown_ref, wple_ref, wgate_ref,
                 wall_o, wout_o, wup_o, wdown_o, wple_o, wgate_o, wabt_o):
    o1 = D_QKV + D_GDN
    o3 = o1 + 2 * HEADS

    def put(dst, src):
        wall_o[:, dst:dst + LANES] = jnp.transpose(win_ref[src:src + LANES, :]).astype(BF16)

    for j in range(o1 // LANES):
        put(j * LANES, j * LANES)
    for j in range(2 * D_SGU // LANES):
        put(COL_U + j * LANES, o3 + j * LANES)
    ab = win_ref[o1:o3, :]
    ab16 = jnp.concatenate([ab, jnp.zeros((2 * SUBLANES - 2 * HEADS, LANES), F32)], axis=0)
    wabt_o[...] = ab16.astype(BF16)
    ab128 = jnp.concatenate([ab, jnp.zeros((LANES - 2 * HEADS, LANES), F32)], axis=0)
    wall_o[:, COL_AB:] = jnp.transpose(ab128).astype(BF16)
    wout_o[...] = wout_ref[...].astype(BF16)
    wup_o[...] = wup_ref[...].astype(BF16)
    wdown_o[...] = wdown_ref[...].astype(BF16)
    wple_o[...] = wple_ref[...].astype(BF16)
    wgate_o[...] = wgate_ref[...].astype(BF16)


def _prep_call(w_in, w_out, w_up, w_down, w_ple, w_ple_gate):
    steps = PREP_STEPS
    ins = [w_in, w_out, w_up, w_down, w_ple, w_ple_gate]
    out_cols = [W_COLS] + [w.shape[2] for w in ins[1:]]
    block = lambda w, cols: pl.BlockSpec((None, w.shape[1] // steps, cols), lambda i: (0, i, 0))
    out_block = lambda w, cols: pl.BlockSpec((w.shape[1] // steps, cols), lambda i: (i, 0))
    d, d_in = w_in.shape[1], w_in.shape[2]
    assert d // steps == LANES
    win_t = jnp.transpose(w_in[0])
    return pl.pallas_call(
        _prep_kernel,
        grid=(steps,),
        in_specs=[pl.BlockSpec((d_in, LANES), lambda i: (0, i))] + [block(w, w.shape[2]) for w in ins[1:]],
        out_specs=[out_block(w, c) for w, c in zip(ins, out_cols)]
        + [pl.BlockSpec((2 * SUBLANES, LANES), lambda i: (0, i))],
        out_shape=[jax.ShapeDtypeStruct((w.shape[1], c), BF16) for w, c in zip(ins, out_cols)]
        + [jax.ShapeDtypeStruct((2 * SUBLANES, d), BF16)],
        compiler_params=pltpu.CompilerParams(
            dimension_semantics=("arbitrary",), vmem_limit_bytes=VMEM_LIMIT),
        name="prep",
    )(win_t, *ins[1:])


def _proj_kernel(x_ref, gmix_ref, w_ref, wabt_ref, wconv_ref,
                 alog_row_ref, dtb_row_ref, alog_col_ref, dtb_col_ref, lng_ref, lnb_ref,
                 ws_ref, bst_ref,
                 qs_ref, k_ref, v_ref, zs_ref, sg_ref, col_ref, row_ref, cstate_ref,
                 cbuf):
    rows = x_ref.shape[0]
    first_tile = pl.program_id(1) == 0
    h = _rms(x_ref[...], gmix_ref[...]).astype(BF16)

    def project(lo, width=D_GDN):
        return _dot(h, w_ref[:, lo:lo + width])

    def conv_silu(pre, lo):
        cols = slice(lo, lo + D_GDN)
        groups = rows // SUBLANES
        tap = lambda j: wconv_ref[j:j + 1, cols].reshape(1, 1, D_GDN)
        shape3 = (1, SUBLANES, D_GDN)
        last_p = jnp.where(first_tile, 0.0, cbuf[0:SUBLANES, cols]).reshape(shape3)
        last_b = jnp.where(first_tile, 0.0, cbuf[SUBLANES:, cols]).reshape(shape3)
        cstate_ref[:, cols] = pre[rows - (CONV_W - 1):rows, :]
        cur = pre.reshape(groups, SUBLANES, D_GDN)
        sub = lax.broadcasted_iota(jnp.int32, (CONV_BLOCK, SUBLANES, D_GDN), 1)

        def delay(a, a_last, s):
            rolled = pltpu.roll(a, s, axis=1)
            rolled_prev = jnp.concatenate([pltpu.roll(a_last, s, axis=1), rolled[:-1]], axis=0)
            return jnp.where(sub >= s, rolled, rolled_prev)

        assert CONV_W == 4
        out = []
        for g0 in range(0, groups, CONV_BLOCK):
            p = cur[g0:g0 + CONV_BLOCK]
            dp = delay(p, last_p, 1)
            b = p * tap(1) + dp * tap(0)
            y = p * tap(3) + dp * tap(2) + delay(b, last_b, 2)
            last_p, last_b = p[CONV_BLOCK - 1:], b[CONV_BLOCK - 1:]
            out.append(_silu(y))
        cbuf[0:SUBLANES, cols] = last_p.reshape(SUBLANES, D_GDN)
        cbuf[SUBLANES:, cols] = last_b.reshape(SUBLANES, D_GDN)
        return jnp.concatenate(out, axis=0).reshape(rows, D_GDN)

    pre_vv = project(COL_VV)
    pre_u = project(COL_U)
    vg = _layer_norm(_gelu_tanh(pre_vv), lng_ref[...], lnb_ref[...]).astype(BF16)
    ab = project(COL_AB, LANES)
    pre_q = project(COL_Q)
    u_act = _gelu_tanh(pre_u)
    ti = lax.broadcasted_iota(jnp.int32, (SGU_CHUNK, SGU_CHUNK), 0)
    si = lax.broadcasted_iota(jnp.int32, (SGU_CHUNK, SGU_CHUNK), 1)
    for hh in range(HEADS):
        lo = hh * DH
        w_tril = jnp.where(ti >= si, ws_ref[hh], 0.0).astype(BF16)
        bias = bst_ref[:, hh:hh + 1]
        for c in range(rows // SGU_CHUNK):
            r0 = c * SGU_CHUNK
            mix = _dot(w_tril, vg[r0:r0 + SGU_CHUNK, lo:lo + DH]) + bias
            sg_ref[r0:r0 + SGU_CHUNK, lo:lo + DH] = (u_act[r0:r0 + SGU_CHUNK, lo:lo + DH] * mix).astype(BF16)
    pre_k = project(COL_K)

    lane = lax.broadcasted_iota(jnp.int32, (GDN_CHUNK, LANES), 1)
    g_col = -jnp.exp(alog_row_ref[...]) * _softplus(ab + dtb_row_ref[...])
    abt = _dot_nt(wabt_ref[...], h)[0:2 * HEADS, :]
    sub = lax.broadcasted_iota(jnp.int32, (2 * HEADS, GDN_CHUNK), 0)
    g_row = -jnp.exp(alog_col_ref[...]) * _softplus(abt + dtb_col_ref[...])
    ri = lax.broadcasted_iota(jnp.int32, (GDN_CHUNK, GDN_CHUNK), 0)
    ci = lax.broadcasted_iota(jnp.int32, (GDN_CHUNK, GDN_CHUNK), 1)
    tri_l = (ri >= ci).astype(BF16)
    tri_u = (ri <= ci).astype(BF16)
    beta_col = _sigmoid(ab)
    beta_row = _sigmoid(abt)
    for c in range(rows // GDN_CHUNK):
        r0 = c * GDN_CHUNK
        p0, p1, p2 = _split3(g_col[r0:r0 + GDN_CHUNK, :])
        gc = _dot(tri_l, p0) + _dot(tri_l, p1) + _dot(tri_l, p2)
        col_ref[r0:r0 + GDN_CHUNK, :] = jnp.where(lane < HEADS, beta_col[r0:r0 + GDN_CHUNK, :], gc)
        p0, p1, p2 = _split3(g_row[:, r0:r0 + GDN_CHUNK])
        gr = _dot(p0, tri_u) + _dot(p1, tri_u) + _dot(p2, tri_u)
        row_ref[c] = jnp.where(sub < HEADS, beta_row[:, r0:r0 + GDN_CHUNK], gr)

    y = conv_silu(pre_q, COL_Q)
    for hh in range(HEADS):
        lo = hh * DH
        qs_ref[:, lo:lo + DH] = (_l2n(y[:, lo:lo + DH]) * (DH ** -0.5)).astype(BF16)
    pre_v = project(COL_V)
    y = conv_silu(pre_k, COL_K)
    for hh in range(HEADS):
        lo = hh * DH
        k_ref[:, lo:lo + DH] = _l2n(y[:, lo:lo + DH]).astype(BF16)
    pre_z = project(COL_Z)
    v_ref[...] = conv_silu(pre_v, COL_V).astype(BF16)
    zs_ref[...] = _silu(pre_z).astype(BF16)


def _proj_call(x, gmix, w_all, wabt, wconv, alog_row, dtb_row, alog_col, dtb_col, lng, lnb, ws, bst):
    b, l, _ = x.shape
    rows = PROJ_ROWS
    nt = l // rows
    nchunk = rows // GDN_CHUNK
    tile = lambda w: pl.BlockSpec((None, rows, w), lambda i, j: (i, j, 0))
    ins = [x, gmix, w_all, wabt, wconv, alog_row, dtb_row, alog_col, dtb_col, lng, lnb, ws, bst]
    in_specs = [tile(D_MODEL)] + [_resident(a.shape) for a in ins[1:]]
    out_shape = [jax.ShapeDtypeStruct((b, l, D_GDN), BF16)] * 5 + [
        jax.ShapeDtypeStruct((b, l, LANES), F32),
        jax.ShapeDtypeStruct((b, l // GDN_CHUNK, 2 * HEADS, GDN_CHUNK), F32),
        jax.ShapeDtypeStruct((b, CONV_W - 1, D_QKV), F32),
    ]
    out_specs = [tile(D_GDN)] * 5 + [
        tile(LANES),
        pl.BlockSpec((None, nchunk, 2 * HEADS, GDN_CHUNK), lambda i, j: (i, j, 0, 0)),
        pl.BlockSpec((None, CONV_W - 1, D_QKV), lambda i, j: (i, 0, 0)),
    ]
    return pl.pallas_call(
        _proj_kernel,
        grid=(b, nt),
        in_specs=in_specs,
        out_specs=out_specs,
        out_shape=out_shape,
        scratch_shapes=[pltpu.VMEM((2 * SUBLANES, D_QKV), F32)],
        compiler_params=pltpu.CompilerParams(
            dimension_semantics=("arbitrary", "arbitrary"), vmem_limit_bytes=VMEM_LIMIT),
        name="proj",
    )(*ins)


def _neumann_many(a_list):
    c = a_list[0].shape[0]
    p = [-a for a in a_list]
    pb = [x.astype(BF16) for x in p]
    q = [_dot(x, x) for x in pb]
    yield
    power = 2
    while 2 * power < c:
        qb = [x.astype(BF16) for x in q]
        r = [_dot(jnp.concatenate([pi.astype(BF16), qi], axis=0), qi) for pi, qi in zip(p, qb)]
        yield
        p = [pi + qi + ri[:c] for pi, qi, ri in zip(p, q, r)]
        q = [ri[c:] for ri in r]
        power *= 2
    tail = [_dot(pi.astype(BF16), qi.astype(BF16)) for pi, qi in zip(p, q)]
    yield
    return [pi + qi + ti for pi, qi, ti in zip(p, q, tail)]


def _interleave(steps, fillers):
    fillers = list(fillers)
    count = 0
    while True:
        try:
            next(steps)
        except StopIteration as stop:
            for f in fillers:
                f()
            return stop.value
        count += 1
        if fillers and count % 2 == 0:
            fillers.pop(0)()


def _gdn_kernel(qs_ref, k_ref, v_ref, col_ref, row_ref, zs_ref, gnorm_ref,
                og_ref, state_ref, s_scr):
    rows = qs_ref.shape[0]
    c_ = GDN_CHUNK
    t = pl.program_id(1)

    @pl.when(t == 0)
    def _():
        s_scr[...] = jnp.zeros(s_scr.shape, F32)

    ri = lax.broadcasted_iota(jnp.int32, (c_, c_), 0)
    ci = lax.broadcasted_iota(jnp.int32, (c_, c_), 1)
    incl = ri >= ci
    strict = ri > ci
    gnorm = gnorm_ref[...]

    def prepare(chunks):
        pairs = [(c, hh) for c in chunks for hh in range(HEADS)]
        cols = {c: col_ref[c * c_:(c + 1) * c_, :] for c in chunks}
        rws = {c: row_ref[c] for c in chunks}
        beta = [cols[c][:, hh:hh + 1] for c, hh in pairs]
        gc = [cols[c][:, HEADS + hh:HEADS + hh + 1] for c, hh in pairs]
        gr = [rws[c][HEADS + hh:HEADS + hh + 1, :] for c, hh in pairs]
        glast = [cols[c][c_ - 1:c_, HEADS + hh:HEADS + hh + 1] for c, hh in pairs]
        kh = [k_ref[c * c_:(c + 1) * c_, hh * DH:(hh + 1) * DH] for c, hh in pairs]
        qh = [qs_ref[c * c_:(c + 1) * c_, hh * DH:(hh + 1) * DH] for c, hh in pairs]
        kf = [x.astype(F32) for x in kh]
        kb = [x * b for x, b in zip(kf, beta)]
        m1 = [_dot_nt(jnp.concatenate([x.astype(BF16), q], axis=0), k) for x, q, k in zip(kb, qh, kh)]
        yield
        decay = [jnp.exp(jnp.where(incl, a - b, -jnp.inf)) for a, b in zip(gc, gr)]
        attn = [(m[c_:] * d).astype(BF16) for m, d in zip(m1, decay)]
        p = yield from _neumann_many([jnp.where(strict, m[:c_] * d, 0.0) for m, d in zip(m1, decay)])
        eg = [jnp.exp(x) for x in gc]
        rhs = [jnp.concatenate([x * e, v_ref[c * c_:(c + 1) * c_, hh * DH:(hh + 1) * DH].astype(F32) * b], axis=1)
               for x, e, b, (c, hh) in zip(kb, eg, beta, pairs)]
        wu = [(x + _dot(pi.astype(BF16), x.astype(BF16))).astype(BF16) for x, pi in zip(rhs, p)]
        yield
        ke_t = [jnp.transpose((x * jnp.exp(gl - g)).astype(BF16)) for x, gl, g in zip(kf, glast, gc)]
        r1 = [_dot(jnp.concatenate([kt, a], axis=0), x) for kt, a, x in zip(ke_t, attn, wu)]
        yield
        staged = {}
        for i, pair in enumerate(pairs):
            q_eff = qh[i].astype(F32) * eg[i] - r1[i][DH:, :DH]
            mq = jnp.concatenate([(-r1[i][:DH, :DH]).astype(BF16), q_eff.astype(BF16)], axis=0)
            staged[pair] = (mq, r1[i][:DH, DH:], r1[i][DH:, DH:], jnp.exp(glast[i]))
        return staged

    state = [s_scr[hh] for hh in range(HEADS)]

    def advance(c, staged):
        r0 = c * c_
        for hh in range(HEADS):
            lo = hh * DH
            mq, n_mat, o_zero, e_last = staged[c, hh]
            r = _dot(mq, state[hh].astype(BF16))
            state[hh] = state[hh] * e_last + r[:DH] + n_mat
            og = _rms(r[DH:] + o_zero, gnorm) * zs_ref[r0:r0 + c_, lo:lo + DH].astype(F32)
            og_ref[r0:r0 + c_, lo:lo + DH] = og.astype(BF16)

    nchunk = rows // c_
    fillers = []
    for g0 in range(0, nchunk, GDN_GROUP):
        group = list(range(g0, g0 + GDN_GROUP))
        staged = _interleave(prepare(group), fillers)
        fillers = [functools.partial(advance, c, staged) for c in group]
    for f in fillers:
        f()
    for hh in range(HEADS):
        s_scr[hh] = state[hh]

    @pl.when(t == pl.num_programs(1) - 1)
    def _():
        state_ref[...] = s_scr[...]


def _gdn_call(qs, k, v, col, row, zs, gnorm):
    b, l, _ = qs.shape
    rows = GDN_ROWS
    nt = l // rows
    nchunk = rows // GDN_CHUNK
    tile = lambda w: pl.BlockSpec((None, rows, w), lambda i, j: (i, j, 0))
    return pl.pallas_call(
        _gdn_kernel,
        grid=(b, nt),
        in_specs=[tile(D_GDN)] * 3 + [
            tile(LANES),
            pl.BlockSpec((None, nchunk, 2 * HEADS, GDN_CHUNK), lambda i, j: (i, j, 0, 0)),
            tile(D_GDN),
            _resident(gnorm.shape),
        ],
        out_specs=[tile(D_GDN), pl.BlockSpec((None, HEADS, DH, DH), lambda i, j: (i, 0, 0, 0))],
        out_shape=[jax.ShapeDtypeStruct((b, l, D_GDN), BF16),
                   jax.ShapeDtypeStruct((b, HEADS, DH, DH), F32)],
        scratch_shapes=[pltpu.VMEM((HEADS, DH, DH), F32)],
        compiler_params=pltpu.CompilerParams(
            dimension_semantics=("arbitrary", "arbitrary"), vmem_limit_bytes=VMEM_LIMIT),
        name="gdn",
    )(qs, k, v, col, row, zs, gnorm)


def _tail_kernel(x_ref, og_ref, sg_ref, p_ref, wout_ref, gff_ref, wup_ref, wdown_ref,
                 gple_ref, wple_ref, wgate_ref, gfin_ref, y_ref):
    x1 = x_ref[...] + _dot(og_ref[...], wout_ref[0:D_GDN, :]) + _dot(sg_ref[...], wout_ref[D_GDN:, :])
    h2 = _rms(x1, gff_ref[...]).astype(BF16)
    y_ref[...] = x1
    ff_block = D_FF // 4
    for c in range(D_FF // ff_block):
        lo = c * ff_block
        hid = jnp.maximum(_dot(h2, wup_ref[:, lo:lo + ff_block]), 0.0)
        y_ref[...] += _dot((hid * hid).astype(BF16), wdown_ref[lo:lo + ff_block, :])
    acc = y_ref[...]
    h3 = _rms(acc, gple_ref[...]).astype(BF16)
    gate = _sigmoid(_dot(h3, wgate_ref[...]))
    x3 = acc + _dot(p_ref[...].astype(BF16), wple_ref[...]) * gate
    y_ref[...] = _rms(x3, gfin_ref[...])


def _tail_call(x, og, sg, p, wout, gff, wup, wdown, gple, wple, wgate, gfin, rows):
    n = x.shape[0]
    tile = lambda w: pl.BlockSpec((rows, w), lambda i: (i, 0))
    weights = [wout, gff, wup, wdown, gple, wple, wgate, gfin]
    return pl.pallas_call(
        _tail_kernel,
        grid=(n // rows,),
        in_specs=[tile(D_MODEL), tile(D_GDN), tile(D_SGU), tile(D_PLE)] + [_resident(w.shape) for w in weights],
        out_specs=tile(D_MODEL),
        out_shape=jax.ShapeDtypeStruct((n, D_MODEL), F32),
        compiler_params=pltpu.CompilerParams(
            dimension_semantics=("arbitrary",), vmem_limit_bytes=VMEM_LIMIT),
        name="tail",
    )(x, og, sg, p, *weights)


def _sample_proj_kernel(x_ref, hist_ref, gmix_ref, w_ref, wconv_ref,
                        alog_row_ref, dtb_row_ref, lng_ref, lnb_ref, ws_ref, bs_ref,
                        nconv_ref, vg_ref, sg_ref, zs_ref, v_ref, col_ref, qt_ref, kt_ref):
    n = x_ref.shape[0]
    h = _rms(x_ref[...], gmix_ref[...]).astype(BF16)
    pre = _dot(h, w_ref[:, COL_Q:COL_Q + D_QKV])
    y = pre * wconv_ref[CONV_W - 1:CONV_W, :]
    for j in range(CONV_W - 1):
        y = y + hist_ref[j] * wconv_ref[j:j + 1, :]
        if j > 0:
            nconv_ref[j - 1] = hist_ref[j]
    nconv_ref[CONV_W - 2] = pre
    y = _silu(y)
    v_ref[...] = y[:, 2 * D_GDN:]
    blocks = n // SAMPLE_BLOCK
    for hh in range(HEADS):
        lo = hh * DH
        qt = jnp.transpose(_l2n(y[:, lo:lo + DH]) * (DH ** -0.5))
        kt = jnp.transpose(_l2n(y[:, D_GDN + lo:D_GDN + lo + DH]))
        for i in range(blocks):
            qt_ref[i, hh] = qt[:, i * SAMPLE_BLOCK:(i + 1) * SAMPLE_BLOCK]
            kt_ref[i, hh] = kt[:, i * SAMPLE_BLOCK:(i + 1) * SAMPLE_BLOCK]
    zs_ref[...] = _silu(_dot(h, w_ref[:, COL_Z:COL_Z + D_GDN]))
    ab = _dot(h, w_ref[:, COL_AB:COL_AB + LANES])
    lane = lax.broadcasted_iota(jnp.int32, ab.shape, 1)
    g_col = -jnp.exp(alog_row_ref[...]) * _softplus(ab + dtb_row_ref[...])
    col_ref[...] = jnp.where(lane < HEADS, _sigmoid(ab), g_col)
    uv = _gelu_tanh(_dot(h, w_ref[:, COL_U:COL_U + 2 * D_SGU]))
    vg = _layer_norm(uv[:, D_SGU:], lng_ref[...], lnb_ref[...])
    vg_ref[...] = vg
    for hh in range(HEADS):
        lo = hh * DH
        mix = ws_ref[hh, 0:1, 0:1] * vg[:, lo:lo + DH] + bs_ref[hh:hh + 1, 0:1]
        sg_ref[:, lo:lo + DH] = (uv[:, lo:lo + DH] * mix).astype(BF16)


def _sample_proj_call(x, hist, gmix, w_all, wconv, alog_row, dtb_row, lng, lnb, ws, bs):
    n = x.shape[0]
    blocks = n // SAMPLE_BLOCK
    out_shape = [
        jax.ShapeDtypeStruct((CONV_W - 1, n, D_QKV), F32),
        jax.ShapeDtypeStruct((n, D_SGU), F32),
        jax.ShapeDtypeStruct((n, D_SGU), BF16),
        jax.ShapeDtypeStruct((n, D_GDN), F32),
        jax.ShapeDtypeStruct((n, D_GDN), F32),
        jax.ShapeDtypeStruct((n, LANES), F32),
        jax.ShapeDtypeStruct((blocks, HEADS, DH, SAMPLE_BLOCK), F32),
        jax.ShapeDtypeStruct((blocks, HEADS, DH, SAMPLE_BLOCK), F32),
    ]
    return pl.pallas_call(
        _sample_proj_kernel,
        out_shape=out_shape,
        compiler_params=pltpu.CompilerParams(vmem_limit_bytes=VMEM_LIMIT),
        name="sample_proj",
    )(x, hist, gmix, w_all, wconv, alog_row, dtb_row, lng, lnb, ws, bs)


def _sample_gdn_kernel(s_ref, qt_ref, kt_ref, v_ref, col_ref, zs_ref, gnorm_ref, og_ref, snew_ref):
    gnorm = gnorm_ref[...]
    for b0 in range(0, SAMPLE_BLOCK, SAMPLE_GROUP):
        pairs = [(bb, hh) for bb in range(b0, b0 + SAMPLE_GROUP) for hh in range(HEADS)]
        kcol = [kt_ref[hh, :, bb:bb + 1] for bb, hh in pairs]
        qcol = [qt_ref[hh, :, bb:bb + 1] for bb, hh in pairs]
        beta = [col_ref[bb:bb + 1, hh:hh + 1] for bb, hh in pairs]
        eg = [jnp.exp(col_ref[bb:bb + 1, HEADS + hh:HEADS + hh + 1]) for bb, hh in pairs]
        sd = [s_ref[bb, hh] * e for (bb, hh), e in zip(pairs, eg)]
        sk = [jnp.sum(s * kc, axis=0, keepdims=True) for s, kc in zip(sd, kcol)]
        v_new = [b * (v_ref[bb:bb + 1, hh * DH:(hh + 1) * DH] - x) for (bb, hh), b, x in zip(pairs, beta, sk)]
        s1 = [s + kc * vn for s, kc, vn in zip(sd, kcol, v_new)]
        for (bb, hh), s in zip(pairs, s1):
            snew_ref[bb, hh] = s
        o = [jnp.sum(s * qc, axis=0, keepdims=True) for s, qc in zip(s1, qcol)]
        for (bb, hh), x in zip(pairs, o):
            lo = hh * DH
            og = _rms(x, gnorm) * zs_ref[bb:bb + 1, lo:lo + DH]
            og_ref[bb:bb + 1, lo:lo + DH] = og.astype(BF16)


def _sample_gdn_call(state, qt, kt, v, col, zs, gnorm):
    n = state.shape[0]
    blk = SAMPLE_BLOCK
    rowtile = lambda w: pl.BlockSpec((blk, w), lambda i: (i, 0))
    return pl.pallas_call(
        _sample_gdn_kernel,
        grid=(n // blk,),
        in_specs=[
            pl.BlockSpec((blk, HEADS, DH, DH), lambda i: (i, 0, 0, 0)),
            pl.BlockSpec((None, HEADS, DH, blk), lambda i: (i, 0, 0, 0)),
            pl.BlockSpec((None, HEADS, DH, blk), lambda i: (i, 0, 0, 0)),
            rowtile(D_GDN), rowtile(LANES), rowtile(D_GDN),
            _resident(gnorm.shape),
        ],
        out_specs=[rowtile(D_GDN), pl.BlockSpec((blk, HEADS, DH, DH), lambda i: (i, 0, 0, 0))],
        out_shape=[jax.ShapeDtypeStruct((n, D_GDN), BF16),
                   jax.ShapeDtypeStruct((n, HEADS, DH, DH), F32)],
        compiler_params=pltpu.CompilerParams(
            dimension_semantics=("arbitrary",), vmem_limit_bytes=VMEM_LIMIT),
        name="sample_gdn",
    )(state, qt, kt, v, col, zs, gnorm)


def kernel(x_prompt, x_sample, state_conv, state_gdn, p_prompt, p_sample, g_mix, w_in, w_conv, a_log, dt_bias, gdn_norm, sgu_ln_g, sgu_ln_b, w_s, b_s, w_out, g_ff, w_up, w_down, g_ple, w_ple, w_ple_gate, g_final):
    depth = w_in.shape[0]
    assert depth == 1, "single-layer trunk only"
    b, l, d = x_prompt.shape
    n = x_sample.shape[0]
    assert x_sample.shape[1] == 1 and d == D_MODEL
    assert l % PROJ_ROWS == 0 and l % GDN_ROWS == 0 and (b * l) % TAIL_ROWS == 0 and n % SAMPLE_BLOCK == 0

    w_all, wout, wup, wdown, wple, wgate, wabt = _prep_call(w_in, w_out, w_up, w_down, w_ple, w_ple_gate)
    row2 = lambda a: a.reshape(1, -1)
    pad_lanes = lambda a: jnp.pad(a, (HEADS, LANES - 2 * HEADS)).reshape(1, LANES)
    pad_rows = lambda a: jnp.pad(a, (HEADS, 0)).reshape(2 * HEADS, 1)
    alog_row, dtb_row = pad_lanes(a_log[0]), pad_lanes(dt_bias[0])
    alog_col, dtb_col = pad_rows(a_log[0]), pad_rows(dt_bias[0])
    gmix, gff, gple, gfin = row2(g_mix[0]), row2(g_ff[0]), row2(g_ple[0]), row2(g_final)
    lng, lnb, gnorm = row2(sgu_ln_g[0]), row2(sgu_ln_b[0]), row2(gdn_norm[0])
    tail_w = (wout, gff, wup, wdown, gple, wple, wgate, gfin)

    qs, k, v, zs, sg, col, row, conv_p = _proj_call(
        x_prompt, gmix, w_all, wabt, w_conv[0], alog_row, dtb_row, alog_col, dtb_col,
        lng, lnb, w_s[0], b_s[0].T)
    og, gdn_p = _gdn_call(qs, k, v, col, row, zs, gnorm)
    y_prompt = _tail_call(x_prompt.reshape(b * l, d), og.reshape(b * l, D_GDN), sg.reshape(b * l, D_SGU),
                          p_prompt[0].reshape(b * l, D_PLE), *tail_w, rows=TAIL_ROWS).reshape(b, l, d)

    nconv, vg_s, sg_s, zs_s, v_s, col_s, qt, kt = _sample_proj_call(
        x_sample.reshape(n, d), jnp.transpose(state_conv[0], (1, 0, 2)), gmix, w_all,
        w_conv[0], alog_row, dtb_row, lng, lnb, w_s[0], b_s[0])
    og_s, gdn_s = _sample_gdn_call(state_gdn[0], qt, kt, v_s, col_s, zs_s, gnorm)
    y_sample = _tail_call(x_sample.reshape(n, d), og_s, sg_s, p_sample[0].reshape(n, D_PLE), *tail_w,
                          rows=n).reshape(n, 1, d)

    return (y_prompt, y_sample,
            conv_p[None], gdn_p[None],
            jnp.transpose(nconv, (1, 0, 2))[None], gdn_s[None],
            vg_s.reshape(n, 1, D_SGU)[None])
```

```python
import functools

import jax
import jax.numpy as jnp
from jax import lax
from jax.experimental import pallas as pl
from jax.experimental.pallas import tpu as pltpu

F32 = jnp.float32
BF16 = jnp.bfloat16

D_MODEL = 1024
D_GDN = 512
D_SGU = 512
HEADS = 4
DH = 128
D_QKV = 3 * D_GDN
D_FF = 4 * D_MODEL
D_PLE = 256
CONV_W = 4
GDN_CHUNK = 64
SGU_CHUNK = 128
EPS = 1e-6
LANES = 128
SUBLANES = 8
VMEM_LIMIT = 56 * 1024 * 1024

COL_Q, COL_K, COL_V, COL_Z, COL_U, COL_VV, COL_AB = 0, 512, 1024, 1536, 2048, 2560, 3072
W_COLS = COL_AB + LANES

PROJ_ROWS = 1024
CONV_BLOCK = 8
GDN_ROWS = 1024
GDN_GROUP = 4
TAIL_ROWS = 512
SAMPLE_BLOCK = 8
SAMPLE_GROUP = 4
PREP_STEPS = 8


def _dot(a, b):
    return jnp.dot(a, b, preferred_element_type=F32)


def _dot_nt(a, b):
    return lax.dot_general(a, b, (((1,), (1,)), ((), ())), preferred_element_type=F32)


def _dot_tn(a, b):
    return lax.dot_general(a, b, (((0,), (0,)), ((), ())), preferred_element_type=F32)


def _split3(x):
    hi = x.astype(BF16)
    r1 = x - hi.astype(F32)
    mid = r1.astype(BF16)
    lo = (r1 - mid.astype(F32)).astype(BF16)
    return hi, mid, lo


def _rms(x, gain):
    return x * lax.rsqrt(jnp.mean(x * x, axis=-1, keepdims=True) + EPS) * gain


def _sigmoid(x):
    return 1.0 / (1.0 + jnp.exp(-x))


def _silu(x):
    hx = 0.5 * x
    return hx + hx * jnp.tanh(hx)


def _softplus(x):
    return jnp.maximum(x, 0.0) + jnp.log(1.0 + jnp.exp(-jnp.abs(x)))


def _gelu_tanh(x):
    c = 0.7978845608028654
    hx = 0.5 * x
    return hx + hx * jnp.tanh(x * (c + (c * 0.044715) * (x * x)))


def _l2n(x):
    return x * lax.rsqrt(jnp.sum(x * x, axis=-1, keepdims=True) + 1e-6)


def _layer_norm(x, gain, bias):
    mu = jnp.mean(x, axis=-1, keepdims=True)
    xc = x - mu
    var = jnp.mean(xc * xc, axis=-1, keepdims=True)
    return xc * lax.rsqrt(var + EPS) * gain + bias


def _resident(shape):
    nd = len(shape)
    return pl.BlockSpec(shape, lambda *_: (0,) * nd, pipeline_mode=pl.Buffered(1))


def _prep_kernel(win_ref, wout_ref, wup_ref, wdown_ref, wple_ref, wgate_ref,
                 wall_o, wout_o, wup_o, wdown_o, wple_o, wgate_o, wabt_o):
    o1 = D_QKV + D_GDN
    o3 = o1 + 2 * HEADS

    def put(dst, src):
        wall_o[:, dst:dst + LANES] = jnp.transpose(win_ref[src:src + LANES, :]).astype(BF16)

    for j in range(o1 // LANES):
        put(j * LANES, j * LANES)
    for j in range(2 * D_SGU // LANES):
        put(COL_U + j * LANES, o3 + j * LANES)
    ab = win_ref[o1:o3, :]
    ab16 = jnp.concatenate([ab, jnp.zeros((2 * SUBLANES - 2 * HEADS, LANES), F32)], axis=0)
    wabt_o[...] = ab16.astype(BF16)
    ab128 = jnp.concatenate([ab, jnp.zeros((LANES - 2 * HEADS, LANES), F32)], axis=0)
    wall_o[:, COL_AB:] = jnp.transpose(ab128).astype(BF16)
    wout_o[...] = wout_ref[...].astype(BF16)
    wup_o[...] = wup_ref[...].astype(BF16)
    wdown_o[...] = wdown_ref[...].astype(BF16)
    wple_o[...] = wple_ref[...].astype(BF16)
    wgate_o[...] = wgate_ref[...].astype(BF16)


def _prep_call(w_in, w_out, w_up, w_down, w_ple, w_ple_gate):
    steps = PREP_STEPS
    ins = [w_in, w_out, w_up, w_down, w_ple, w_ple_gate]
    out_cols = [W_COLS] + [w.shape[2] for w in ins[1:]]
    block = lambda w, cols: pl.BlockSpec((None, w.shape[1] // steps, cols), lambda i: (0, i, 0))
    out_block = lambda w, cols: pl.BlockSpec((w.shape[1] // steps, cols), lambda i: (i, 0))
    d, d_in = w_in.shape[1], w_in.shape[2]
    assert d // steps == LANES
    win_t = jnp.transpose(w_in[0])
    return pl.pallas_call(
        _prep_kernel,
        grid=(steps,),
        in_specs=[pl.BlockSpec((d_in, LANES), lambda i: (0, i))] + [block(w, w.shape[2]) for w in ins[1:]],
        out_specs=[out_block(w, c) for w, c in zip(ins, out_cols)]
        + [pl.BlockSpec((2 * SUBLANES, LANES), lambda i: (0, i))],
        out_shape=[jax.ShapeDtypeStruct((w.shape[1], c), BF16) for w, c in zip(ins, out_cols)]
        + [jax.ShapeDtypeStruct((2 * SUBLANES, d), BF16)],
        compiler_params=pltpu.CompilerParams(
            dimension_semantics=("arbitrary",), vmem_limit_bytes=VMEM_LIMIT),
        name="prep",
    )(win_t, *ins[1:])


def _proj_kernel(x_ref, gmix_ref, w_ref, wabt_ref, wconv_ref,
                 alog_row_ref, dtb_row_ref, alog_col_ref, dtb_col_ref, lng_ref, lnb_ref,
                 ws_ref, bst_ref,
                 qs_ref, k_ref, v_ref, zs_ref, sg_ref, col_ref, row_ref, cstate_ref,
                 cbuf):
    rows = x_ref.shape[0]
    first_tile = pl.program_id(1) == 0
    h = _rms(x_ref[...], gmix_ref[...]).astype(BF16)

    def project(lo, width=D_GDN):
        return _dot(h, w_ref[:, lo:lo + width])

    def conv_silu(pre, lo):
        cols = slice(lo, lo + D_GDN)
        groups = rows // SUBLANES
        tap = lambda j: wconv_ref[j:j + 1, cols].reshape(1, 1, D_GDN)
        shape3 = (1, SUBLANES, D_GDN)
        last_p = jnp.where(first_tile, 0.0, cbuf[0:SUBLANES, cols]).reshape(shape3)
        last_b = jnp.where(first_tile, 0.0, cbuf[SUBLANES:, cols]).reshape(shape3)
        cstate_ref[:, cols] = pre[rows - (CONV_W - 1):rows, :]
        cur = pre.reshape(groups, SUBLANES, D_GDN)
        sub = lax.broadcasted_iota(jnp.int32, (CONV_BLOCK, SUBLANES, D_GDN), 1)

        def delay(a, a_last, s):
            rolled = pltpu.roll(a, s, axis=1)
            rolled_prev = jnp.concatenate([pltpu.roll(a_last, s, axis=1), rolled[:-1]], axis=0)
            return jnp.where(sub >= s, rolled, rolled_prev)

        assert CONV_W == 4
        out = []
        for g0 in range(0, groups, CONV_BLOCK):
            p = cur[g0:g0 + CONV_BLOCK]
            dp = delay(p, last_p, 1)
            b = p * tap(1) + dp * tap(0)
            y = p * tap(3) + dp * tap(2) + delay(b, last_b, 2)
            last_p, last_b = p[CONV_BLOCK - 1:], b[CONV_BLOCK - 1:]
            out.append(_silu(y))
        cbuf[0:SUBLANES, cols] = last_p.reshape(SUBLANES, D_GDN)
        cbuf[SUBLANES:, cols] = last_b.reshape(SUBLANES, D_GDN)
        return jnp.concatenate(out, axis=0).reshape(rows, D_GDN)

    pre_vv = project(COL_VV)
    pre_u = project(COL_U)
    vg = _layer_norm(_gelu_tanh(pre_vv), lng_ref[...], lnb_ref[...]).astype(BF16)
    ab = project(COL_AB, LANES)
    pre_q = project(COL_Q)
    u_act = _gelu_tanh(pre_u)
    ti = lax.broadcasted_iota(jnp.int32, (SGU_CHUNK, SGU_CHUNK), 0)
    si = lax.broadcasted_iota(jnp.int32, (SGU_CHUNK, SGU_CHUNK), 1)
    for hh in range(HEADS):
        lo = hh * DH
        w_tril = jnp.where(ti >= si, ws_ref[hh], 0.0).astype(BF16)
        bias = bst_ref[:, hh:hh + 1]
        for c in range(rows // SGU_CHUNK):
            r0 = c * SGU_CHUNK
            mix = _dot(w_tril, vg[r0:r0 + SGU_CHUNK, lo:lo + DH]) + bias
            sg_ref[r0:r0 + SGU_CHUNK, lo:lo + DH] = (u_act[r0:r0 + SGU_CHUNK, lo:lo + DH] * mix).astype(BF16)
    pre_k = project(COL_K)

    lane = lax.broadcasted_iota(jnp.int32, (GDN_CHUNK, LANES), 1)
    g_col = -jnp.exp(alog_row_ref[...]) * _softplus(ab + dtb_row_ref[...])
    abt = _dot_nt(wabt_ref[...], h)[0:2 * HEADS, :]
    sub = lax.broadcasted_iota(jnp.int32, (2 * HEADS, GDN_CHUNK), 0)
    g_row = -jnp.exp(alog_col_ref[...]) * _softplus(abt + dtb_col_ref[...])
    ri = lax.broadcasted_iota(jnp.int32, (GDN_CHUNK, GDN_CHUNK), 0)
    ci = lax.broadcasted_iota(jnp.int32, (GDN_CHUNK, GDN_CHUNK), 1)
    tri_l = (ri >= ci).astype(BF16)
    tri_u = (ri <= ci).astype(BF16)
    beta_col = _sigmoid(ab)
    beta_row = _sigmoid(abt)
    for c in range(rows // GDN_CHUNK):
        r0 = c * GDN_CHUNK
        p0, p1, p2 = _split3(g_col[r0:r0 + GDN_CHUNK, :])
        gc = _dot(tri_l, p0) + _dot(tri_l, p1) + _dot(tri_l, p2)
        col_ref[r0:r0 + GDN_CHUNK, :] = jnp.where(lane < HEADS, beta_col[r0:r0 + GDN_CHUNK, :], gc)
        p0, p1, p2 = _split3(g_row[:, r0:r0 + GDN_CHUNK])
        gr = _dot(p0, tri_u) + _dot(p1, tri_u) + _dot(p2, tri_u)
        row_ref[c] = jnp.where(sub < HEADS, beta_row[:, r0:r0 + GDN_CHUNK], gr)

    y = conv_silu(pre_q, COL_Q)
    for hh in range(HEADS):
        lo = hh * DH
        qs_ref[:, lo:lo + DH] = (_l2n(y[:, lo:lo + DH]) * (DH ** -0.5)).astype(BF16)
    pre_v = project(COL_V)
    y = conv_silu(pre_k, COL_K)
    for hh in range(HEADS):
        lo = hh * DH
        k_ref[:, lo:lo + DH] = _l2n(y[:, lo:lo + DH]).astype(BF16)
    pre_z = project(COL_Z)
    v_ref[...] = conv_silu(pre_v, COL_V).astype(BF16)
    zs_ref[...] = _silu(pre_z).astype(BF16)


def _proj_call(x, gmix, w_all, wabt, wconv, alog_row, dtb_row, alog_col, dtb_col, lng, lnb, ws, bst):
    b, l, _ = x.shape
    rows = PROJ_ROWS
    nt = l // rows
    nchunk = rows // GDN_CHUNK
    tile = lambda w: pl.BlockSpec((None, rows, w), lambda i, j: (i, j, 0))
    ins = [x, gmix, w_all, wabt, wconv, alog_row, dtb_row, alog_col, dtb_col, lng, lnb, ws, bst]
    in_specs = [tile(D_MODEL)] + [_resident(a.shape) for a in ins[1:]]
    out_shape = [jax.ShapeDtypeStruct((b, l, D_GDN), BF16)] * 5 + [
        jax.ShapeDtypeStruct((b, l, LANES), F32),
        jax.ShapeDtypeStruct((b, l // GDN_CHUNK, 2 * HEADS, GDN_CHUNK), F32),
        jax.ShapeDtypeStruct((b, CONV_W - 1, D_QKV), F32),
    ]
    out_specs = [tile(D_GDN)] * 5 + [
        tile(LANES),
        pl.BlockSpec((None, nchunk, 2 * HEADS, GDN_CHUNK), lambda i, j: (i, j, 0, 0)),
        pl.BlockSpec((None, CONV_W - 1, D_QKV), lambda i, j: (i, 0, 0)),
    ]
    return pl.pallas_call(
        _proj_kernel,
        grid=(b, nt),
        in_specs=in_specs,
        out_specs=out_specs,
        out_shape=out_shape,
        scratch_shapes=[pltpu.VMEM((2 * SUBLANES, D_QKV), F32)],
        compiler_params=pltpu.CompilerParams(
            dimension_semantics=("arbitrary", "arbitrary"), vmem_limit_bytes=VMEM_LIMIT),
        name="proj",
    )(*ins)


def _neumann_many(a_list):
    c = a_list[0].shape[0]
    p = [-a for a in a_list]
    pb = [x.astype(BF16) for x in p]
    q = [_dot(x, x) for x in pb]
    yield
    power = 2
    while 2 * power < c:
        qb = [x.astype(BF16) for x in q]
        r = [_dot(jnp.concatenate([pi.astype(BF16), qi], axis=0), qi) for pi, qi in zip(p, qb)]
        yield
        p = [pi + qi + ri[:c] for pi, qi, ri in zip(p, q, r)]
        q = [ri[c:] for ri in r]
        power *= 2
    tail = [_dot(pi.astype(BF16), qi.astype(BF16)) for pi, qi in zip(p, q)]
    yield
    return [pi + qi + ti for pi, qi, ti in zip(p, q, tail)]


def _interleave(steps, fillers):
    fillers = list(fillers)
    count = 0
    while True:
        try:
            next(steps)
        except StopIteration as stop:
            for f in fillers:
                f()
            return stop.value
        count += 1
        if fillers and count % 2 == 0:
            fillers.pop(0)()


def _gdn_kernel(qs_ref, k_ref, v_ref, col_ref, row_ref, zs_ref, gnorm_ref,
                og_ref, state_ref, s_scr):
    rows = qs_ref.shape[0]
    c_ = GDN_CHUNK
    t = pl.program_id(1)

    @pl.when(t == 0)
    def _():
        s_scr[...] = jnp.zeros(s_scr.shape, F32)

    ri = lax.broadcasted_iota(jnp.int32, (c_, c_), 0)
    ci = lax.broadcasted_iota(jnp.int32, (c_, c_), 1)
    incl = ri >= ci
    strict = ri > ci
    gnorm = gnorm_ref[...]

    def prepare(chunks):
        pairs = [(c, hh) for c in chunks for hh in range(HEADS)]
        cols = {c: col_ref[c * c_:(c + 1) * c_, :] for c in chunks}
        rws = {c: row_ref[c] for c in chunks}
        beta = [cols[c][:, hh:hh + 1] for c, hh in pairs]
        gc = [cols[c][:, HEADS + hh:HEADS + hh + 1] for c, hh in pairs]
        gr = [rws[c][HEADS + hh:HEADS + hh + 1, :] for c, hh in pairs]
        glast = [cols[c][c_ - 1:c_, HEADS + hh:HEADS + hh + 1] for c, hh in pairs]
        kh = [k_ref[c * c_:(c + 1) * c_, hh * DH:(hh + 1) * DH] for c, hh in pairs]
        qh = [qs_ref[c * c_:(c + 1) * c_, hh * DH:(hh + 1) * DH] for c, hh in pairs]
        kf = [x.astype(F32) for x in kh]
        kb = [x * b for x, b in zip(kf, beta)]
        m1 = [_dot_nt(jnp.concatenate([x.astype(BF16), q], axis=0), k) for x, q, k in zip(kb, qh, kh)]
        yield
        decay = [jnp.exp(jnp.where(incl, a - b, -jnp.inf)) for a, b in zip(gc, gr)]
        attn = [(m[c_:] * d).astype(BF16) for m, d in zip(m1, decay)]
        p = yield from _neumann_many([jnp.where(strict, m[:c_] * d, 0.0) for m, d in zip(m1, decay)])
        eg = [jnp.exp(x) for x in gc]
        rhs = [jnp.concatenate([x * e, v_ref[c * c_:(c + 1) * c_, hh * DH:(hh + 1) * DH].astype(F32) * b], axis=1)
               for x, e, b, (c, hh) in zip(kb, eg, beta, pairs)]
        wu = [(x + _dot(pi.astype(BF16), x.astype(BF16))).astype(BF16) for x, pi in zip(rhs, p)]
        yield
        ke_t = [jnp.transpose((x * jnp.exp(gl - g)).astype(BF16)) for x, gl, g in zip(kf, glast, gc)]
        r1 = [_dot(jnp.concatenate([kt, a], axis=0), x) for kt, a, x in zip(ke_t, attn, wu)]
        yield
        staged = {}
        for i, pair in enumerate(pairs):
            q_eff = qh[i].astype(F32) * eg[i] - r1[i][DH:, :DH]
            mq = jnp.concatenate([(-r1[i][:DH, :DH]).astype(BF16), q_eff.astype(BF16)], axis=0)
            staged[pair] = (mq, r1[i][:DH, DH:], r1[i][DH:, DH:], jnp.exp(glast[i]))
        return staged

    state = [s_scr[hh] for hh in range(HEADS)]

    def advance(c, staged):
        r0 = c * c_
        for hh in range(HEADS):
            lo = hh * DH
            mq, n_mat, o_zero, e_last = staged[c, hh]
            r = _dot(mq, state[hh].astype(BF16))
            state[hh] = state[hh] * e_last + r[:DH] + n_mat
            og = _rms(r[DH:] + o_zero, gnorm) * zs_ref[r0:r0 + c_, lo:lo + DH].astype(F32)
            og_ref[r0:r0 + c_, lo:lo + DH] = og.astype(BF16)

    nchunk = rows // c_
    fillers = []
    for g0 in range(0, nchunk, GDN_GROUP):
        group = list(range(g0, g0 + GDN_GROUP))
        staged = _interleave(prepare(group), fillers)
        fillers = [functools.partial(advance, c, staged) for c in group]
    for f in fillers:
        f()
    for hh in range(HEADS):
        s_scr[hh] = state[hh]

    @pl.when(t == pl.num_programs(1) - 1)
    def _():
        state_ref[...] = s_scr[...]


def _gdn_call(qs, k, v, col, row, zs, gnorm):
    b, l, _ = qs.shape
    rows = GDN_ROWS
    nt = l // rows
    nchunk = rows // GDN_CHUNK
    tile = lambda w: pl.BlockSpec((None, rows, w), lambda i, j: (i, j, 0))
    return pl.pallas_call(
        _gdn_kernel,
        grid=(b, nt),
        in_specs=[tile(D_GDN)] * 3 + [
            tile(LANES),
            pl.BlockSpec((None, nchunk, 2 * HEADS, GDN_CHUNK), lambda i, j: (i, j, 0, 0)),
            tile(D_GDN),
            _resident(gnorm.shape),
        ],
        out_specs=[tile(D_GDN), pl.BlockSpec((None, HEADS, DH, DH), lambda i, j: (i, 0, 0, 0))],
        out_shape=[jax.ShapeDtypeStruct((b, l, D_GDN), BF16),
                   jax.ShapeDtypeStruct((b, HEADS, DH, DH), F32)],
        scratch_shapes=[pltpu.VMEM((HEADS, DH, DH), F32)],
        compiler_params=pltpu.CompilerParams(
            dimension_semantics=("arbitrary", "arbitrary"), vmem_limit_bytes=VMEM_LIMIT),
        name="gdn",
    )(qs, k, v, col, row, zs, gnorm)


def _tail_kernel(x_ref, og_ref, sg_ref, p_ref, xs_ref, ogs_ref, sgs_ref, ps_ref,
                 wout_ref, gff_ref, wup_ref, wdown_ref, gple_ref, wple_ref, wgate_ref, gfin_ref,
                 y_ref, ys_ref):
    def rows_through_tail(x_ref, og_ref, sg_ref, p_ref, y_ref):
        x1 = x_ref[...] + _dot(og_ref[...], wout_ref[0:D_GDN, :]) + _dot(sg_ref[...], wout_ref[D_GDN:, :])
        h2 = _rms(x1, gff_ref[...]).astype(BF16)
        y_ref[...] = x1
        ff_block = D_FF // 4
        for c in range(D_FF // ff_block):
            lo = c * ff_block
            hid = jnp.maximum(_dot(h2, wup_ref[:, lo:lo + ff_block]), 0.0)
            y_ref[...] += _dot((hid * hid).astype(BF16), wdown_ref[lo:lo + ff_block, :])
        acc = y_ref[...]
        h3 = _rms(acc, gple_ref[...]).astype(BF16)
        gate = _sigmoid(_dot(h3, wgate_ref[...]))
        x3 = acc + _dot(p_ref[...].astype(BF16), wple_ref[...]) * gate
        y_ref[...] = _rms(x3, gfin_ref[...])

    last = pl.num_programs(0) - 1

    @pl.when(pl.program_id(0) < last)
    def _():
        rows_through_tail(x_ref, og_ref, sg_ref, p_ref, y_ref)

    @pl.when(pl.program_id(0) == last)
    def _():
        rows_through_tail(xs_ref, ogs_ref, sgs_ref, ps_ref, ys_ref)


def _tail_call(x, og, sg, p, xs, ogs, sgs, ps, wout, gff, wup, wdown, gple, wple, wgate, gfin):
    n = x.shape[0]
    rows = TAIL_ROWS
    steps = n // rows
    tile = lambda w: pl.BlockSpec((rows, w), lambda i: (jnp.minimum(i, steps - 1), 0))
    sample = [xs, ogs, sgs, ps]
    weights = [wout, gff, wup, wdown, gple, wple, wgate, gfin]
    return pl.pallas_call(
        _tail_kernel,
        grid=(steps + 1,),
        in_specs=[tile(D_MODEL), tile(D_GDN), tile(D_SGU), tile(D_PLE)]
        + [_resident(a.shape) for a in sample + weights],
        out_specs=[tile(D_MODEL), pl.BlockSpec(xs.shape, lambda i: (0, 0))],
        out_shape=[jax.ShapeDtypeStruct((n, D_MODEL), F32), jax.ShapeDtypeStruct(xs.shape, F32)],
        compiler_params=pltpu.CompilerParams(
            dimension_semantics=("arbitrary",), vmem_limit_bytes=VMEM_LIMIT),
        name="tail",
    )(x, og, sg, p, *sample, *weights)


def _sample_proj_kernel(x_ref, hist_ref, gmix_ref, w_ref, wconv_ref,
                        alog_row_ref, dtb_row_ref, lng_ref, lnb_ref, ws_ref, bs_ref,
                        nconv_ref, vg_ref, sg_ref, zs_ref, v_ref, col_ref, q_ref, kt_ref):
    n = x_ref.shape[0]
    h = _rms(x_ref[...], gmix_ref[...]).astype(BF16)
    pre = _dot(h, w_ref[:, COL_Q:COL_Q + D_QKV])
    y = pre * wconv_ref[CONV_W - 1:CONV_W, :]
    for j in range(CONV_W - 1):
        y = y + hist_ref[j] * wconv_ref[j:j + 1, :]
        if j > 0:
            nconv_ref[j - 1] = hist_ref[j]
    nconv_ref[CONV_W - 2] = pre
    y = _silu(y)
    v_ref[...] = y[:, 2 * D_GDN:]
    blocks = n // SAMPLE_BLOCK
    for hh in range(HEADS):
        lo = hh * DH
        q_ref[:, lo:lo + DH] = _l2n(y[:, lo:lo + DH]) * (DH ** -0.5)
        kt = jnp.transpose(_l2n(y[:, D_GDN + lo:D_GDN + lo + DH]))
        for i in range(blocks):
            kt_ref[i, hh] = kt[:, i * SAMPLE_BLOCK:(i + 1) * SAMPLE_BLOCK]
    zs_ref[...] = _silu(_dot(h, w_ref[:, COL_Z:COL_Z + D_GDN]))
    ab = _dot(h, w_ref[:, COL_AB:COL_AB + LANES])
    lane = lax.broadcasted_iota(jnp.int32, ab.shape, 1)
    g_col = -jnp.exp(alog_row_ref[...]) * _softplus(ab + dtb_row_ref[...])
    col_ref[...] = jnp.where(lane < HEADS, _sigmoid(ab), g_col)
    uv = _gelu_tanh(_dot(h, w_ref[:, COL_U:COL_U + 2 * D_SGU]))
    vg = _layer_norm(uv[:, D_SGU:], lng_ref[...], lnb_ref[...])
    vg_ref[...] = vg
    for hh in range(HEADS):
        lo = hh * DH
        mix = ws_ref[hh, 0:1, 0:1] * vg[:, lo:lo + DH] + bs_ref[hh:hh + 1, 0:1]
        sg_ref[:, lo:lo + DH] = (uv[:, lo:lo + DH] * mix).astype(BF16)


def _sample_proj_call(x, hist, gmix, w_all, wconv, alog_row, dtb_row, lng, lnb, ws, bs):
    n = x.shape[0]
    blocks = n // SAMPLE_BLOCK
    out_shape = [
        jax.ShapeDtypeStruct((CONV_W - 1, n, D_QKV), F32),
        jax.ShapeDtypeStruct((n, D_SGU), F32),
        jax.ShapeDtypeStruct((n, D_SGU), BF16),
        jax.ShapeDtypeStruct((n, D_GDN), F32),
        jax.ShapeDtypeStruct((n, D_GDN), F32),
        jax.ShapeDtypeStruct((n, LANES), F32),
        jax.ShapeDtypeStruct((n, D_GDN), F32),
        jax.ShapeDtypeStruct((blocks, HEADS, DH, SAMPLE_BLOCK), F32),
    ]
    return pl.pallas_call(
        _sample_proj_kernel,
        out_shape=out_shape,
        compiler_params=pltpu.CompilerParams(vmem_limit_bytes=VMEM_LIMIT),
        name="sample_proj",
    )(x, hist, gmix, w_all, wconv, alog_row, dtb_row, lng, lnb, ws, bs)


def _sample_gdn_kernel(s_ref, q_ref, kt_ref, v_ref, col_ref, zs_ref, gnorm_ref, og_ref, snew_ref):
    gnorm = gnorm_ref[...]
    q_rows = [q_ref[:, hh * DH:(hh + 1) * DH].astype(BF16) for hh in range(HEADS)]
    for b0 in range(0, SAMPLE_BLOCK, SAMPLE_GROUP):
        pairs = [(bb, hh) for bb in range(b0, b0 + SAMPLE_GROUP) for hh in range(HEADS)]
        kcol = [kt_ref[hh, :, bb:bb + 1] for bb, hh in pairs]
        beta = [col_ref[bb:bb + 1, hh:hh + 1] for bb, hh in pairs]
        eg = [jnp.exp(col_ref[bb:bb + 1, HEADS + hh:HEADS + hh + 1]) for bb, hh in pairs]
        sd = [s_ref[bb, hh] * e for (bb, hh), e in zip(pairs, eg)]
        sk = [jnp.sum(s * kc, axis=0, keepdims=True) for s, kc in zip(sd, kcol)]
        v_new = [b * (v_ref[bb:bb + 1, hh * DH:(hh + 1) * DH] - x) for (bb, hh), b, x in zip(pairs, beta, sk)]
        s1 = [s + kc * vn for s, kc, vn in zip(sd, kcol, v_new)]
        for (bb, hh), s in zip(pairs, s1):
            snew_ref[bb, hh] = s
        o = [_dot(q_rows[hh], s.astype(BF16))[bb:bb + 1, :] for (bb, hh), s in zip(pairs, s1)]
        for (bb, hh), x in zip(pairs, o):
            lo = hh * DH
            og = _rms(x, gnorm) * zs_ref[bb:bb + 1, lo:lo + DH]
            og_ref[bb:bb + 1, lo:lo + DH] = og.astype(BF16)


def _sample_gdn_call(state, q, kt, v, col, zs, gnorm):
    n = state.shape[0]
    blk = SAMPLE_BLOCK
    rowtile = lambda w: pl.BlockSpec((blk, w), lambda i: (i, 0))
    return pl.pallas_call(
        _sample_gdn_kernel,
        grid=(n // blk,),
        in_specs=[
            pl.BlockSpec((blk, HEADS, DH, DH), lambda i: (i, 0, 0, 0)),
            rowtile(D_GDN),
            pl.BlockSpec((None, HEADS, DH, blk), lambda i: (i, 0, 0, 0)),
            rowtile(D_GDN), rowtile(LANES), rowtile(D_GDN),
            _resident(gnorm.shape),
        ],
        out_specs=[rowtile(D_GDN), pl.BlockSpec((blk, HEADS, DH, DH), lambda i: (i, 0, 0, 0))],
        out_shape=[jax.ShapeDtypeStruct((n, D_GDN), BF16),
                   jax.ShapeDtypeStruct((n, HEADS, DH, DH), F32)],
        compiler_params=pltpu.CompilerParams(
            dimension_semantics=("arbitrary",), vmem_limit_bytes=VMEM_LIMIT),
        name="sample_gdn",
    )(state, q, kt, v, col, zs, gnorm)


def kernel(x_prompt, x_sample, state_conv, state_gdn, p_prompt, p_sample, g_mix, w_in, w_conv, a_log, dt_bias, gdn_norm, sgu_ln_g, sgu_ln_b, w_s, b_s, w_out, g_ff, w_up, w_down, g_ple, w_ple, w_ple_gate, g_final):
    depth = w_in.shape[0]
    assert depth == 1, "single-layer trunk only"
    b, l, d = x_prompt.shape
    n = x_sample.shape[0]
    assert x_sample.shape[1] == 1 and d == D_MODEL
    assert l % PROJ_ROWS == 0 and l % GDN_ROWS == 0 and (b * l) % TAIL_ROWS == 0 and n % SAMPLE_BLOCK == 0

    w_all, wout, wup, wdown, wple, wgate, wabt = _prep_call(w_in, w_out, w_up, w_down, w_ple, w_ple_gate)
    row2 = lambda a: a.reshape(1, -1)
    pad_lanes = lambda a: jnp.pad(a, (HEADS, LANES - 2 * HEADS)).reshape(1, LANES)
    pad_rows = lambda a: jnp.pad(a, (HEADS, 0)).reshape(2 * HEADS, 1)
    alog_row, dtb_row = pad_lanes(a_log[0]), pad_lanes(dt_bias[0])
    alog_col, dtb_col = pad_rows(a_log[0]), pad_rows(dt_bias[0])
    gmix, gff, gple, gfin = row2(g_mix[0]), row2(g_ff[0]), row2(g_ple[0]), row2(g_final)
    lng, lnb, gnorm = row2(sgu_ln_g[0]), row2(sgu_ln_b[0]), row2(gdn_norm[0])
    tail_w = (wout, gff, wup, wdown, gple, wple, wgate, gfin)

    qs, k, v, zs, sg, col, row, conv_p = _proj_call(
        x_prompt, gmix, w_all, wabt, w_conv[0], alog_row, dtb_row, alog_col, dtb_col,
        lng, lnb, w_s[0], b_s[0].T)
    og, gdn_p = _gdn_call(qs, k, v, col, row, zs, gnorm)

    nconv, vg_s, sg_s, zs_s, v_s, col_s, q_s, kt = _sample_proj_call(
        x_sample.reshape(n, d), jnp.transpose(state_conv[0], (1, 0, 2)), gmix, w_all,
        w_conv[0], alog_row, dtb_row, lng, lnb, w_s[0], b_s[0])
    og_s, gdn_s = _sample_gdn_call(state_gdn[0], q_s, kt, v_s, col_s, zs_s, gnorm)

    y_prompt, y_sample = _tail_call(
        x_prompt.reshape(b * l, d), og.reshape(b * l, D_GDN), sg.reshape(b * l, D_SGU),
        p_prompt[0].reshape(b * l, D_PLE),
        x_sample.reshape(n, d), og_s, sg_s, p_sample[0].reshape(n, D_PLE), *tail_w)
    y_prompt, y_sample = y_prompt.reshape(b, l, d), y_sample.reshape(n, 1, d)

    return (y_prompt, y_sample,
            conv_p[None], gdn_p[None],
            jnp.transpose(nconv, (1, 0, 2))[None], gdn_s[None],
            vg_s.reshape(n, 1, D_SGU)[None])
```

```python
import functools

import jax
import jax.numpy as jnp
from jax import lax
from jax.experimental import pallas as pl
from jax.experimental.pallas import tpu as pltpu

F32 = jnp.float32
BF16 = jnp.bfloat16

D_MODEL = 1024
D_GDN = 512
D_SGU = 512
HEADS = 4
DH = 128
D_QKV = 3 * D_GDN
D_FF = 4 * D_MODEL
D_PLE = 256
CONV_W = 4
GDN_CHUNK = 64
SGU_CHUNK = 128
EPS = 1e-6
LANES = 128
SUBLANES = 8
VMEM_LIMIT = 56 * 1024 * 1024

COL_Q, COL_K, COL_V, COL_Z, COL_U, COL_VV, COL_AB = 0, 512, 1024, 1536, 2048, 2560, 3072
W_COLS = COL_AB + LANES

PROJ_ROWS = 1024
CONV_BLOCK = 8
GDN_ROWS = 1024
GDN_GROUP = 4
TAIL_ROWS = 512
SAMPLE_BLOCK = 8
SAMPLE_GROUP = 4
PREP_STEPS = 8


def _dot(a, b):
    return jnp.dot(a, b, preferred_element_type=F32)


def _dot_nt(a, b):
    return lax.dot_general(a, b, (((1,), (1,)), ((), ())), preferred_element_type=F32)


def _dot_tn(a, b):
    return lax.dot_general(a, b, (((0,), (0,)), ((), ())), preferred_element_type=F32)


def _split3(x):
    hi = x.astype(BF16)
    r1 = x - hi.astype(F32)
    mid = r1.astype(BF16)
    lo = (r1 - mid.astype(F32)).astype(BF16)
    return hi, mid, lo


def _rms(x, gain):
    return x * lax.rsqrt(jnp.mean(x * x, axis=-1, keepdims=True) + EPS) * gain


def _sigmoid(x):
    return 1.0 / (1.0 + jnp.exp(-x))


def _silu(x):
    hx = 0.5 * x
    return hx + hx * jnp.tanh(hx)


def _softplus(x):
    return jnp.maximum(x, 0.0) + jnp.log(1.0 + jnp.exp(-jnp.abs(x)))


def _gelu_tanh(x):
    c = 0.7978845608028654
    hx = 0.5 * x
    return hx + hx * jnp.tanh(x * (c + (c * 0.044715) * (x * x)))


def _l2n(x):
    return x * lax.rsqrt(jnp.sum(x * x, axis=-1, keepdims=True) + 1e-6)


def _layer_norm(x, gain, bias):
    mu = jnp.mean(x, axis=-1, keepdims=True)
    xc = x - mu
    var = jnp.mean(xc * xc, axis=-1, keepdims=True)
    return xc * lax.rsqrt(var + EPS) * gain + bias


def _resident(shape):
    nd = len(shape)
    return pl.BlockSpec(shape, lambda *_: (0,) * nd, pipeline_mode=pl.Buffered(1))


def _prep_kernel(win_ref, wall_o, wabt_o):
    o1 = D_QKV + D_GDN
    o3 = o1 + 2 * HEADS

    def put(dst, src):
        wall_o[:, dst:dst + LANES] = jnp.transpose(win_ref[src:src + LANES, :]).astype(BF16)

    for j in range(o1 // LANES):
        put(j * LANES, j * LANES)
    for j in range(2 * D_SGU // LANES):
        put(COL_U + j * LANES, o3 + j * LANES)
    ab = win_ref[o1:o3, :]
    ab16 = jnp.concatenate([ab, jnp.zeros((2 * SUBLANES - 2 * HEADS, LANES), F32)], axis=0)
    wabt_o[...] = ab16.astype(BF16)
    ab128 = jnp.concatenate([ab, jnp.zeros((LANES - 2 * HEADS, LANES), F32)], axis=0)
    wall_o[:, COL_AB:] = jnp.transpose(ab128).astype(BF16)


def _prep_call(w_in):
    steps = PREP_STEPS
    d, d_in = w_in.shape[1], w_in.shape[2]
    assert d // steps == LANES
    win_t = jnp.transpose(w_in[0])
    return pl.pallas_call(
        _prep_kernel,
        grid=(steps,),
        in_specs=[pl.BlockSpec((d_in, LANES), lambda i: (0, i))],
        out_specs=[pl.BlockSpec((LANES, W_COLS), lambda i: (i, 0)),
                   pl.BlockSpec((2 * SUBLANES, LANES), lambda i: (0, i))],
        out_shape=[jax.ShapeDtypeStruct((d, W_COLS), BF16), jax.ShapeDtypeStruct((2 * SUBLANES, d), BF16)],
        compiler_params=pltpu.CompilerParams(
            dimension_semantics=("arbitrary",), vmem_limit_bytes=VMEM_LIMIT),
        name="prep",
    )(win_t)


def _proj_kernel(x_ref, gmix_ref, w_ref, wabt_ref, wconv_ref,
                 alog_row_ref, dtb_row_ref, alog_col_ref, dtb_col_ref, lng_ref, lnb_ref,
                 ws_ref, bst_ref, wout_ref, wup_ref, wdown_ref, wple_ref, wgate_ref,
                 qs_ref, k_ref, v_ref, zs_ref, sg_ref, col_ref, row_ref, cstate_ref,
                 wout_o, wup_o, wdown_o, wple_o, wgate_o,
                 cbuf):
    rows = x_ref.shape[0]
    first_tile = pl.program_id(1) == 0
    wout_o[...] = wout_ref[...].astype(BF16)
    wup_o[...] = wup_ref[...].astype(BF16)
    wdown_o[...] = wdown_ref[...].astype(BF16)
    wple_o[...] = wple_ref[...].astype(BF16)
    wgate_o[...] = wgate_ref[...].astype(BF16)
    h = _rms(x_ref[...], gmix_ref[...]).astype(BF16)

    def project(lo, width=D_GDN):
        return _dot(h, w_ref[:, lo:lo + width])

    def conv_silu(pre, lo):
        cols = slice(lo, lo + D_GDN)
        groups = rows // SUBLANES
        tap = lambda j: wconv_ref[j:j + 1, cols].reshape(1, 1, D_GDN)
        shape3 = (1, SUBLANES, D_GDN)
        last_p = jnp.where(first_tile, 0.0, cbuf[0:SUBLANES, cols]).reshape(shape3)
        last_b = jnp.where(first_tile, 0.0, cbuf[SUBLANES:, cols]).reshape(shape3)
        cstate_ref[:, cols] = pre[rows - (CONV_W - 1):rows, :]
        cur = pre.reshape(groups, SUBLANES, D_GDN)
        sub = lax.broadcasted_iota(jnp.int32, (CONV_BLOCK, SUBLANES, D_GDN), 1)

        def delay(a, a_last, s):
            rolled = pltpu.roll(a, s, axis=1)
            rolled_prev = jnp.concatenate([pltpu.roll(a_last, s, axis=1), rolled[:-1]], axis=0)
            return jnp.where(sub >= s, rolled, rolled_prev)

        assert CONV_W == 4
        out = []
        for g0 in range(0, groups, CONV_BLOCK):
            p = cur[g0:g0 + CONV_BLOCK]
            dp = delay(p, last_p, 1)
            b = p * tap(1) + dp * tap(0)
            y = p * tap(3) + dp * tap(2) + delay(b, last_b, 2)
            last_p, last_b = p[CONV_BLOCK - 1:], b[CONV_BLOCK - 1:]
            out.append(_silu(y))
        cbuf[0:SUBLANES, cols] = last_p.reshape(SUBLANES, D_GDN)
        cbuf[SUBLANES:, cols] = last_b.reshape(SUBLANES, D_GDN)
        return jnp.concatenate(out, axis=0).reshape(rows, D_GDN)

    pre_vv = project(COL_VV)
    pre_u = project(COL_U)
    vg = _layer_norm(_gelu_tanh(pre_vv), lng_ref[...], lnb_ref[...]).astype(BF16)
    ab = project(COL_AB, LANES)
    pre_q = project(COL_Q)
    u_act = _gelu_tanh(pre_u)
    ti = lax.broadcasted_iota(jnp.int32, (SGU_CHUNK, SGU_CHUNK), 0)
    si = lax.broadcasted_iota(jnp.int32, (SGU_CHUNK, SGU_CHUNK), 1)
    for hh in range(HEADS):
        lo = hh * DH
        w_tril = jnp.where(ti >= si, ws_ref[hh], 0.0).astype(BF16)
        bias = bst_ref[:, hh:hh + 1]
        for c in range(rows // SGU_CHUNK):
            r0 = c * SGU_CHUNK
            mix = _dot(w_tril, vg[r0:r0 + SGU_CHUNK, lo:lo + DH]) + bias
            sg_ref[r0:r0 + SGU_CHUNK, lo:lo + DH] = (u_act[r0:r0 + SGU_CHUNK, lo:lo + DH] * mix).astype(BF16)
    pre_k = project(COL_K)

    lane = lax.broadcasted_iota(jnp.int32, (GDN_CHUNK, LANES), 1)
    g_col = -jnp.exp(alog_row_ref[...]) * _softplus(ab + dtb_row_ref[...])
    abt = _dot_nt(wabt_ref[...], h)[0:2 * HEADS, :]
    sub = lax.broadcasted_iota(jnp.int32, (2 * HEADS, GDN_CHUNK), 0)
    g_row = -jnp.exp(alog_col_ref[...]) * _softplus(abt + dtb_col_ref[...])
    ri = lax.broadcasted_iota(jnp.int32, (GDN_CHUNK, GDN_CHUNK), 0)
    ci = lax.broadcasted_iota(jnp.int32, (GDN_CHUNK, GDN_CHUNK), 1)
    tri_l = (ri >= ci).astype(BF16)
    tri_u = (ri <= ci).astype(BF16)
    beta_col = _sigmoid(ab)
    beta_row = _sigmoid(abt)
    for c in range(rows // GDN_CHUNK):
        r0 = c * GDN_CHUNK
        p0, p1, p2 = _split3(g_col[r0:r0 + GDN_CHUNK, :])
        gc = _dot(tri_l, p0) + _dot(tri_l, p1) + _dot(tri_l, p2)
        col_ref[r0:r0 + GDN_CHUNK, :] = jnp.where(lane < HEADS, beta_col[r0:r0 + GDN_CHUNK, :], gc)
        p0, p1, p2 = _split3(g_row[:, r0:r0 + GDN_CHUNK])
        gr = _dot(p0, tri_u) + _dot(p1, tri_u) + _dot(p2, tri_u)
        row_ref[c] = jnp.where(sub < HEADS, beta_row[:, r0:r0 + GDN_CHUNK], gr)

    y = conv_silu(pre_q, COL_Q)
    for hh in range(HEADS):
        lo = hh * DH
        qs_ref[:, lo:lo + DH] = (_l2n(y[:, lo:lo + DH]) * (DH ** -0.5)).astype(BF16)
    pre_v = project(COL_V)
    y = conv_silu(pre_k, COL_K)
    for hh in range(HEADS):
        lo = hh * DH
        k_ref[:, lo:lo + DH] = _l2n(y[:, lo:lo + DH]).astype(BF16)
    pre_z = project(COL_Z)
    v_ref[...] = conv_silu(pre_v, COL_V).astype(BF16)
    zs_ref[...] = _silu(pre_z).astype(BF16)


def _proj_call(x, gmix, w_all, wabt, wconv, alog_row, dtb_row, alog_col, dtb_col, lng, lnb, ws, bst, tail_f32):
    b, l, _ = x.shape
    rows = PROJ_ROWS
    nt = l // rows
    steps = b * nt
    nchunk = rows // GDN_CHUNK
    tile = lambda w: pl.BlockSpec((None, rows, w), lambda i, j: (i, j, 0))
    ins = [x, gmix, w_all, wabt, wconv, alog_row, dtb_row, alog_col, dtb_col, lng, lnb, ws, bst]
    in_specs = [tile(D_MODEL)] + [_resident(a.shape) for a in ins[1:]]
    in_specs += [pl.BlockSpec((None, w.shape[1] // steps, w.shape[2]), lambda i, j: (0, i * nt + j, 0))
                 for w in tail_f32]
    out_shape = [jax.ShapeDtypeStruct((b, l, D_GDN), BF16)] * 5 + [
        jax.ShapeDtypeStruct((b, l, LANES), F32),
        jax.ShapeDtypeStruct((b, l // GDN_CHUNK, 2 * HEADS, GDN_CHUNK), F32),
        jax.ShapeDtypeStruct((b, CONV_W - 1, D_QKV), F32),
    ] + [jax.ShapeDtypeStruct(w.shape[1:], BF16) for w in tail_f32]
    out_specs = [tile(D_GDN)] * 5 + [
        tile(LANES),
        pl.BlockSpec((None, nchunk, 2 * HEADS, GDN_CHUNK), lambda i, j: (i, j, 0, 0)),
        pl.BlockSpec((None, CONV_W - 1, D_QKV), lambda i, j: (i, 0, 0)),
    ] + [pl.BlockSpec((w.shape[1] // steps, w.shape[2]), lambda i, j: (i * nt + j, 0)) for w in tail_f32]
    ins = ins + list(tail_f32)
    return pl.pallas_call(
        _proj_kernel,
        grid=(b, nt),
        in_specs=in_specs,
        out_specs=out_specs,
        out_shape=out_shape,
        scratch_shapes=[pltpu.VMEM((2 * SUBLANES, D_QKV), F32)],
        compiler_params=pltpu.CompilerParams(
            dimension_semantics=("arbitrary", "arbitrary"), vmem_limit_bytes=VMEM_LIMIT),
        name="proj",
    )(*ins)


def _neumann_many(a_list):
    c = a_list[0].shape[0]
    p = [-a for a in a_list]
    pb = [x.astype(BF16) for x in p]
    q = [_dot(x, x) for x in pb]
    yield
    power = 2
    while 2 * power < c:
        qb = [x.astype(BF16) for x in q]
        r = [_dot(jnp.concatenate([pi.astype(BF16), qi], axis=0), qi) for pi, qi in zip(p, qb)]
        yield
        p = [pi + qi + ri[:c] for pi, qi, ri in zip(p, q, r)]
        q = [ri[c:] for ri in r]
        power *= 2
    tail = [_dot(pi.astype(BF16), qi.astype(BF16)) for pi, qi in zip(p, q)]
    yield
    return [pi + qi + ti for pi, qi, ti in zip(p, q, tail)]


def _interleave(steps, fillers):
    fillers = list(fillers)
    count = 0
    while True:
        try:
            next(steps)
        except StopIteration as stop:
            for f in fillers:
                f()
            return stop.value
        count += 1
        if fillers and count % 2 == 0:
            fillers.pop(0)()


def _gdn_kernel(qs_ref, k_ref, v_ref, col_ref, row_ref, zs_ref, gnorm_ref,
                og_ref, state_ref, s_scr):
    rows = qs_ref.shape[0]
    c_ = GDN_CHUNK
    t = pl.program_id(1)

    @pl.when(t == 0)
    def _():
        s_scr[...] = jnp.zeros(s_scr.shape, F32)

    ri = lax.broadcasted_iota(jnp.int32, (c_, c_), 0)
    ci = lax.broadcasted_iota(jnp.int32, (c_, c_), 1)
    incl = ri >= ci
    strict = ri > ci
    gnorm = gnorm_ref[...]

    def prepare(chunks):
        pairs = [(c, hh) for c in chunks for hh in range(HEADS)]
        cols = {c: col_ref[c * c_:(c + 1) * c_, :] for c in chunks}
        rws = {c: row_ref[c] for c in chunks}
        beta = [cols[c][:, hh:hh + 1] for c, hh in pairs]
        gc = [cols[c][:, HEADS + hh:HEADS + hh + 1] for c, hh in pairs]
        gr = [rws[c][HEADS + hh:HEADS + hh + 1, :] for c, hh in pairs]
        glast = [cols[c][c_ - 1:c_, HEADS + hh:HEADS + hh + 1] for c, hh in pairs]
        kh = [k_ref[c * c_:(c + 1) * c_, hh * DH:(hh + 1) * DH] for c, hh in pairs]
        qh = [qs_ref[c * c_:(c + 1) * c_, hh * DH:(hh + 1) * DH] for c, hh in pairs]
        kf = [x.astype(F32) for x in kh]
        kb = [x * b for x, b in zip(kf, beta)]
        m1 = [_dot_nt(jnp.concatenate([x.astype(BF16), q], axis=0), k) for x, q, k in zip(kb, qh, kh)]
        yield
        decay = [jnp.exp(jnp.where(incl, a - b, -jnp.inf)) for a, b in zip(gc, gr)]
        attn = [(m[c_:] * d).astype(BF16) for m, d in zip(m1, decay)]
        p = yield from _neumann_many([jnp.where(strict, m[:c_] * d, 0.0) for m, d in zip(m1, decay)])
        eg = [jnp.exp(x) for x in gc]
        rhs = [jnp.concatenate([x * e, v_ref[c * c_:(c + 1) * c_, hh * DH:(hh + 1) * DH].astype(F32) * b], axis=1)
               for x, e, b, (c, hh) in zip(kb, eg, beta, pairs)]
        wu = [(x + _dot(pi.astype(BF16), x.astype(BF16))).astype(BF16) for x, pi in zip(rhs, p)]
        yield
        ke_t = [jnp.transpose((x * jnp.exp(gl - g)).astype(BF16)) for x, gl, g in zip(kf, glast, gc)]
        r1 = [_dot(jnp.concatenate([kt, a], axis=0), x) for kt, a, x in zip(ke_t, attn, wu)]
        yield
        staged = {}
        for i, pair in enumerate(pairs):
            q_eff = qh[i].astype(F32) * eg[i] - r1[i][DH:, :DH]
            mq = jnp.concatenate([(-r1[i][:DH, :DH]).astype(BF16), q_eff.astype(BF16)], axis=0)
            staged[pair] = (mq, r1[i][:DH, DH:], r1[i][DH:, DH:], jnp.exp(glast[i]))
        return staged

    state = [s_scr[hh] for hh in range(HEADS)]

    def advance(c, staged):
        r0 = c * c_
        for hh in range(HEADS):
            lo = hh * DH
            mq, n_mat, o_zero, e_last = staged[c, hh]
            r = _dot(mq, state[hh].astype(BF16))
            state[hh] = state[hh] * e_last + r[:DH] + n_mat
            og = _rms(r[DH:] + o_zero, gnorm) * zs_ref[r0:r0 + c_, lo:lo + DH].astype(F32)
            og_ref[r0:r0 + c_, lo:lo + DH] = og.astype(BF16)

    nchunk = rows // c_
    fillers = []
    for g0 in range(0, nchunk, GDN_GROUP):
        group = list(range(g0, g0 + GDN_GROUP))
        staged = _interleave(prepare(group), fillers)
        fillers = [functools.partial(advance, c, staged) for c in group]
    for f in fillers:
        f()
    for hh in range(HEADS):
        s_scr[hh] = state[hh]

    @pl.when(t == pl.num_programs(1) - 1)
    def _():
        state_ref[...] = s_scr[...]


def _gdn_call(qs, k, v, col, row, zs, gnorm):
    b, l, _ = qs.shape
    rows = GDN_ROWS
    nt = l // rows
    nchunk = rows // GDN_CHUNK
    tile = lambda w: pl.BlockSpec((None, rows, w), lambda i, j: (i, j, 0))
    return pl.pallas_call(
        _gdn_kernel,
        grid=(b, nt),
        in_specs=[tile(D_GDN)] * 3 + [
            tile(LANES),
            pl.BlockSpec((None, nchunk, 2 * HEADS, GDN_CHUNK), lambda i, j: (i, j, 0, 0)),
            tile(D_GDN),
            _resident(gnorm.shape),
        ],
        out_specs=[tile(D_GDN), pl.BlockSpec((None, HEADS, DH, DH), lambda i, j: (i, 0, 0, 0))],
        out_shape=[jax.ShapeDtypeStruct((b, l, D_GDN), BF16),
                   jax.ShapeDtypeStruct((b, HEADS, DH, DH), F32)],
        scratch_shapes=[pltpu.VMEM((HEADS, DH, DH), F32)],
        compiler_params=pltpu.CompilerParams(
            dimension_semantics=("arbitrary", "arbitrary"), vmem_limit_bytes=VMEM_LIMIT),
        name="gdn",
    )(qs, k, v, col, row, zs, gnorm)


def _tail_kernel(x_ref, og_ref, sg_ref, p_ref, xs_ref, ogs_ref, sgs_ref, ps_ref,
                 wout_ref, gff_ref, wup_ref, wdown_ref, gple_ref, wple_ref, wgate_ref, gfin_ref,
                 y_ref, ys_ref):
    def rows_through_tail(x_ref, og_ref, sg_ref, p_ref, y_ref):
        x1 = x_ref[...] + _dot(og_ref[...], wout_ref[0:D_GDN, :]) + _dot(sg_ref[...], wout_ref[D_GDN:, :])
        h2 = _rms(x1, gff_ref[...]).astype(BF16)
        y_ref[...] = x1
        ff_block = D_FF // 4
        for c in range(D_FF // ff_block):
            lo = c * ff_block
            hid = jnp.maximum(_dot(h2, wup_ref[:, lo:lo + ff_block]), 0.0)
            y_ref[...] += _dot((hid * hid).astype(BF16), wdown_ref[lo:lo + ff_block, :])
        acc = y_ref[...]
        h3 = _rms(acc, gple_ref[...]).astype(BF16)
        gate = _sigmoid(_dot(h3, wgate_ref[...]))
        x3 = acc + _dot(p_ref[...].astype(BF16), wple_ref[...]) * gate
        y_ref[...] = _rms(x3, gfin_ref[...])

    last = pl.num_programs(0) - 1

    @pl.when(pl.program_id(0) < last)
    def _():
        rows_through_tail(x_ref, og_ref, sg_ref, p_ref, y_ref)

    @pl.when(pl.program_id(0) == last)
    def _():
        rows_through_tail(xs_ref, ogs_ref, sgs_ref, ps_ref, ys_ref)


def _tail_call(x, og, sg, p, xs, ogs, sgs, ps, wout, gff, wup, wdown, gple, wple, wgate, gfin):
    n = x.shape[0]
    rows = TAIL_ROWS
    steps = n // rows
    tile = lambda w: pl.BlockSpec((rows, w), lambda i: (jnp.minimum(i, steps - 1), 0))
    sample = [xs, ogs, sgs, ps]
    weights = [wout, gff, wup, wdown, gple, wple, wgate, gfin]
    return pl.pallas_call(
        _tail_kernel,
        grid=(steps + 1,),
        in_specs=[tile(D_MODEL), tile(D_GDN), tile(D_SGU), tile(D_PLE)]
        + [_resident(a.shape) for a in sample + weights],
        out_specs=[tile(D_MODEL), pl.BlockSpec(xs.shape, lambda i: (0, 0))],
        out_shape=[jax.ShapeDtypeStruct((n, D_MODEL), F32), jax.ShapeDtypeStruct(xs.shape, F32)],
        compiler_params=pltpu.CompilerParams(
            dimension_semantics=("arbitrary",), vmem_limit_bytes=VMEM_LIMIT),
        name="tail",
    )(x, og, sg, p, *sample, *weights)


def _sample_proj_kernel(x_ref, hist_ref, gmix_ref, w_ref, wconv_ref,
                        alog_row_ref, dtb_row_ref, lng_ref, lnb_ref, ws_ref, bs_ref,
                        nconv_ref, vg_ref, sg_ref, zs_ref, v_ref, col_ref, q_ref, kt_ref):
    n = x_ref.shape[0]
    h = _rms(x_ref[...], gmix_ref[...]).astype(BF16)
    pre = _dot(h, w_ref[:, COL_Q:COL_Q + D_QKV])
    y = pre * wconv_ref[CONV_W - 1:CONV_W, :]
    for j in range(CONV_W - 1):
        y = y + hist_ref[j] * wconv_ref[j:j + 1, :]
        if j > 0:
            nconv_ref[j - 1] = hist_ref[j]
    nconv_ref[CONV_W - 2] = pre
    y = _silu(y)
    v_ref[...] = y[:, 2 * D_GDN:]
    blocks = n // SAMPLE_BLOCK
    for hh in range(HEADS):
        lo = hh * DH
        q_ref[:, lo:lo + DH] = _l2n(y[:, lo:lo + DH]) * (DH ** -0.5)
        kt = jnp.transpose(_l2n(y[:, D_GDN + lo:D_GDN + lo + DH]))
        for i in range(blocks):
            kt_ref[i, hh] = kt[:, i * SAMPLE_BLOCK:(i + 1) * SAMPLE_BLOCK]
    zs_ref[...] = _silu(_dot(h, w_ref[:, COL_Z:COL_Z + D_GDN]))
    ab = _dot(h, w_ref[:, COL_AB:COL_AB + LANES])
    lane = lax.broadcasted_iota(jnp.int32, ab.shape, 1)
    g_col = -jnp.exp(alog_row_ref[...]) * _softplus(ab + dtb_row_ref[...])
    col_ref[...] = jnp.where(lane < HEADS, _sigmoid(ab), g_col)
    uv = _gelu_tanh(_dot(h, w_ref[:, COL_U:COL_U + 2 * D_SGU]))
    vg = _layer_norm(uv[:, D_SGU:], lng_ref[...], lnb_ref[...])
    vg_ref[...] = vg
    for hh in range(HEADS):
        lo = hh * DH
        mix = ws_ref[hh, 0:1, 0:1] * vg[:, lo:lo + DH] + bs_ref[hh:hh + 1, 0:1]
        sg_ref[:, lo:lo + DH] = (uv[:, lo:lo + DH] * mix).astype(BF16)


def _sample_proj_call(x, hist, gmix, w_all, wconv, alog_row, dtb_row, lng, lnb, ws, bs):
    n = x.shape[0]
    blocks = n // SAMPLE_BLOCK
    out_shape = [
        jax.ShapeDtypeStruct((CONV_W - 1, n, D_QKV), F32),
        jax.ShapeDtypeStruct((n, D_SGU), F32),
        jax.ShapeDtypeStruct((n, D_SGU), BF16),
        jax.ShapeDtypeStruct((n, D_GDN), F32),
        jax.ShapeDtypeStruct((n, D_GDN), F32),
        jax.ShapeDtypeStruct((n, LANES), F32),
        jax.ShapeDtypeStruct((n, D_GDN), F32),
        jax.ShapeDtypeStruct((blocks, HEADS, DH, SAMPLE_BLOCK), F32),
    ]
    return pl.pallas_call(
        _sample_proj_kernel,
        out_shape=out_shape,
        compiler_params=pltpu.CompilerParams(vmem_limit_bytes=VMEM_LIMIT),
        name="sample_proj",
    )(x, hist, gmix, w_all, wconv, alog_row, dtb_row, lng, lnb, ws, bs)


def _sample_gdn_kernel(s_ref, q_ref, kt_ref, v_ref, col_ref, zs_ref, gnorm_ref, og_ref, snew_ref):
    gnorm = gnorm_ref[...]
    q_rows = [q_ref[:, hh * DH:(hh + 1) * DH].astype(BF16) for hh in range(HEADS)]
    for b0 in range(0, SAMPLE_BLOCK, SAMPLE_GROUP):
        pairs = [(bb, hh) for bb in range(b0, b0 + SAMPLE_GROUP) for hh in range(HEADS)]
        kcol = [kt_ref[hh, :, bb:bb + 1] for bb, hh in pairs]
        beta = [col_ref[bb:bb + 1, hh:hh + 1] for bb, hh in pairs]
        eg = [jnp.exp(col_ref[bb:bb + 1, HEADS + hh:HEADS + hh + 1]) for bb, hh in pairs]
        sd = [s_ref[bb, hh] * e for (bb, hh), e in zip(pairs, eg)]
        sk = [jnp.sum(s * kc, axis=0, keepdims=True) for s, kc in zip(sd, kcol)]
        v_new = [b * (v_ref[bb:bb + 1, hh * DH:(hh + 1) * DH] - x) for (bb, hh), b, x in zip(pairs, beta, sk)]
        s1 = [s + kc * vn for s, kc, vn in zip(sd, kcol, v_new)]
        for (bb, hh), s in zip(pairs, s1):
            snew_ref[bb, hh] = s
        o = [_dot(q_rows[hh], s.astype(BF16))[bb:bb + 1, :] for (bb, hh), s in zip(pairs, s1)]
        for (bb, hh), x in zip(pairs, o):
            lo = hh * DH
            og = _rms(x, gnorm) * zs_ref[bb:bb + 1, lo:lo + DH]
            og_ref[bb:bb + 1, lo:lo + DH] = og.astype(BF16)


def _sample_gdn_call(state, q, kt, v, col, zs, gnorm):
    n = state.shape[0]
    blk = SAMPLE_BLOCK
    rowtile = lambda w: pl.BlockSpec((blk, w), lambda i: (i, 0))
    return pl.pallas_call(
        _sample_gdn_kernel,
        grid=(n // blk,),
        in_specs=[
            pl.BlockSpec((blk, HEADS, DH, DH), lambda i: (i, 0, 0, 0)),
            rowtile(D_GDN),
            pl.BlockSpec((None, HEADS, DH, blk), lambda i: (i, 0, 0, 0)),
            rowtile(D_GDN), rowtile(LANES), rowtile(D_GDN),
            _resident(gnorm.shape),
        ],
        out_specs=[rowtile(D_GDN), pl.BlockSpec((blk, HEADS, DH, DH), lambda i: (i, 0, 0, 0))],
        out_shape=[jax.ShapeDtypeStruct((n, D_GDN), BF16),
                   jax.ShapeDtypeStruct((n, HEADS, DH, DH), F32)],
        compiler_params=pltpu.CompilerParams(
            dimension_semantics=("arbitrary",), vmem_limit_bytes=VMEM_LIMIT),
        name="sample_gdn",
    )(state, q, kt, v, col, zs, gnorm)


def kernel(x_prompt, x_sample, state_conv, state_gdn, p_prompt, p_sample, g_mix, w_in, w_conv, a_log, dt_bias, gdn_norm, sgu_ln_g, sgu_ln_b, w_s, b_s, w_out, g_ff, w_up, w_down, g_ple, w_ple, w_ple_gate, g_final):
    depth = w_in.shape[0]
    assert depth == 1, "single-layer trunk only"
    b, l, d = x_prompt.shape
    n = x_sample.shape[0]
    assert x_sample.shape[1] == 1 and d == D_MODEL
    assert l % PROJ_ROWS == 0 and l % GDN_ROWS == 0 and (b * l) % TAIL_ROWS == 0 and n % SAMPLE_BLOCK == 0

    w_all, wabt = _prep_call(w_in)
    row2 = lambda a: a.reshape(1, -1)
    pad_lanes = lambda a: jnp.pad(a, (HEADS, LANES - 2 * HEADS)).reshape(1, LANES)
    pad_rows = lambda a: jnp.pad(a, (HEADS, 0)).reshape(2 * HEADS, 1)
    alog_row, dtb_row = pad_lanes(a_log[0]), pad_lanes(dt_bias[0])
    alog_col, dtb_col = pad_rows(a_log[0]), pad_rows(dt_bias[0])
    gmix, gff, gple, gfin = row2(g_mix[0]), row2(g_ff[0]), row2(g_ple[0]), row2(g_final)
    lng, lnb, gnorm = row2(sgu_ln_g[0]), row2(sgu_ln_b[0]), row2(gdn_norm[0])

    qs, k, v, zs, sg, col, row, conv_p, wout, wup, wdown, wple, wgate = _proj_call(
        x_prompt, gmix, w_all, wabt, w_conv[0], alog_row, dtb_row, alog_col, dtb_col,
        lng, lnb, w_s[0], b_s[0].T, (w_out, w_up, w_down, w_ple, w_ple_gate))
    tail_w = (wout, gff, wup, wdown, gple, wple, wgate, gfin)
    og, gdn_p = _gdn_call(qs, k, v, col, row, zs, gnorm)

    nconv, vg_s, sg_s, zs_s, v_s, col_s, q_s, kt = _sample_proj_call(
        x_sample.reshape(n, d), jnp.transpose(state_conv[0], (1, 0, 2)), gmix, w_all,
        w_conv[0], alog_row, dtb_row, lng, lnb, w_s[0], b_s[0])
    og_s, gdn_s = _sample_gdn_call(state_gdn[0], q_s, kt, v_s, col_s, zs_s, gnorm)

    y_prompt, y_sample = _tail_call(
        x_prompt.reshape(b * l, d), og.reshape(b * l, D_GDN), sg.reshape(b * l, D_SGU),
        p_prompt[0].reshape(b * l, D_PLE),
        x_sample.reshape(n, d), og_s, sg_s, p_sample[0].reshape(n, D_PLE), *tail_w)
    y_prompt, y_sample = y_prompt.reshape(b, l, d), y_sample.reshape(n, 1, d)

    return (y_prompt, y_sample,
            conv_p[None], gdn_p[None],
            jnp.transpose(nconv, (1, 0, 2))[None], gdn_s[None],
            vg_s.reshape(n, 1, D_SGU)[None])
```

```python
import functools

import jax
import jax.numpy as jnp
from jax import lax
from jax.experimental import pallas as pl
from jax.experimental.pallas import tpu as pltpu

F32 = jnp.float32
BF16 = jnp.bfloat16

D_MODEL = 1024
D_GDN = 512
D_SGU = 512
HEADS = 4
DH = 128
D_QKV = 3 * D_GDN
D_FF = 4 * D_MODEL
D_PLE = 256
CONV_W = 4
GDN_CHUNK = 64
SGU_CHUNK = 128
EPS = 1e-6
LANES = 128
SUBLANES = 8
VMEM_LIMIT = 56 * 1024 * 1024

COL_Q, COL_K, COL_V, COL_Z, COL_U, COL_VV, COL_AB = 0, 512, 1024, 1536, 2048, 2560, 3072
W_COLS = COL_AB + LANES

PROJ_ROWS = 1024
CONV_BLOCK = 8
GDN_ROWS = 1024
GDN_GROUP = 4
TAIL_ROWS = 512
SAMPLE_BLOCK = 8
SAMPLE_HALF = 4
PREP_STEPS = 8


def _dot(a, b):
    return jnp.dot(a, b, preferred_element_type=F32)


def _dot_nt(a, b):
    return lax.dot_general(a, b, (((1,), (1,)), ((), ())), preferred_element_type=F32)


def _dot_tn(a, b):
    return lax.dot_general(a, b, (((0,), (0,)), ((), ())), preferred_element_type=F32)


def _split3(x):
    hi = x.astype(BF16)
    r1 = x - hi.astype(F32)
    mid = r1.astype(BF16)
    lo = (r1 - mid.astype(F32)).astype(BF16)
    return hi, mid, lo


def _rms(x, gain):
    return x * lax.rsqrt(jnp.mean(x * x, axis=-1, keepdims=True) + EPS) * gain


def _sigmoid(x):
    return 1.0 / (1.0 + jnp.exp(-x))


def _silu(x):
    hx = 0.5 * x
    return hx + hx * jnp.tanh(hx)


def _softplus(x):
    return jnp.maximum(x, 0.0) + jnp.log(1.0 + jnp.exp(-jnp.abs(x)))


def _gelu_tanh(x):
    c = 0.7978845608028654
    hx = 0.5 * x
    return hx + hx * jnp.tanh(x * (c + (c * 0.044715) * (x * x)))


def _l2n(x):
    return x * lax.rsqrt(jnp.sum(x * x, axis=-1, keepdims=True) + 1e-6)


def _layer_norm(x, gain, bias):
    mu = jnp.mean(x, axis=-1, keepdims=True)
    xc = x - mu
    var = jnp.mean(xc * xc, axis=-1, keepdims=True)
    return xc * lax.rsqrt(var + EPS) * gain + bias


def _resident(shape):
    nd = len(shape)
    return pl.BlockSpec(shape, lambda *_: (0,) * nd, pipeline_mode=pl.Buffered(1))


def _prep_kernel(win_ref, wall_o, wabt_o):
    o1 = D_QKV + D_GDN
    o3 = o1 + 2 * HEADS

    def put(dst, src):
        wall_o[:, dst:dst + LANES] = jnp.transpose(win_ref[src:src + LANES, :]).astype(BF16)

    for j in range(o1 // LANES):
        put(j * LANES, j * LANES)
    for j in range(2 * D_SGU // LANES):
        put(COL_U + j * LANES, o3 + j * LANES)
    ab = win_ref[o1:o3, :]
    ab16 = jnp.concatenate([ab, jnp.zeros((2 * SUBLANES - 2 * HEADS, LANES), F32)], axis=0)
    wabt_o[...] = ab16.astype(BF16)
    ab128 = jnp.concatenate([ab, jnp.zeros((LANES - 2 * HEADS, LANES), F32)], axis=0)
    wall_o[:, COL_AB:] = jnp.transpose(ab128).astype(BF16)


def _prep_call(w_in):
    steps = PREP_STEPS
    d, d_in = w_in.shape[1], w_in.shape[2]
    assert d // steps == LANES
    win_t = jnp.transpose(w_in[0])
    return pl.pallas_call(
        _prep_kernel,
        grid=(steps,),
        in_specs=[pl.BlockSpec((d_in, LANES), lambda i: (0, i))],
        out_specs=[pl.BlockSpec((LANES, W_COLS), lambda i: (i, 0)),
                   pl.BlockSpec((2 * SUBLANES, LANES), lambda i: (0, i))],
        out_shape=[jax.ShapeDtypeStruct((d, W_COLS), BF16), jax.ShapeDtypeStruct((2 * SUBLANES, d), BF16)],
        compiler_params=pltpu.CompilerParams(
            dimension_semantics=("arbitrary",), vmem_limit_bytes=VMEM_LIMIT),
        name="prep",
    )(win_t)


def _proj_kernel(x_ref, gmix_ref, w_ref, wabt_ref, wconv_ref,
                 alog_row_ref, dtb_row_ref, alog_col_ref, dtb_col_ref, lng_ref, lnb_ref,
                 ws_ref, bst_ref, wout_ref, wup_ref, wdown_ref, wple_ref, wgate_ref,
                 qs_ref, k_ref, v_ref, zs_ref, sg_ref, col_ref, row_ref, cstate_ref,
                 wout_o, wup_o, wdown_o, wple_o, wgate_o,
                 cbuf):
    rows = x_ref.shape[0]
    first_tile = pl.program_id(1) == 0
    wout_o[...] = wout_ref[...].astype(BF16)
    wup_o[...] = wup_ref[...].astype(BF16)
    wdown_o[...] = wdown_ref[...].astype(BF16)
    wple_o[...] = wple_ref[...].astype(BF16)
    wgate_o[...] = wgate_ref[...].astype(BF16)
    h = _rms(x_ref[...], gmix_ref[...]).astype(BF16)

    def project(lo, width=D_GDN):
        return _dot(h, w_ref[:, lo:lo + width])

    def conv_silu(pre, lo):
        cols = slice(lo, lo + D_GDN)
        groups = rows // SUBLANES
        tap = lambda j: wconv_ref[j:j + 1, cols].reshape(1, 1, D_GDN)
        shape3 = (1, SUBLANES, D_GDN)
        last_p = jnp.where(first_tile, 0.0, cbuf[0:SUBLANES, cols]).reshape(shape3)
        last_b = jnp.where(first_tile, 0.0, cbuf[SUBLANES:, cols]).reshape(shape3)
        cstate_ref[:, cols] = pre[rows - (CONV_W - 1):rows, :]
        cur = pre.reshape(groups, SUBLANES, D_GDN)
        sub = lax.broadcasted_iota(jnp.int32, (CONV_BLOCK, SUBLANES, D_GDN), 1)

        def delay(a, a_last, s):
            rolled = pltpu.roll(a, s, axis=1)
            rolled_prev = jnp.concatenate([pltpu.roll(a_last, s, axis=1), rolled[:-1]], axis=0)
            return jnp.where(sub >= s, rolled, rolled_prev)

        assert CONV_W == 4
        out = []
        for g0 in range(0, groups, CONV_BLOCK):
            p = cur[g0:g0 + CONV_BLOCK]
            dp = delay(p, last_p, 1)
            b = p * tap(1) + dp * tap(0)
            y = p * tap(3) + dp * tap(2) + delay(b, last_b, 2)
            last_p, last_b = p[CONV_BLOCK - 1:], b[CONV_BLOCK - 1:]
            out.append(_silu(y))
        cbuf[0:SUBLANES, cols] = last_p.reshape(SUBLANES, D_GDN)
        cbuf[SUBLANES:, cols] = last_b.reshape(SUBLANES, D_GDN)
        return jnp.concatenate(out, axis=0).reshape(rows, D_GDN)

    pre_vv = project(COL_VV)
    pre_u = project(COL_U)
    vg = _layer_norm(_gelu_tanh(pre_vv), lng_ref[...], lnb_ref[...]).astype(BF16)
    ab = project(COL_AB, LANES)
    pre_q = project(COL_Q)
    u_act = _gelu_tanh(pre_u)
    ti = lax.broadcasted_iota(jnp.int32, (SGU_CHUNK, SGU_CHUNK), 0)
    si = lax.broadcasted_iota(jnp.int32, (SGU_CHUNK, SGU_CHUNK), 1)
    for hh in range(HEADS):
        lo = hh * DH
        w_tril = jnp.where(ti >= si, ws_ref[hh], 0.0).astype(BF16)
        bias = bst_ref[:, hh:hh + 1]
        for c in range(rows // SGU_CHUNK):
            r0 = c * SGU_CHUNK
            mix = _dot(w_tril, vg[r0:r0 + SGU_CHUNK, lo:lo + DH]) + bias
            sg_ref[r0:r0 + SGU_CHUNK, lo:lo + DH] = (u_act[r0:r0 + SGU_CHUNK, lo:lo + DH] * mix).astype(BF16)
    pre_k = project(COL_K)

    lane = lax.broadcasted_iota(jnp.int32, (GDN_CHUNK, LANES), 1)
    g_col = -jnp.exp(alog_row_ref[...]) * _softplus(ab + dtb_row_ref[...])
    abt = _dot_nt(wabt_ref[...], h)[0:2 * HEADS, :]
    sub = lax.broadcasted_iota(jnp.int32, (2 * HEADS, GDN_CHUNK), 0)
    g_row = -jnp.exp(alog_col_ref[...]) * _softplus(abt + dtb_col_ref[...])
    ri = lax.broadcasted_iota(jnp.int32, (GDN_CHUNK, GDN_CHUNK), 0)
    ci = lax.broadcasted_iota(jnp.int32, (GDN_CHUNK, GDN_CHUNK), 1)
    tri_l = (ri >= ci).astype(BF16)
    tri_u = (ri <= ci).astype(BF16)
    beta_col = _sigmoid(ab)
    beta_row = _sigmoid(abt)
    for c in range(rows // GDN_CHUNK):
        r0 = c * GDN_CHUNK
        p0, p1, p2 = _split3(g_col[r0:r0 + GDN_CHUNK, :])
        gc = _dot(tri_l, p0) + _dot(tri_l, p1) + _dot(tri_l, p2)
        col_ref[r0:r0 + GDN_CHUNK, :] = jnp.where(lane < HEADS, beta_col[r0:r0 + GDN_CHUNK, :], gc)
        p0, p1, p2 = _split3(g_row[:, r0:r0 + GDN_CHUNK])
        gr = _dot(p0, tri_u) + _dot(p1, tri_u) + _dot(p2, tri_u)
        row_ref[c] = jnp.where(sub < HEADS, beta_row[:, r0:r0 + GDN_CHUNK], gr)

    y = conv_silu(pre_q, COL_Q)
    for hh in range(HEADS):
        lo = hh * DH
        qs_ref[:, lo:lo + DH] = (_l2n(y[:, lo:lo + DH]) * (DH ** -0.5)).astype(BF16)
    pre_v = project(COL_V)
    y = conv_silu(pre_k, COL_K)
    for hh in range(HEADS):
        lo = hh * DH
        k_ref[:, lo:lo + DH] = _l2n(y[:, lo:lo + DH]).astype(BF16)
    pre_z = project(COL_Z)
    v_ref[...] = conv_silu(pre_v, COL_V).astype(BF16)
    zs_ref[...] = _silu(pre_z).astype(BF16)


def _proj_call(x, gmix, w_all, wabt, wconv, alog_row, dtb_row, alog_col, dtb_col, lng, lnb, ws, bst, tail_f32):
    b, l, _ = x.shape
    rows = PROJ_ROWS
    nt = l // rows
    steps = b * nt
    nchunk = rows // GDN_CHUNK
    tile = lambda w: pl.BlockSpec((None, rows, w), lambda i, j: (i, j, 0))
    ins = [x, gmix, w_all, wabt, wconv, alog_row, dtb_row, alog_col, dtb_col, lng, lnb, ws, bst]
    in_specs = [tile(D_MODEL)] + [_resident(a.shape) for a in ins[1:]]
    in_specs += [pl.BlockSpec((None, w.shape[1] // steps, w.shape[2]), lambda i, j: (0, i * nt + j, 0))
                 for w in tail_f32]
    out_shape = [jax.ShapeDtypeStruct((b, l, D_GDN), BF16)] * 5 + [
        jax.ShapeDtypeStruct((b, l, LANES), F32),
        jax.ShapeDtypeStruct((b, l // GDN_CHUNK, 2 * HEADS, GDN_CHUNK), F32),
        jax.ShapeDtypeStruct((b, CONV_W - 1, D_QKV), F32),
    ] + [jax.ShapeDtypeStruct(w.shape[1:], BF16) for w in tail_f32]
    out_specs = [tile(D_GDN)] * 5 + [
        tile(LANES),
        pl.BlockSpec((None, nchunk, 2 * HEADS, GDN_CHUNK), lambda i, j: (i, j, 0, 0)),
        pl.BlockSpec((None, CONV_W - 1, D_QKV), lambda i, j: (i, 0, 0)),
    ] + [pl.BlockSpec((w.shape[1] // steps, w.shape[2]), lambda i, j: (i * nt + j, 0)) for w in tail_f32]
    ins = ins + list(tail_f32)
    return pl.pallas_call(
        _proj_kernel,
        grid=(b, nt),
        in_specs=in_specs,
        out_specs=out_specs,
        out_shape=out_shape,
        scratch_shapes=[pltpu.VMEM((2 * SUBLANES, D_QKV), F32)],
        compiler_params=pltpu.CompilerParams(
            dimension_semantics=("arbitrary", "arbitrary"), vmem_limit_bytes=VMEM_LIMIT),
        name="proj",
    )(*ins)


def _neumann_many(a_list):
    c = a_list[0].shape[0]
    p = [-a for a in a_list]
    pb = [x.astype(BF16) for x in p]
    q = [_dot(x, x) for x in pb]
    yield
    power = 2
    while 2 * power < c:
        qb = [x.astype(BF16) for x in q]
        r = [_dot(jnp.concatenate([pi.astype(BF16), qi], axis=0), qi) for pi, qi in zip(p, qb)]
        yield
        p = [pi + qi + ri[:c] for pi, qi, ri in zip(p, q, r)]
        q = [ri[c:] for ri in r]
        power *= 2
    tail = [_dot(pi.astype(BF16), qi.astype(BF16)) for pi, qi in zip(p, q)]
    yield
    return [pi + qi + ti for pi, qi, ti in zip(p, q, tail)]


def _interleave(steps, fillers):
    fillers = list(fillers)
    count = 0
    while True:
        try:
            next(steps)
        except StopIteration as stop:
            for f in fillers:
                f()
            return stop.value
        count += 1
        if fillers and count % 2 == 0:
            fillers.pop(0)()


def _gdn_kernel(qs_ref, k_ref, v_ref, col_ref, row_ref, zs_ref, gnorm_ref,
                og_ref, state_ref, s_scr):
    rows = qs_ref.shape[0]
    c_ = GDN_CHUNK
    t = pl.program_id(1)

    @pl.when(t == 0)
    def _():
        s_scr[...] = jnp.zeros(s_scr.shape, F32)

    ri = lax.broadcasted_iota(jnp.int32, (c_, c_), 0)
    ci = lax.broadcasted_iota(jnp.int32, (c_, c_), 1)
    incl = ri >= ci
    strict = ri > ci
    gnorm = gnorm_ref[...]

    def prepare(chunks):
        pairs = [(c, hh) for c in chunks for hh in range(HEADS)]
        cols = {c: col_ref[c * c_:(c + 1) * c_, :] for c in chunks}
        rws = {c: row_ref[c] for c in chunks}
        beta = [cols[c][:, hh:hh + 1] for c, hh in pairs]
        gc = [cols[c][:, HEADS + hh:HEADS + hh + 1] for c, hh in pairs]
        gr = [rws[c][HEADS + hh:HEADS + hh + 1, :] for c, hh in pairs]
        glast = [cols[c][c_ - 1:c_, HEADS + hh:HEADS + hh + 1] for c, hh in pairs]
        kh = [k_ref[c * c_:(c + 1) * c_, hh * DH:(hh + 1) * DH] for c, hh in pairs]
        qh = [qs_ref[c * c_:(c + 1) * c_, hh * DH:(hh + 1) * DH] for c, hh in pairs]
        kf = [x.astype(F32) for x in kh]
        kb = [x * b for x, b in zip(kf, beta)]
        m1 = [_dot_nt(jnp.concatenate([x.astype(BF16), q], axis=0), k) for x, q, k in zip(kb, qh, kh)]
        yield
        decay = [jnp.exp(jnp.where(incl, a - b, -jnp.inf)) for a, b in zip(gc, gr)]
        attn = [(m[c_:] * d).astype(BF16) for m, d in zip(m1, decay)]
        p = yield from _neumann_many([jnp.where(strict, m[:c_] * d, 0.0) for m, d in zip(m1, decay)])
        eg = [jnp.exp(x) for x in gc]
        rhs = [jnp.concatenate([x * e, v_ref[c * c_:(c + 1) * c_, hh * DH:(hh + 1) * DH].astype(F32) * b], axis=1)
               for x, e, b, (c, hh) in zip(kb, eg, beta, pairs)]
        wu = [(x + _dot(pi.astype(BF16), x.astype(BF16))).astype(BF16) for x, pi in zip(rhs, p)]
        yield
        ke_t = [jnp.transpose((x * jnp.exp(gl - g)).astype(BF16)) for x, gl, g in zip(kf, glast, gc)]
        r1 = [_dot(jnp.concatenate([kt, a], axis=0), x) for kt, a, x in zip(ke_t, attn, wu)]
        yield
        staged = {}
        for i, pair in enumerate(pairs):
            q_eff = qh[i].astype(F32) * eg[i] - r1[i][DH:, :DH]
            mq = jnp.concatenate([(-r1[i][:DH, :DH]).astype(BF16), q_eff.astype(BF16)], axis=0)
            staged[pair] = (mq, r1[i][:DH, DH:], r1[i][DH:, DH:], jnp.exp(glast[i]))
        return staged

    state = [s_scr[hh] for hh in range(HEADS)]

    def advance(c, staged):
        r0 = c * c_
        for hh in range(HEADS):
            lo = hh * DH
            mq, n_mat, o_zero, e_last = staged[c, hh]
            r = _dot(mq, state[hh].astype(BF16))
            state[hh] = state[hh] * e_last + r[:DH] + n_mat
            og = _rms(r[DH:] + o_zero, gnorm) * zs_ref[r0:r0 + c_, lo:lo + DH].astype(F32)
            og_ref[r0:r0 + c_, lo:lo + DH] = og.astype(BF16)

    nchunk = rows // c_
    fillers = []
    for g0 in range(0, nchunk, GDN_GROUP):
        group = list(range(g0, g0 + GDN_GROUP))
        staged = _interleave(prepare(group), fillers)
        fillers = [functools.partial(advance, c, staged) for c in group]
    for f in fillers:
        f()
    for hh in range(HEADS):
        s_scr[hh] = state[hh]

    @pl.when(t == pl.num_programs(1) - 1)
    def _():
        state_ref[...] = s_scr[...]


def _gdn_call(qs, k, v, col, row, zs, gnorm):
    b, l, _ = qs.shape
    rows = GDN_ROWS
    nt = l // rows
    nchunk = rows // GDN_CHUNK
    tile = lambda w: pl.BlockSpec((None, rows, w), lambda i, j: (i, j, 0))
    return pl.pallas_call(
        _gdn_kernel,
        grid=(b, nt),
        in_specs=[tile(D_GDN)] * 3 + [
            tile(LANES),
            pl.BlockSpec((None, nchunk, 2 * HEADS, GDN_CHUNK), lambda i, j: (i, j, 0, 0)),
            tile(D_GDN),
            _resident(gnorm.shape),
        ],
        out_specs=[tile(D_GDN), pl.BlockSpec((None, HEADS, DH, DH), lambda i, j: (i, 0, 0, 0))],
        out_shape=[jax.ShapeDtypeStruct((b, l, D_GDN), BF16),
                   jax.ShapeDtypeStruct((b, HEADS, DH, DH), F32)],
        scratch_shapes=[pltpu.VMEM((HEADS, DH, DH), F32)],
        compiler_params=pltpu.CompilerParams(
            dimension_semantics=("arbitrary", "arbitrary"), vmem_limit_bytes=VMEM_LIMIT),
        name="gdn",
    )(qs, k, v, col, row, zs, gnorm)


def _tail_kernel(x_ref, og_ref, sg_ref, p_ref, xs_ref, sgs_ref, ps_ref,
                 s_ref, q_ref, kt_ref, v_ref, col_ref, zs_ref, gnorm_ref,
                 wout_ref, gff_ref, wup_ref, wdown_ref, gple_ref, wple_ref, wgate_ref, gfin_ref,
                 y_ref, ys_ref, snew_ref, ogs_scr):
    def rows_through_tail(x_ref, og, sg_ref, p_ref, y_ref, side_job=None):
        x1 = x_ref[...] + _dot(og, wout_ref[0:D_GDN, :]) + _dot(sg_ref[...], wout_ref[D_GDN:, :])
        h2 = _rms(x1, gff_ref[...]).astype(BF16)
        y_ref[...] = x1
        ff_block = D_FF // 4
        for c in range(D_FF // ff_block):
            lo = c * ff_block
            hid = jnp.maximum(_dot(h2, wup_ref[:, lo:lo + ff_block]), 0.0)
            y_ref[...] += _dot((hid * hid).astype(BF16), wdown_ref[lo:lo + ff_block, :])
            if c == 0 and side_job is not None:
                side_job()
        acc = y_ref[...]
        h3 = _rms(acc, gple_ref[...]).astype(BF16)
        gate = _sigmoid(_dot(h3, wgate_ref[...]))
        x3 = acc + _dot(p_ref[...].astype(BF16), wple_ref[...]) * gate
        y_ref[...] = _rms(x3, gfin_ref[...])

    step = pl.program_id(0)
    last = pl.num_programs(0) - 1
    assert SAMPLE_BLOCK == 2 * SAMPLE_HALF

    @pl.when(step == 0)
    def _():
        ogs_scr[...] = jnp.zeros(ogs_scr.shape, ogs_scr.dtype)

    @pl.when(step < last)
    def _():
        first_row = pl.multiple_of((step // 2) * SAMPLE_BLOCK, SAMPLE_BLOCK)
        og_rows = ogs_scr.at[pl.ds(first_row, SAMPLE_BLOCK), :]
        side_job = functools.partial(_sample_delta_step, step % 2 == 1, s_ref, q_ref, kt_ref,
                                     v_ref, col_ref, zs_ref, gnorm_ref, og_rows, snew_ref)
        rows_through_tail(x_ref, og_ref[...], sg_ref, p_ref, y_ref, side_job)

    @pl.when(step == last)
    def _():
        rows_through_tail(xs_ref, ogs_scr[...].astype(BF16), sgs_ref, ps_ref, ys_ref)


def _tail_call(x, og, sg, p, xs, sgs, ps, state, q, kt, v, col, zs, gnorm,
               wout, gff, wup, wdown, gple, wple, wgate, gfin):
    n = x.shape[0]
    ns = xs.shape[0]
    rows = TAIL_ROWS
    steps = n // rows
    blk = SAMPLE_BLOCK
    per_block = SAMPLE_BLOCK // SAMPLE_HALF
    assert ns == steps * SAMPLE_HALF
    prompt_step = lambda i: jnp.minimum(i, steps - 1)
    tile = lambda w: pl.BlockSpec((rows, w), lambda i: (prompt_step(i), 0))
    srow = lambda w: pl.BlockSpec((blk, w), lambda i: (prompt_step(i) // per_block, 0))
    sample = [xs, sgs, ps]
    weights = [wout, gff, wup, wdown, gple, wple, wgate, gfin]
    state_block = pl.BlockSpec((blk, HEADS, DH, DH), lambda i: (prompt_step(i) // per_block, 0, 0, 0))
    return pl.pallas_call(
        _tail_kernel,
        grid=(steps + 1,),
        in_specs=[tile(D_MODEL), tile(D_GDN), tile(D_SGU), tile(D_PLE)]
        + [_resident(a.shape) for a in sample]
        + [state_block, srow(D_GDN), pl.BlockSpec((None, HEADS, DH, SAMPLE_HALF), lambda i: (prompt_step(i), 0, 0, 0)),
           srow(D_GDN), srow(LANES), srow(D_GDN), _resident(gnorm.shape)]
        + [_resident(a.shape) for a in weights],
        out_specs=[tile(D_MODEL), pl.BlockSpec(xs.shape, lambda i: (0, 0)), state_block],
        out_shape=[jax.ShapeDtypeStruct((n, D_MODEL), F32), jax.ShapeDtypeStruct(xs.shape, F32),
                   jax.ShapeDtypeStruct(state.shape, F32)],
        scratch_shapes=[pltpu.VMEM((ns, D_GDN), F32)],
        compiler_params=pltpu.CompilerParams(
            dimension_semantics=("arbitrary",), vmem_limit_bytes=VMEM_LIMIT),
        name="tail",
    )(x, og, sg, p, *sample, state, q, kt, v, col, zs, gnorm, *weights)


def _sample_proj_kernel(x_ref, hist_ref, gmix_ref, w_ref, wconv_ref,
                        alog_row_ref, dtb_row_ref, lng_ref, lnb_ref, ws_ref, bs_ref,
                        nconv_ref, vg_ref, sg_ref, zs_ref, v_ref, col_ref, q_ref, kt_ref):
    n = x_ref.shape[0]
    h = _rms(x_ref[...], gmix_ref[...]).astype(BF16)
    pre = _dot(h, w_ref[:, COL_Q:COL_Q + D_QKV])
    y = pre * wconv_ref[CONV_W - 1:CONV_W, :]
    for j in range(CONV_W - 1):
        y = y + hist_ref[j] * wconv_ref[j:j + 1, :]
        if j > 0:
            nconv_ref[j - 1] = hist_ref[j]
    nconv_ref[CONV_W - 2] = pre
    y = _silu(y)
    v_ref[...] = y[:, 2 * D_GDN:]
    blocks = n // SAMPLE_HALF
    for hh in range(HEADS):
        lo = hh * DH
        q_ref[:, lo:lo + DH] = _l2n(y[:, lo:lo + DH]) * (DH ** -0.5)
        kt = jnp.transpose(_l2n(y[:, D_GDN + lo:D_GDN + lo + DH]))
        for i in range(blocks):
            kt_ref[i, hh] = kt[:, i * SAMPLE_HALF:(i + 1) * SAMPLE_HALF]
    zs_ref[...] = _silu(_dot(h, w_ref[:, COL_Z:COL_Z + D_GDN]))
    ab = _dot(h, w_ref[:, COL_AB:COL_AB + LANES])
    lane = lax.broadcasted_iota(jnp.int32, ab.shape, 1)
    g_col = -jnp.exp(alog_row_ref[...]) * _softplus(ab + dtb_row_ref[...])
    col_ref[...] = jnp.where(lane < HEADS, _sigmoid(ab), g_col)
    uv = _gelu_tanh(_dot(h, w_ref[:, COL_U:COL_U + 2 * D_SGU]))
    vg = _layer_norm(uv[:, D_SGU:], lng_ref[...], lnb_ref[...])
    vg_ref[...] = vg
    for hh in range(HEADS):
        lo = hh * DH
        mix = ws_ref[hh, 0:1, 0:1] * vg[:, lo:lo + DH] + bs_ref[hh:hh + 1, 0:1]
        sg_ref[:, lo:lo + DH] = (uv[:, lo:lo + DH] * mix).astype(BF16)


def _sample_proj_call(x, hist, gmix, w_all, wconv, alog_row, dtb_row, lng, lnb, ws, bs):
    n = x.shape[0]
    blocks = n // SAMPLE_HALF
    out_shape = [
        jax.ShapeDtypeStruct((CONV_W - 1, n, D_QKV), F32),
        jax.ShapeDtypeStruct((n, D_SGU), F32),
        jax.ShapeDtypeStruct((n, D_SGU), BF16),
        jax.ShapeDtypeStruct((n, D_GDN), F32),
        jax.ShapeDtypeStruct((n, D_GDN), F32),
        jax.ShapeDtypeStruct((n, LANES), F32),
        jax.ShapeDtypeStruct((n, D_GDN), F32),
        jax.ShapeDtypeStruct((blocks, HEADS, DH, SAMPLE_HALF), F32),
    ]
    return pl.pallas_call(
        _sample_proj_kernel,
        out_shape=out_shape,
        compiler_params=pltpu.CompilerParams(vmem_limit_bytes=VMEM_LIMIT),
        name="sample_proj",
    )(x, hist, gmix, w_all, wconv, alog_row, dtb_row, lng, lnb, ws, bs)


def _sample_delta_step(upper, s_ref, q_ref, kt_ref, v_ref, col_ref, zs_ref, gnorm_ref, og_ref, snew_ref):
    gnorm = gnorm_ref[...]
    row0 = jnp.where(upper, SAMPLE_HALF, 0)
    pick = lambda blk, bb: jnp.where(upper, blk[SAMPLE_HALF + bb:SAMPLE_HALF + bb + 1], blk[bb:bb + 1])
    col_blk, v_blk, zs_blk = col_ref[...], v_ref[...], zs_ref[...]
    q_rows = [q_ref[:, hh * DH:(hh + 1) * DH].astype(BF16) for hh in range(HEADS)]
    pairs = [(bb, hh) for bb in range(SAMPLE_HALF) for hh in range(HEADS)]
    kcol = [kt_ref[hh, :, bb:bb + 1] for bb, hh in pairs]
    beta = [pick(col_blk[:, hh:hh + 1], bb) for bb, hh in pairs]
    eg = [jnp.exp(pick(col_blk[:, HEADS + hh:HEADS + hh + 1], bb)) for bb, hh in pairs]
    sd = [s_ref[row0 + bb, hh] * e for (bb, hh), e in zip(pairs, eg)]
    sk = [jnp.sum(s * kc, axis=0, keepdims=True) for s, kc in zip(sd, kcol)]
    v_new = [b * (pick(v_blk[:, hh * DH:(hh + 1) * DH], bb) - x) for (bb, hh), b, x in zip(pairs, beta, sk)]
    s1 = [s + kc * vn for s, kc, vn in zip(sd, kcol, v_new)]
    for (bb, hh), s in zip(pairs, s1):
        snew_ref[row0 + bb, hh] = s
    o = [pick(_dot(q_rows[hh], s.astype(BF16)), bb) for (bb, hh), s in zip(pairs, s1)]
    sub = lax.broadcasted_iota(jnp.int32, (SAMPLE_BLOCK, DH), 0)
    for hh in range(HEADS):
        lo = hh * DH
        merged = og_ref[:, lo:lo + DH]
        for bb in range(SAMPLE_HALF):
            og = _rms(o[bb * HEADS + hh], gnorm) * pick(zs_blk[:, lo:lo + DH], bb)
            merged = jnp.where(sub == row0 + bb, og.astype(og_ref.dtype), merged)
        og_ref[:, lo:lo + DH] = merged


def kernel(x_prompt, x_sample, state_conv, state_gdn, p_prompt, p_sample, g_mix, w_in, w_conv, a_log, dt_bias, gdn_norm, sgu_ln_g, sgu_ln_b, w_s, b_s, w_out, g_ff, w_up, w_down, g_ple, w_ple, w_ple_gate, g_final):
    depth = w_in.shape[0]
    assert depth == 1, "single-layer trunk only"
    b, l, d = x_prompt.shape
    n = x_sample.shape[0]
    assert x_sample.shape[1] == 1 and d == D_MODEL
    assert l % PROJ_ROWS == 0 and l % GDN_ROWS == 0 and (b * l) % TAIL_ROWS == 0 and n % SAMPLE_BLOCK == 0

    w_all, wabt = _prep_call(w_in)
    row2 = lambda a: a.reshape(1, -1)
    pad_lanes = lambda a: jnp.pad(a, (HEADS, LANES - 2 * HEADS)).reshape(1, LANES)
    pad_rows = lambda a: jnp.pad(a, (HEADS, 0)).reshape(2 * HEADS, 1)
    alog_row, dtb_row = pad_lanes(a_log[0]), pad_lanes(dt_bias[0])
    alog_col, dtb_col = pad_rows(a_log[0]), pad_rows(dt_bias[0])
    gmix, gff, gple, gfin = row2(g_mix[0]), row2(g_ff[0]), row2(g_ple[0]), row2(g_final)
    lng, lnb, gnorm = row2(sgu_ln_g[0]), row2(sgu_ln_b[0]), row2(gdn_norm[0])

    qs, k, v, zs, sg, col, row, conv_p, wout, wup, wdown, wple, wgate = _proj_call(
        x_prompt, gmix, w_all, wabt, w_conv[0], alog_row, dtb_row, alog_col, dtb_col,
        lng, lnb, w_s[0], b_s[0].T, (w_out, w_up, w_down, w_ple, w_ple_gate))
    tail_w = (wout, gff, wup, wdown, gple, wple, wgate, gfin)
    og, gdn_p = _gdn_call(qs, k, v, col, row, zs, gnorm)

    nconv, vg_s, sg_s, zs_s, v_s, col_s, q_s, kt = _sample_proj_call(
        x_sample.reshape(n, d), jnp.transpose(state_conv[0], (1, 0, 2)), gmix, w_all,
        w_conv[0], alog_row, dtb_row, lng, lnb, w_s[0], b_s[0])

    y_prompt, y_sample, gdn_s = _tail_call(
        x_prompt.reshape(b * l, d), og.reshape(b * l, D_GDN), sg.reshape(b * l, D_SGU),
        p_prompt[0].reshape(b * l, D_PLE),
        x_sample.reshape(n, d), sg_s, p_sample[0].reshape(n, D_PLE),
        state_gdn[0], q_s, kt, v_s, col_s, zs_s, gnorm, *tail_w)
    y_prompt, y_sample = y_prompt.reshape(b, l, d), y_sample.reshape(n, 1, d)

    return (y_prompt, y_sample,
            conv_p[None], gdn_p[None],
            jnp.transpose(nconv, (1, 0, 2))[None], gdn_s[None],
            vg_s.reshape(n, 1, D_SGU)[None])
```

```python
import functools

import jax
import jax.numpy as jnp
from jax import lax
from jax.experimental import pallas as pl
from jax.experimental.pallas import tpu as pltpu

F32 = jnp.float32
BF16 = jnp.bfloat16

D_MODEL = 1024
D_GDN = 512
D_SGU = 512
HEADS = 4
DH = 128
D_QKV = 3 * D_GDN
D_FF = 4 * D_MODEL
D_PLE = 256
CONV_W = 4
GDN_CHUNK = 64
SGU_CHUNK = 128
EPS = 1e-6
LANES = 128
SUBLANES = 8
VMEM_LIMIT = 56 * 1024 * 1024

COL_Q, COL_K, COL_V, COL_Z, COL_U, COL_VV, COL_AB = 0, 512, 1024, 1536, 2048, 2560, 3072
W_COLS = COL_AB + LANES

PROJ_ROWS = 1024
CONV_BLOCK = 8
GDN_ROWS = 2048
GDN_GROUP = 4
TAIL_ROWS = 512
SAMPLE_BLOCK = 8
SAMPLE_HALF = 4
PREP_STEPS = 8


def _dot(a, b):
    return jnp.dot(a, b, preferred_element_type=F32)


def _dot_nt(a, b):
    return lax.dot_general(a, b, (((1,), (1,)), ((), ())), preferred_element_type=F32)


def _dot_tn(a, b):
    return lax.dot_general(a, b, (((0,), (0,)), ((), ())), preferred_element_type=F32)


def _split3(x):
    hi = x.astype(BF16)
    r1 = x - hi.astype(F32)
    mid = r1.astype(BF16)
    lo = (r1 - mid.astype(F32)).astype(BF16)
    return hi, mid, lo


def _rms(x, gain):
    return x * lax.rsqrt(jnp.mean(x * x, axis=-1, keepdims=True) + EPS) * gain


def _sigmoid(x):
    return 1.0 / (1.0 + jnp.exp(-x))


def _silu(x):
    hx = 0.5 * x
    return hx + hx * jnp.tanh(hx)


def _softplus(x):
    return jnp.maximum(x, 0.0) + jnp.log(1.0 + jnp.exp(-jnp.abs(x)))


def _gelu_tanh(x):
    c = 0.7978845608028654
    hx = 0.5 * x
    return hx + hx * jnp.tanh(x * (c + (c * 0.044715) * (x * x)))


def _l2n(x):
    return x * lax.rsqrt(jnp.sum(x * x, axis=-1, keepdims=True) + 1e-6)


def _layer_norm(x, gain, bias):
    mu = jnp.mean(x, axis=-1, keepdims=True)
    xc = x - mu
    var = jnp.mean(xc * xc, axis=-1, keepdims=True)
    return xc * lax.rsqrt(var + EPS) * gain + bias


def _resident(shape):
    nd = len(shape)
    return pl.BlockSpec(shape, lambda *_: (0,) * nd, pipeline_mode=pl.Buffered(1))


def _prep_kernel(win_ref, wall_o, wabt_o):
    o1 = D_QKV + D_GDN
    o3 = o1 + 2 * HEADS

    def put(dst, src):
        wall_o[:, dst:dst + LANES] = jnp.transpose(win_ref[src:src + LANES, :]).astype(BF16)

    for j in range(o1 // LANES):
        put(j * LANES, j * LANES)
    for j in range(2 * D_SGU // LANES):
        put(COL_U + j * LANES, o3 + j * LANES)
    ab = win_ref[o1:o3, :]
    ab16 = jnp.concatenate([ab, jnp.zeros((2 * SUBLANES - 2 * HEADS, LANES), F32)], axis=0)
    wabt_o[...] = ab16.astype(BF16)
    ab128 = jnp.concatenate([ab, jnp.zeros((LANES - 2 * HEADS, LANES), F32)], axis=0)
    wall_o[:, COL_AB:] = jnp.transpose(ab128).astype(BF16)


def _prep_call(w_in):
    steps = PREP_STEPS
    d, d_in = w_in.shape[1], w_in.shape[2]
    assert d // steps == LANES
    win_t = jnp.transpose(w_in[0])
    return pl.pallas_call(
        _prep_kernel,
        grid=(steps,),
        in_specs=[pl.BlockSpec((d_in, LANES), lambda i: (0, i))],
        out_specs=[pl.BlockSpec((LANES, W_COLS), lambda i: (i, 0)),
                   pl.BlockSpec((2 * SUBLANES, LANES), lambda i: (0, i))],
        out_shape=[jax.ShapeDtypeStruct((d, W_COLS), BF16), jax.ShapeDtypeStruct((2 * SUBLANES, d), BF16)],
        compiler_params=pltpu.CompilerParams(
            dimension_semantics=("arbitrary",), vmem_limit_bytes=VMEM_LIMIT),
        name="prep",
    )(win_t)


def _proj_kernel(x_ref, gmix_ref, w_ref, wabt_ref, wconv_ref,
                 alog_row_ref, dtb_row_ref, alog_col_ref, dtb_col_ref, lng_ref, lnb_ref,
                 ws_ref, bst_ref, wout_ref, wup_ref, wdown_ref, wple_ref, wgate_ref,
                 qs_ref, k_ref, v_ref, zs_ref, sg_ref, col_ref, row_ref, cstate_ref,
                 wout_o, wup_o, wdown_o, wple_o, wgate_o,
                 cbuf):
    rows = x_ref.shape[0]
    first_tile = pl.program_id(1) == 0
    wout_o[...] = wout_ref[...].astype(BF16)
    wup_o[...] = wup_ref[...].astype(BF16)
    wdown_o[...] = wdown_ref[...].astype(BF16)
    wple_o[...] = wple_ref[...].astype(BF16)
    wgate_o[...] = wgate_ref[...].astype(BF16)
    h = _rms(x_ref[...], gmix_ref[...]).astype(BF16)

    def project(lo, width=D_GDN):
        return _dot(h, w_ref[:, lo:lo + width])

    def conv_silu(pre, lo):
        cols = slice(lo, lo + D_GDN)
        groups = rows // SUBLANES
        tap = lambda j: wconv_ref[j:j + 1, cols].reshape(1, 1, D_GDN)
        shape3 = (1, SUBLANES, D_GDN)
        last_p = jnp.where(first_tile, 0.0, cbuf[0:SUBLANES, cols]).reshape(shape3)
        last_b = jnp.where(first_tile, 0.0, cbuf[SUBLANES:, cols]).reshape(shape3)
        cstate_ref[:, cols] = pre[rows - (CONV_W - 1):rows, :]
        cur = pre.reshape(groups, SUBLANES, D_GDN)
        sub = lax.broadcasted_iota(jnp.int32, (CONV_BLOCK, SUBLANES, D_GDN), 1)

        def delay(a, a_last, s):
            rolled = pltpu.roll(a, s, axis=1)
            rolled_prev = jnp.concatenate([pltpu.roll(a_last, s, axis=1), rolled[:-1]], axis=0)
            return jnp.where(sub >= s, rolled, rolled_prev)

        assert CONV_W == 4
        out = []
        for g0 in range(0, groups, CONV_BLOCK):
            p = cur[g0:g0 + CONV_BLOCK]
            dp = delay(p, last_p, 1)
            b = p * tap(1) + dp * tap(0)
            y = p * tap(3) + dp * tap(2) + delay(b, last_b, 2)
            last_p, last_b = p[CONV_BLOCK - 1:], b[CONV_BLOCK - 1:]
            out.append(_silu(y))
        cbuf[0:SUBLANES, cols] = last_p.reshape(SUBLANES, D_GDN)
        cbuf[SUBLANES:, cols] = last_b.reshape(SUBLANES, D_GDN)
        return jnp.concatenate(out, axis=0).reshape(rows, D_GDN)

    pre_vv = project(COL_VV)
    pre_u = project(COL_U)
    vg = _layer_norm(_gelu_tanh(pre_vv), lng_ref[...], lnb_ref[...]).astype(BF16)
    ab = project(COL_AB, LANES)
    pre_q = project(COL_Q)
    u_act = _gelu_tanh(pre_u)
    ti = lax.broadcasted_iota(jnp.int32, (SGU_CHUNK, SGU_CHUNK), 0)
    si = lax.broadcasted_iota(jnp.int32, (SGU_CHUNK, SGU_CHUNK), 1)
    for hh in range(HEADS):
        lo = hh * DH
        w_tril = jnp.where(ti >= si, ws_ref[hh], 0.0).astype(BF16)
        bias = bst_ref[:, hh:hh + 1]
        for c in range(rows // SGU_CHUNK):
            r0 = c * SGU_CHUNK
            mix = _dot(w_tril, vg[r0:r0 + SGU_CHUNK, lo:lo + DH]) + bias
            sg_ref[r0:r0 + SGU_CHUNK, lo:lo + DH] = (u_act[r0:r0 + SGU_CHUNK, lo:lo + DH] * mix).astype(BF16)
    pre_k = project(COL_K)

    lane = lax.broadcasted_iota(jnp.int32, (GDN_CHUNK, LANES), 1)
    g_col = -jnp.exp(alog_row_ref[...]) * _softplus(ab + dtb_row_ref[...])
    abt = _dot_nt(wabt_ref[...], h)[0:2 * HEADS, :]
    sub = lax.broadcasted_iota(jnp.int32, (2 * HEADS, GDN_CHUNK), 0)
    g_row = -jnp.exp(alog_col_ref[...]) * _softplus(abt + dtb_col_ref[...])
    ri = lax.broadcasted_iota(jnp.int32, (GDN_CHUNK, GDN_CHUNK), 0)
    ci = lax.broadcasted_iota(jnp.int32, (GDN_CHUNK, GDN_CHUNK), 1)
    tri_l = (ri >= ci).astype(BF16)
    tri_u = (ri <= ci).astype(BF16)
    beta_col = _sigmoid(ab)
    beta_row = _sigmoid(abt)
    for c in range(rows // GDN_CHUNK):
        r0 = c * GDN_CHUNK
        p0, p1, p2 = _split3(g_col[r0:r0 + GDN_CHUNK, :])
        gc = _dot(tri_l, p0) + _dot(tri_l, p1) + _dot(tri_l, p2)
        col_ref[r0:r0 + GDN_CHUNK, :] = jnp.where(lane < HEADS, beta_col[r0:r0 + GDN_CHUNK, :], gc)
        p0, p1, p2 = _split3(g_row[:, r0:r0 + GDN_CHUNK])
        gr = _dot(p0, tri_u) + _dot(p1, tri_u) + _dot(p2, tri_u)
        row_ref[c] = jnp.where(sub < HEADS, beta_row[:, r0:r0 + GDN_CHUNK], gr)

    y = conv_silu(pre_q, COL_Q)
    for hh in range(HEADS):
        lo = hh * DH
        qs_ref[:, lo:lo + DH] = (_l2n(y[:, lo:lo + DH]) * (DH ** -0.5)).astype(BF16)
    pre_v = project(COL_V)
    y = conv_silu(pre_k, COL_K)
    for hh in range(HEADS):
        lo = hh * DH
        k_ref[:, lo:lo + DH] = _l2n(y[:, lo:lo + DH]).astype(BF16)
    pre_z = project(COL_Z)
    v_ref[...] = conv_silu(pre_v, COL_V).astype(BF16)
    zs_ref[...] = _silu(pre_z).astype(BF16)


def _proj_call(x, gmix, w_all, wabt, wconv, alog_row, dtb_row, alog_col, dtb_col, lng, lnb, ws, bst, tail_f32):
    b, l, _ = x.shape
    rows = PROJ_ROWS
    nt = l // rows
    steps = b * nt
    nchunk = rows // GDN_CHUNK
    tile = lambda w: pl.BlockSpec((None, rows, w), lambda i, j: (i, j, 0))
    ins = [x, gmix, w_all, wabt, wconv, alog_row, dtb_row, alog_col, dtb_col, lng, lnb, ws, bst]
    in_specs = [tile(D_MODEL)] + [_resident(a.shape) for a in ins[1:]]
    in_specs += [pl.BlockSpec((None, w.shape[1] // steps, w.shape[2]), lambda i, j: (0, i * nt + j, 0))
                 for w in tail_f32]
    out_shape = [jax.ShapeDtypeStruct((b, l, D_GDN), BF16)] * 5 + [
        jax.ShapeDtypeStruct((b, l, LANES), F32),
        jax.ShapeDtypeStruct((b, l // GDN_CHUNK, 2 * HEADS, GDN_CHUNK), F32),
        jax.ShapeDtypeStruct((b, CONV_W - 1, D_QKV), F32),
    ] + [jax.ShapeDtypeStruct(w.shape[1:], BF16) for w in tail_f32]
    out_specs = [tile(D_GDN)] * 5 + [
        tile(LANES),
        pl.BlockSpec((None, nchunk, 2 * HEADS, GDN_CHUNK), lambda i, j: (i, j, 0, 0)),
        pl.BlockSpec((None, CONV_W - 1, D_QKV), lambda i, j: (i, 0, 0)),
    ] + [pl.BlockSpec((w.shape[1] // steps, w.shape[2]), lambda i, j: (i * nt + j, 0)) for w in tail_f32]
    ins = ins + list(tail_f32)
    return pl.pallas_call(
        _proj_kernel,
        grid=(b, nt),
        in_specs=in_specs,
        out_specs=out_specs,
        out_shape=out_shape,
        scratch_shapes=[pltpu.VMEM((2 * SUBLANES, D_QKV), F32)],
        compiler_params=pltpu.CompilerParams(
            dimension_semantics=("arbitrary", "arbitrary"), vmem_limit_bytes=VMEM_LIMIT),
        name="proj",
    )(*ins)


def _neumann_many(a_list):
    c = a_list[0].shape[0]
    p = [-a for a in a_list]
    pb = [x.astype(BF16) for x in p]
    q = [_dot(x, x) for x in pb]
    yield
    power = 2
    while 2 * power < c:
        qb = [x.astype(BF16) for x in q]
        r = [_dot(jnp.concatenate([pi.astype(BF16), qi], axis=0), qi) for pi, qi in zip(p, qb)]
        yield
        p = [pi + qi + ri[:c] for pi, qi, ri in zip(p, q, r)]
        q = [ri[c:] for ri in r]
        power *= 2
    tail = [_dot(pi.astype(BF16), qi.astype(BF16)) for pi, qi in zip(p, q)]
    yield
    return [pi + qi + ti for pi, qi, ti in zip(p, q, tail)]


def _interleave(steps, fillers):
    fillers = list(fillers)
    count = 0
    while True:
        try:
            next(steps)
        except StopIteration as stop:
            for f in fillers:
                f()
            return stop.value
        count += 1
        if fillers and count % 2 == 0:
            fillers.pop(0)()


def _gdn_kernel(qs_ref, k_ref, v_ref, col_ref, row_ref, zs_ref, gnorm_ref,
                og_ref, state_ref, s_scr):
    rows = qs_ref.shape[0]
    c_ = GDN_CHUNK
    t = pl.program_id(1)

    @pl.when(t == 0)
    def _():
        s_scr[...] = jnp.zeros(s_scr.shape, F32)

    ri = lax.broadcasted_iota(jnp.int32, (c_, c_), 0)
    ci = lax.broadcasted_iota(jnp.int32, (c_, c_), 1)
    incl = ri >= ci
    strict = ri > ci
    gnorm = gnorm_ref[...]

    def prepare(chunks):
        pairs = [(c, hh) for c in chunks for hh in range(HEADS)]
        cols = {c: col_ref[c * c_:(c + 1) * c_, :] for c in chunks}
        rws = {c: row_ref[c] for c in chunks}
        beta = [cols[c][:, hh:hh + 1] for c, hh in pairs]
        gc = [cols[c][:, HEADS + hh:HEADS + hh + 1] for c, hh in pairs]
        gr = [rws[c][HEADS + hh:HEADS + hh + 1, :] for c, hh in pairs]
        glast = [cols[c][c_ - 1:c_, HEADS + hh:HEADS + hh + 1] for c, hh in pairs]
        kh = [k_ref[c * c_:(c + 1) * c_, hh * DH:(hh + 1) * DH] for c, hh in pairs]
        qh = [qs_ref[c * c_:(c + 1) * c_, hh * DH:(hh + 1) * DH] for c, hh in pairs]
        kf = [x.astype(F32) for x in kh]
        kb = [x * b for x, b in zip(kf, beta)]
        m1 = [_dot_nt(jnp.concatenate([x.astype(BF16), q], axis=0), k) for x, q, k in zip(kb, qh, kh)]
        yield
        decay = [jnp.exp(jnp.where(incl, a - b, -jnp.inf)) for a, b in zip(gc, gr)]
        attn = [(m[c_:] * d).astype(BF16) for m, d in zip(m1, decay)]
        p = yield from _neumann_many([jnp.where(strict, m[:c_] * d, 0.0) for m, d in zip(m1, decay)])
        eg = [jnp.exp(x) for x in gc]
        rhs = [jnp.concatenate([x * e, v_ref[c * c_:(c + 1) * c_, hh * DH:(hh + 1) * DH].astype(F32) * b], axis=1)
               for x, e, b, (c, hh) in zip(kb, eg, beta, pairs)]
        wu = [(x + _dot(pi.astype(BF16), x.astype(BF16))).astype(BF16) for x, pi in zip(rhs, p)]
        yield
        ke_t = [jnp.transpose((x * jnp.exp(gl - g)).astype(BF16)) for x, gl, g in zip(kf, glast, gc)]
        r1 = [_dot(jnp.concatenate([kt, a], axis=0), x) for kt, a, x in zip(ke_t, attn, wu)]
        yield
        staged = {}
        for i, pair in enumerate(pairs):
            q_eff = qh[i].astype(F32) * eg[i] - r1[i][DH:, :DH]
            mq = jnp.concatenate([(-r1[i][:DH, :DH]).astype(BF16), q_eff.astype(BF16)], axis=0)
            staged[pair] = (mq, r1[i][:DH, DH:], r1[i][DH:, DH:], jnp.exp(glast[i]))
        return staged

    state = [s_scr[hh] for hh in range(HEADS)]

    def advance(c, staged):
        r0 = c * c_
        for hh in range(HEADS):
            lo = hh * DH
            mq, n_mat, o_zero, e_last = staged[c, hh]
            r = _dot(mq, state[hh].astype(BF16))
            state[hh] = state[hh] * e_last + r[:DH] + n_mat
            og = _rms(r[DH:] + o_zero, gnorm) * zs_ref[r0:r0 + c_, lo:lo + DH].astype(F32)
            og_ref[r0:r0 + c_, lo:lo + DH] = og.astype(BF16)

    nchunk = rows // c_
    fillers = []
    for g0 in range(0, nchunk, GDN_GROUP):
        group = list(range(g0, g0 + GDN_GROUP))
        staged = _interleave(prepare(group), fillers)
        fillers = [functools.partial(advance, c, staged) for c in group]
    for f in fillers:
        f()
    for hh in range(HEADS):
        s_scr[hh] = state[hh]

    @pl.when(t == pl.num_programs(1) - 1)
    def _():
        state_ref[...] = s_scr[...]


def _gdn_call(qs, k, v, col, row, zs, gnorm):
    b, l, _ = qs.shape
    rows = GDN_ROWS
    nt = l // rows
    nchunk = rows // GDN_CHUNK
    tile = lambda w: pl.BlockSpec((None, rows, w), lambda i, j: (i, j, 0))
    return pl.pallas_call(
        _gdn_kernel,
        grid=(b, nt),
        in_specs=[tile(D_GDN)] * 3 + [
            tile(LANES),
            pl.BlockSpec((None, nchunk, 2 * HEADS, GDN_CHUNK), lambda i, j: (i, j, 0, 0)),
            tile(D_GDN),
            _resident(gnorm.shape),
        ],
        out_specs=[tile(D_GDN), pl.BlockSpec((None, HEADS, DH, DH), lambda i, j: (i, 0, 0, 0))],
        out_shape=[jax.ShapeDtypeStruct((b, l, D_GDN), BF16),
                   jax.ShapeDtypeStruct((b, HEADS, DH, DH), F32)],
        scratch_shapes=[pltpu.VMEM((HEADS, DH, DH), F32)],
        compiler_params=pltpu.CompilerParams(
            dimension_semantics=("arbitrary", "arbitrary"), vmem_limit_bytes=VMEM_LIMIT),
        name="gdn",
    )(qs, k, v, col, row, zs, gnorm)


def _tail_kernel(x_ref, og_ref, sg_ref, p_ref, xs_ref, sgs_ref, ps_ref,
                 s_ref, qt_ref, kt_ref, v_ref, col_ref, zs_ref, gnorm_ref,
                 wout_ref, gff_ref, wup_ref, wdown_ref, gple_ref, wple_ref, wgate_ref, gfin_ref,
                 y_ref, ys_ref, snew_ref, ogs_scr):
    def rows_through_tail(x_ref, og, sg_ref, p_ref, y_ref, side_job=None):
        x1 = x_ref[...] + _dot(og, wout_ref[0:D_GDN, :]) + _dot(sg_ref[...], wout_ref[D_GDN:, :])
        h2 = _rms(x1, gff_ref[...]).astype(BF16)
        y_ref[...] = x1
        ff_block = D_FF // 4
        for c in range(D_FF // ff_block):
            lo = c * ff_block
            hid = jnp.maximum(_dot(h2, wup_ref[:, lo:lo + ff_block]), 0.0)
            y_ref[...] += _dot((hid * hid).astype(BF16), wdown_ref[lo:lo + ff_block, :])
            if c == 0 and side_job is not None:
                side_job()
        acc = y_ref[...]
        h3 = _rms(acc, gple_ref[...]).astype(BF16)
        gate = _sigmoid(_dot(h3, wgate_ref[...]))
        x3 = acc + _dot(p_ref[...].astype(BF16), wple_ref[...]) * gate
        y_ref[...] = _rms(x3, gfin_ref[...])

    step = pl.program_id(0)
    last = pl.num_programs(0) - 1
    assert SAMPLE_BLOCK == 2 * SAMPLE_HALF

    @pl.when(step == 0)
    def _():
        ogs_scr[...] = jnp.zeros(ogs_scr.shape, ogs_scr.dtype)

    @pl.when(step < last)
    def _():
        first_row = pl.multiple_of((step // 2) * SAMPLE_BLOCK, SAMPLE_BLOCK)
        og_rows = ogs_scr.at[pl.ds(first_row, SAMPLE_BLOCK), :]
        side_job = functools.partial(_sample_delta_step, step % 2 == 1, s_ref, qt_ref, kt_ref,
                                     v_ref, col_ref, zs_ref, gnorm_ref, og_rows, snew_ref)
        rows_through_tail(x_ref, og_ref[...], sg_ref, p_ref, y_ref, side_job)

    @pl.when(step == last)
    def _():
        rows_through_tail(xs_ref, ogs_scr[...].astype(BF16), sgs_ref, ps_ref, ys_ref)


def _tail_call(x, og, sg, p, xs, sgs, ps, state, q, kt, v, col, zs, gnorm,
               wout, gff, wup, wdown, gple, wple, wgate, gfin):
    n = x.shape[0]
    ns = xs.shape[0]
    rows = TAIL_ROWS
    steps = n // rows
    blk = SAMPLE_BLOCK
    per_block = SAMPLE_BLOCK // SAMPLE_HALF
    assert ns == steps * SAMPLE_HALF
    prompt_step = lambda i: jnp.minimum(i, steps - 1)
    tile = lambda w: pl.BlockSpec((rows, w), lambda i: (prompt_step(i), 0))
    srow = lambda w: pl.BlockSpec((blk, w), lambda i: (prompt_step(i) // per_block, 0))
    sample = [xs, sgs, ps]
    weights = [wout, gff, wup, wdown, gple, wple, wgate, gfin]
    state_block = pl.BlockSpec((blk, HEADS, DH, DH), lambda i: (prompt_step(i) // per_block, 0, 0, 0))
    cols4 = pl.BlockSpec((None, HEADS, DH, SAMPLE_HALF), lambda i: (prompt_step(i), 0, 0, 0))
    return pl.pallas_call(
        _tail_kernel,
        grid=(steps + 1,),
        in_specs=[tile(D_MODEL), tile(D_GDN), tile(D_SGU), tile(D_PLE)]
        + [_resident(a.shape) for a in sample]
        + [state_block, cols4, cols4, srow(D_GDN), srow(LANES), srow(D_GDN), _resident(gnorm.shape)]
        + [_resident(a.shape) for a in weights],
        out_specs=[tile(D_MODEL), pl.BlockSpec(xs.shape, lambda i: (0, 0)), state_block],
        out_shape=[jax.ShapeDtypeStruct((n, D_MODEL), F32), jax.ShapeDtypeStruct(xs.shape, F32),
                   jax.ShapeDtypeStruct(state.shape, F32)],
        scratch_shapes=[pltpu.VMEM((ns, D_GDN), F32)],
        compiler_params=pltpu.CompilerParams(
            dimension_semantics=("arbitrary",), vmem_limit_bytes=VMEM_LIMIT),
        name="tail",
    )(x, og, sg, p, *sample, state, q, kt, v, col, zs, gnorm, *weights)


def _sample_proj_kernel(x_ref, hist_ref, gmix_ref, w_ref, wconv_ref,
                        alog_row_ref, dtb_row_ref, lng_ref, lnb_ref, ws_ref, bs_ref,
                        nconv_ref, vg_ref, sg_ref, zs_ref, v_ref, col_ref, qt_ref, kt_ref):
    n = x_ref.shape[0]
    h = _rms(x_ref[...], gmix_ref[...]).astype(BF16)
    pre = _dot(h, w_ref[:, COL_Q:COL_Q + D_QKV])
    y = pre * wconv_ref[CONV_W - 1:CONV_W, :]
    for j in range(CONV_W - 1):
        y = y + hist_ref[j] * wconv_ref[j:j + 1, :]
        if j > 0:
            nconv_ref[j - 1] = hist_ref[j]
    nconv_ref[CONV_W - 2] = pre
    y = _silu(y)
    v_ref[...] = y[:, 2 * D_GDN:]
    blocks = n // SAMPLE_HALF
    for hh in range(HEADS):
        lo = hh * DH
        qt = jnp.transpose(_l2n(y[:, lo:lo + DH]) * (DH ** -0.5))
        kt = jnp.transpose(_l2n(y[:, D_GDN + lo:D_GDN + lo + DH]))
        for i in range(blocks):
            qt_ref[i, hh] = qt[:, i * SAMPLE_HALF:(i + 1) * SAMPLE_HALF]
            kt_ref[i, hh] = kt[:, i * SAMPLE_HALF:(i + 1) * SAMPLE_HALF]
    zs_ref[...] = _silu(_dot(h, w_ref[:, COL_Z:COL_Z + D_GDN]))
    ab = _dot(h, w_ref[:, COL_AB:COL_AB + LANES])
    lane = lax.broadcasted_iota(jnp.int32, ab.shape, 1)
    g_col = -jnp.exp(alog_row_ref[...]) * _softplus(ab + dtb_row_ref[...])
    col_ref[...] = jnp.where(lane < HEADS, _sigmoid(ab), g_col)
    uv = _gelu_tanh(_dot(h, w_ref[:, COL_U:COL_U + 2 * D_SGU]))
    vg = _layer_norm(uv[:, D_SGU:], lng_ref[...], lnb_ref[...])
    vg_ref[...] = vg
    for hh in range(HEADS):
        lo = hh * DH
        mix = ws_ref[hh, 0:1, 0:1] * vg[:, lo:lo + DH] + bs_ref[hh:hh + 1, 0:1]
        sg_ref[:, lo:lo + DH] = (uv[:, lo:lo + DH] * mix).astype(BF16)


def _sample_proj_call(x, hist, gmix, w_all, wconv, alog_row, dtb_row, lng, lnb, ws, bs):
    n = x.shape[0]
    blocks = n // SAMPLE_HALF
    out_shape = [
        jax.ShapeDtypeStruct((CONV_W - 1, n, D_QKV), F32),
        jax.ShapeDtypeStruct((n, D_SGU), F32),
        jax.ShapeDtypeStruct((n, D_SGU), BF16),
        jax.ShapeDtypeStruct((n, D_GDN), F32),
        jax.ShapeDtypeStruct((n, D_GDN), F32),
        jax.ShapeDtypeStruct((n, LANES), F32),
        jax.ShapeDtypeStruct((blocks, HEADS, DH, SAMPLE_HALF), F32),
        jax.ShapeDtypeStruct((blocks, HEADS, DH, SAMPLE_HALF), F32),
    ]
    return pl.pallas_call(
        _sample_proj_kernel,
        out_shape=out_shape,
        compiler_params=pltpu.CompilerParams(vmem_limit_bytes=VMEM_LIMIT),
        name="sample_proj",
    )(x, hist, gmix, w_all, wconv, alog_row, dtb_row, lng, lnb, ws, bs)


def _sample_delta_step(upper, s_ref, qt_ref, kt_ref, v_ref, col_ref, zs_ref, gnorm_ref, og_ref, snew_ref):
    gnorm = gnorm_ref[...]
    row0 = jnp.where(upper, SAMPLE_HALF, 0)
    pick = lambda blk, bb: jnp.where(upper, blk[SAMPLE_HALF + bb:SAMPLE_HALF + bb + 1], blk[bb:bb + 1])
    col_blk, v_blk, zs_blk = col_ref[...], v_ref[...], zs_ref[...]
    pairs = [(bb, hh) for bb in range(SAMPLE_HALF) for hh in range(HEADS)]
    kcol = [kt_ref[hh, :, bb:bb + 1] for bb, hh in pairs]
    qcol = [qt_ref[hh, :, bb:bb + 1] for bb, hh in pairs]
    beta = [pick(col_blk[:, hh:hh + 1], bb) for bb, hh in pairs]
    eg = [jnp.exp(pick(col_blk[:, HEADS + hh:HEADS + hh + 1], bb)) for bb, hh in pairs]
    sd = [s_ref[row0 + bb, hh] * e for (bb, hh), e in zip(pairs, eg)]
    sk = [jnp.sum(s * kc, axis=0, keepdims=True) for s, kc in zip(sd, kcol)]
    v_new = [b * (pick(v_blk[:, hh * DH:(hh + 1) * DH], bb) - x) for (bb, hh), b, x in zip(pairs, beta, sk)]
    s1 = [s + kc * vn for s, kc, vn in zip(sd, kcol, v_new)]
    for (bb, hh), s in zip(pairs, s1):
        snew_ref[row0 + bb, hh] = s
    o = [jnp.sum(s * qc, axis=0, keepdims=True) for s, qc in zip(s1, qcol)]
    sub = lax.broadcasted_iota(jnp.int32, (SAMPLE_BLOCK, DH), 0)
    for hh in range(HEADS):
        lo = hh * DH
        merged = og_ref[:, lo:lo + DH]
        for bb in range(SAMPLE_HALF):
            og = _rms(o[bb * HEADS + hh], gnorm) * pick(zs_blk[:, lo:lo + DH], bb)
            merged = jnp.where(sub == row0 + bb, og.astype(og_ref.dtype), merged)
        og_ref[:, lo:lo + DH] = merged


def kernel(x_prompt, x_sample, state_conv, state_gdn, p_prompt, p_sample, g_mix, w_in, w_conv, a_log, dt_bias, gdn_norm, sgu_ln_g, sgu_ln_b, w_s, b_s, w_out, g_ff, w_up, w_down, g_ple, w_ple, w_ple_gate, g_final):
    depth = w_in.shape[0]
    assert depth == 1, "single-layer trunk only"
    b, l, d = x_prompt.shape
    n = x_sample.shape[0]
    assert x_sample.shape[1] == 1 and d == D_MODEL
    assert l % PROJ_ROWS == 0 and l % GDN_ROWS == 0 and (b * l) % TAIL_ROWS == 0 and n % SAMPLE_BLOCK == 0

    w_all, wabt = _prep_call(w_in)
    row2 = lambda a: a.reshape(1, -1)
    pad_lanes = lambda a: jnp.pad(a, (HEADS, LANES - 2 * HEADS)).reshape(1, LANES)
    pad_rows = lambda a: jnp.pad(a, (HEADS, 0)).reshape(2 * HEADS, 1)
    alog_row, dtb_row = pad_lanes(a_log[0]), pad_lanes(dt_bias[0])
    alog_col, dtb_col = pad_rows(a_log[0]), pad_rows(dt_bias[0])
    gmix, gff, gple, gfin = row2(g_mix[0]), row2(g_ff[0]), row2(g_ple[0]), row2(g_final)
    lng, lnb, gnorm = row2(sgu_ln_g[0]), row2(sgu_ln_b[0]), row2(gdn_norm[0])

    qs, k, v, zs, sg, col, row, conv_p, wout, wup, wdown, wple, wgate = _proj_call(
        x_prompt, gmix, w_all, wabt, w_conv[0], alog_row, dtb_row, alog_col, dtb_col,
        lng, lnb, w_s[0], b_s[0].T, (w_out, w_up, w_down, w_ple, w_ple_gate))
    tail_w = (wout, gff, wup, wdown, gple, wple, wgate, gfin)
    og, gdn_p = _gdn_call(qs, k, v, col, row, zs, gnorm)

    nconv, vg_s, sg_s, zs_s, v_s, col_s, q_s, kt = _sample_proj_call(
        x_sample.reshape(n, d), jnp.transpose(state_conv[0], (1, 0, 2)), gmix, w_all,
        w_conv[0], alog_row, dtb_row, lng, lnb, w_s[0], b_s[0])

    y_prompt, y_sample, gdn_s = _tail_call(
        x_prompt.reshape(b * l, d), og.reshape(b * l, D_GDN), sg.reshape(b * l, D_SGU),
        p_prompt[0].reshape(b * l, D_PLE),
        x_sample.reshape(n, d), sg_s, p_sample[0].reshape(n, D_PLE),
        state_gdn[0], q_s, kt, v_s, col_s, zs_s, gnorm, *tail_w)
    y_prompt, y_sample = y_prompt.reshape(b, l, d), y_sample.reshape(n, 1, d)

    return (y_prompt, y_sample,
            conv_p[None], gdn_p[None],
            jnp.transpose(nconv, (1, 0, 2))[None], gdn_s[None],
            vg_s.reshape(n, 1, D_SGU)[None])
```

```python
import functools

import jax
import jax.numpy as jnp
from jax import lax
from jax.experimental import pallas as pl
from jax.experimental.pallas import tpu as pltpu

F32 = jnp.float32
BF16 = jnp.bfloat16

D_MODEL = 1024
D_GDN = 512
D_SGU = 512
HEADS = 4
DH = 128
D_QKV = 3 * D_GDN
D_FF = 4 * D_MODEL
D_PLE = 256
CONV_W = 4
GDN_CHUNK = 64
SGU_CHUNK = 128
EPS = 1e-6
LANES = 128
SUBLANES = 8
VMEM_LIMIT = 56 * 1024 * 1024

COL_Q, COL_K, COL_V, COL_Z, COL_U, COL_VV, COL_AB = 0, 512, 1024, 1536, 2048, 2560, 3072
W_COLS = COL_AB + LANES

PROJ_ROWS = 1024
CONV_BLOCK = 8
GDN_ROWS = 2048
GDN_GROUP = 4
TAIL_ROWS = 512
SAMPLE_BLOCK = 8
SAMPLE_HALF = 4
PREP_STEPS = 8


def _dot(a, b):
    return jnp.dot(a, b, preferred_element_type=F32)


def _dot_nt(a, b):
    return lax.dot_general(a, b, (((1,), (1,)), ((), ())), preferred_element_type=F32)


def _dot_tn(a, b):
    return lax.dot_general(a, b, (((0,), (0,)), ((), ())), preferred_element_type=F32)


def _split3(x):
    hi = x.astype(BF16)
    r1 = x - hi.astype(F32)
    mid = r1.astype(BF16)
    lo = (r1 - mid.astype(F32)).astype(BF16)
    return hi, mid, lo


def _rms(x, gain):
    return x * lax.rsqrt(jnp.mean(x * x, axis=-1, keepdims=True) + EPS) * gain


def _sigmoid(x):
    return 1.0 / (1.0 + jnp.exp(-x))


def _silu(x):
    hx = 0.5 * x
    return hx + hx * jnp.tanh(hx)


def _softplus(x):
    return jnp.maximum(x, 0.0) + jnp.log(1.0 + jnp.exp(-jnp.abs(x)))


def _gelu_tanh(x):
    c = 0.7978845608028654
    hx = 0.5 * x
    return hx + hx * jnp.tanh(x * (c + (c * 0.044715) * (x * x)))


def _l2n(x):
    return x * lax.rsqrt(jnp.sum(x * x, axis=-1, keepdims=True) + 1e-6)


def _layer_norm(x, gain, bias):
    mu = jnp.mean(x, axis=-1, keepdims=True)
    xc = x - mu
    var = jnp.mean(xc * xc, axis=-1, keepdims=True)
    return xc * lax.rsqrt(var + EPS) * gain + bias


def _resident(shape):
    nd = len(shape)
    return pl.BlockSpec(shape, lambda *_: (0,) * nd, pipeline_mode=pl.Buffered(1))


def _prep_kernel(win_ref, wall_o, wabt_o):
    o1 = D_QKV + D_GDN
    o3 = o1 + 2 * HEADS

    def put(dst, src):
        wall_o[:, dst:dst + LANES] = jnp.transpose(win_ref[src:src + LANES, :]).astype(BF16)

    for j in range(o1 // LANES):
        put(j * LANES, j * LANES)
    for j in range(2 * D_SGU // LANES):
        put(COL_U + j * LANES, o3 + j * LANES)
    ab = win_ref[o1:o3, :]
    ab16 = jnp.concatenate([ab, jnp.zeros((2 * SUBLANES - 2 * HEADS, LANES), F32)], axis=0)
    wabt_o[...] = ab16.astype(BF16)
    ab128 = jnp.concatenate([ab, jnp.zeros((LANES - 2 * HEADS, LANES), F32)], axis=0)
    wall_o[:, COL_AB:] = jnp.transpose(ab128).astype(BF16)


def _prep_call(w_in):
    steps = PREP_STEPS
    d, d_in = w_in.shape[1], w_in.shape[2]
    assert d // steps == LANES
    win_t = jnp.transpose(w_in[0])
    return pl.pallas_call(
        _prep_kernel,
        grid=(steps,),
        in_specs=[pl.BlockSpec((d_in, LANES), lambda i: (0, i))],
        out_specs=[pl.BlockSpec((LANES, W_COLS), lambda i: (i, 0)),
                   pl.BlockSpec((2 * SUBLANES, LANES), lambda i: (0, i))],
        out_shape=[jax.ShapeDtypeStruct((d, W_COLS), BF16), jax.ShapeDtypeStruct((2 * SUBLANES, d), BF16)],
        compiler_params=pltpu.CompilerParams(
            dimension_semantics=("arbitrary",), vmem_limit_bytes=VMEM_LIMIT),
        name="prep",
    )(win_t)


def _proj_kernel(x_ref, gmix_ref, w_ref, wabt_ref, wconv_ref,
                 alog_row_ref, dtb_row_ref, alog_col_ref, dtb_col_ref, lng_ref, lnb_ref,
                 ws_ref, bst_ref, wout_ref, wup_ref, wdown_ref, wple_ref, wgate_ref,
                 qs_ref, k_ref, v_ref, zs_ref, sg_ref, col_ref, row_ref, cstate_ref,
                 wout_o, wup_o, wdown_o, wple_o, wgate_o,
                 cbuf):
    rows = x_ref.shape[0]
    first_tile = pl.program_id(1) == 0
    wout_o[...] = wout_ref[...].astype(BF16)
    wup_o[...] = wup_ref[...].astype(BF16)
    wdown_o[...] = wdown_ref[...].astype(BF16)
    wple_o[...] = wple_ref[...].astype(BF16)
    wgate_o[...] = wgate_ref[...].astype(BF16)
    h = _rms(x_ref[...], gmix_ref[...]).astype(BF16)

    def project(lo, width=D_GDN):
        return _dot(h, w_ref[:, lo:lo + width])

    def conv_silu(pre, lo):
        cols = slice(lo, lo + D_GDN)
        groups = rows // SUBLANES
        tap = lambda j: wconv_ref[j:j + 1, cols].reshape(1, 1, D_GDN)
        shape3 = (1, SUBLANES, D_GDN)
        last_p = jnp.where(first_tile, 0.0, cbuf[0:SUBLANES, cols]).reshape(shape3)
        last_b = jnp.where(first_tile, 0.0, cbuf[SUBLANES:, cols]).reshape(shape3)
        cstate_ref[:, cols] = pre[rows - (CONV_W - 1):rows, :]
        cur = pre.reshape(groups, SUBLANES, D_GDN)
        sub = lax.broadcasted_iota(jnp.int32, (CONV_BLOCK, SUBLANES, D_GDN), 1)

        def delay(a, a_last, s):
            rolled = pltpu.roll(a, s, axis=1)
            rolled_prev = jnp.concatenate([pltpu.roll(a_last, s, axis=1), rolled[:-1]], axis=0)
            return jnp.where(sub >= s, rolled, rolled_prev)

        assert CONV_W == 4
        out = []
        for g0 in range(0, groups, CONV_BLOCK):
            p = cur[g0:g0 + CONV_BLOCK]
            dp = delay(p, last_p, 1)
            b = p * tap(1) + dp * tap(0)
            y = p * tap(3) + dp * tap(2) + delay(b, last_b, 2)
            last_p, last_b = p[CONV_BLOCK - 1:], b[CONV_BLOCK - 1:]
            out.append(_silu(y))
        cbuf[0:SUBLANES, cols] = last_p.reshape(SUBLANES, D_GDN)
        cbuf[SUBLANES:, cols] = last_b.reshape(SUBLANES, D_GDN)
        return jnp.concatenate(out, axis=0).reshape(rows, D_GDN)

    pre_vv = project(COL_VV)
    pre_u = project(COL_U)
    vg = _layer_norm(_gelu_tanh(pre_vv), lng_ref[...], lnb_ref[...]).astype(BF16)
    ab = project(COL_AB, LANES)
    pre_q = project(COL_Q)
    u_act = _gelu_tanh(pre_u)
    ti = lax.broadcasted_iota(jnp.int32, (SGU_CHUNK, SGU_CHUNK), 0)
    si = lax.broadcasted_iota(jnp.int32, (SGU_CHUNK, SGU_CHUNK), 1)
    for hh in range(HEADS):
        lo = hh * DH
        w_tril = jnp.where(ti >= si, ws_ref[hh], 0.0).astype(BF16)
        bias = bst_ref[:, hh:hh + 1]
        for c in range(rows // SGU_CHUNK):
            r0 = c * SGU_CHUNK
            mix = _dot(w_tril, vg[r0:r0 + SGU_CHUNK, lo:lo + DH]) + bias
            sg_ref[r0:r0 + SGU_CHUNK, lo:lo + DH] = (u_act[r0:r0 + SGU_CHUNK, lo:lo + DH] * mix).astype(BF16)
    pre_k = project(COL_K)

    lane = lax.broadcasted_iota(jnp.int32, (GDN_CHUNK, LANES), 1)
    g_col = -jnp.exp(alog_row_ref[...]) * _softplus(ab + dtb_row_ref[...])
    abt = _dot_nt(wabt_ref[...], h)[0:2 * HEADS, :]
    sub = lax.broadcasted_iota(jnp.int32, (2 * HEADS, GDN_CHUNK), 0)
    g_row = -jnp.exp(alog_col_ref[...]) * _softplus(abt + dtb_col_ref[...])
    ri = lax.broadcasted_iota(jnp.int32, (GDN_CHUNK, GDN_CHUNK), 0)
    ci = lax.broadcasted_iota(jnp.int32, (GDN_CHUNK, GDN_CHUNK), 1)
    tri_l = (ri >= ci).astype(BF16)
    tri_u = (ri <= ci).astype(BF16)
    beta_col = _sigmoid(ab)
    beta_row = _sigmoid(abt)
    for c in range(rows // GDN_CHUNK):
        r0 = c * GDN_CHUNK
        p0, p1, p2 = _split3(g_col[r0:r0 + GDN_CHUNK, :])
        gc = _dot(tri_l, p0) + _dot(tri_l, p1) + _dot(tri_l, p2)
        col_ref[r0:r0 + GDN_CHUNK, :] = jnp.where(lane < HEADS, beta_col[r0:r0 + GDN_CHUNK, :], gc)
        p0, p1, p2 = _split3(g_row[:, r0:r0 + GDN_CHUNK])
        gr = _dot(p0, tri_u) + _dot(p1, tri_u) + _dot(p2, tri_u)
        row_ref[c] = jnp.where(sub < HEADS, beta_row[:, r0:r0 + GDN_CHUNK], gr)

    y = conv_silu(pre_q, COL_Q)
    for hh in range(HEADS):
        lo = hh * DH
        qs_ref[:, lo:lo + DH] = (_l2n(y[:, lo:lo + DH]) * (DH ** -0.5)).astype(BF16)
    pre_v = project(COL_V)
    y = conv_silu(pre_k, COL_K)
    for hh in range(HEADS):
        lo = hh * DH
        k_ref[:, lo:lo + DH] = _l2n(y[:, lo:lo + DH]).astype(BF16)
    pre_z = project(COL_Z)
    v_ref[...] = conv_silu(pre_v, COL_V).astype(BF16)
    zs_ref[...] = _silu(pre_z).astype(BF16)


def _proj_call(x, gmix, w_all, wabt, wconv, alog_row, dtb_row, alog_col, dtb_col, lng, lnb, ws, bst, tail_f32):
    b, l, _ = x.shape
    rows = PROJ_ROWS
    nt = l // rows
    steps = b * nt
    nchunk = rows // GDN_CHUNK
    tile = lambda w: pl.BlockSpec((None, rows, w), lambda i, j: (i, j, 0))
    ins = [x, gmix, w_all, wabt, wconv, alog_row, dtb_row, alog_col, dtb_col, lng, lnb, ws, bst]
    in_specs = [tile(D_MODEL)] + [_resident(a.shape) for a in ins[1:]]
    in_specs += [pl.BlockSpec((None, w.shape[1] // steps, w.shape[2]), lambda i, j: (0, i * nt + j, 0))
                 for w in tail_f32]
    out_shape = [jax.ShapeDtypeStruct((b, l, D_GDN), BF16)] * 5 + [
        jax.ShapeDtypeStruct((b, l, LANES), F32),
        jax.ShapeDtypeStruct((b, l // GDN_CHUNK, 2 * HEADS, GDN_CHUNK), F32),
        jax.ShapeDtypeStruct((b, CONV_W - 1, D_QKV), F32),
    ] + [jax.ShapeDtypeStruct(w.shape[1:], BF16) for w in tail_f32]
    out_specs = [tile(D_GDN)] * 5 + [
        tile(LANES),
        pl.BlockSpec((None, nchunk, 2 * HEADS, GDN_CHUNK), lambda i, j: (i, j, 0, 0)),
        pl.BlockSpec((None, CONV_W - 1, D_QKV), lambda i, j: (i, 0, 0)),
    ] + [pl.BlockSpec((w.shape[1] // steps, w.shape[2]), lambda i, j: (i * nt + j, 0)) for w in tail_f32]
    ins = ins + list(tail_f32)
    return pl.pallas_call(
        _proj_kernel,
        grid=(b, nt),
        in_specs=in_specs,
        out_specs=out_specs,
        out_shape=out_shape,
        scratch_shapes=[pltpu.VMEM((2 * SUBLANES, D_QKV), F32)],
        compiler_params=pltpu.CompilerParams(
            dimension_semantics=("arbitrary", "arbitrary"), vmem_limit_bytes=VMEM_LIMIT),
        name="proj",
    )(*ins)


def _neumann_many(a_list):
    c = a_list[0].shape[0]
    p = [-a for a in a_list]
    pb = [x.astype(BF16) for x in p]
    q = [_dot(x, x) for x in pb]
    yield
    power = 2
    while 2 * power < c:
        qb = [x.astype(BF16) for x in q]
        r = [_dot(jnp.concatenate([pi.astype(BF16), qi], axis=0), qi) for pi, qi in zip(p, qb)]
        yield
        p = [pi + qi + ri[:c] for pi, qi, ri in zip(p, q, r)]
        q = [ri[c:] for ri in r]
        power *= 2
    tail = [_dot(pi.astype(BF16), qi.astype(BF16)) for pi, qi in zip(p, q)]
    yield
    return [pi + qi + ti for pi, qi, ti in zip(p, q, tail)]


def _interleave(steps, fillers):
    fillers = list(fillers)
    count = 0
    while True:
        try:
            next(steps)
        except StopIteration as stop:
            for f in fillers:
                f()
            return stop.value
        count += 1
        if fillers and count % 2 == 0:
            fillers.pop(0)()


def _gdn_kernel(qs_ref, k_ref, v_ref, col_ref, row_ref, zs_ref, gnorm_ref,
                og_ref, state_ref, s_scr):
    rows = qs_ref.shape[0]
    c_ = GDN_CHUNK
    t = pl.program_id(1)

    @pl.when(t == 0)
    def _():
        s_scr[...] = jnp.zeros(s_scr.shape, F32)

    ri = lax.broadcasted_iota(jnp.int32, (c_, c_), 0)
    ci = lax.broadcasted_iota(jnp.int32, (c_, c_), 1)
    incl = ri >= ci
    strict = ri > ci
    gnorm = gnorm_ref[...]

    def prepare(chunks):
        pairs = [(c, hh) for c in chunks for hh in range(HEADS)]
        cols = {c: col_ref[c * c_:(c + 1) * c_, :] for c in chunks}
        rws = {c: row_ref[c] for c in chunks}
        beta = [cols[c][:, hh:hh + 1] for c, hh in pairs]
        gc = [cols[c][:, HEADS + hh:HEADS + hh + 1] for c, hh in pairs]
        gr = [rws[c][HEADS + hh:HEADS + hh + 1, :] for c, hh in pairs]
        glast = [cols[c][c_ - 1:c_, HEADS + hh:HEADS + hh + 1] for c, hh in pairs]
        kh = [k_ref[c * c_:(c + 1) * c_, hh * DH:(hh + 1) * DH] for c, hh in pairs]
        qh = [qs_ref[c * c_:(c + 1) * c_, hh * DH:(hh + 1) * DH] for c, hh in pairs]
        kf = [x.astype(F32) for x in kh]
        kb = [x * b for x, b in zip(kf, beta)]
        m1 = [_dot_nt(jnp.concatenate([x.astype(BF16), q], axis=0), k) for x, q, k in zip(kb, qh, kh)]
        yield
        decay = [jnp.exp(jnp.where(incl, a - b, -jnp.inf)) for a, b in zip(gc, gr)]
        attn = [(m[c_:] * d).astype(BF16) for m, d in zip(m1, decay)]
        p = yield from _neumann_many([jnp.where(strict, m[:c_] * d, 0.0) for m, d in zip(m1, decay)])
        eg = [jnp.exp(x) for x in gc]
        rhs = [jnp.concatenate([x * e, v_ref[c * c_:(c + 1) * c_, hh * DH:(hh + 1) * DH].astype(F32) * b], axis=1)
               for x, e, b, (c, hh) in zip(kb, eg, beta, pairs)]
        wu = [(x + _dot(pi.astype(BF16), x.astype(BF16))).astype(BF16) for x, pi in zip(rhs, p)]
        yield
        ke_t = [jnp.transpose((x * jnp.exp(gl - g)).astype(BF16)) for x, gl, g in zip(kf, glast, gc)]
        r1 = [_dot(jnp.concatenate([kt, a], axis=0), x) for kt, a, x in zip(ke_t, attn, wu)]
        yield
        staged = {}
        for i, pair in enumerate(pairs):
            q_eff = qh[i].astype(F32) * eg[i] - r1[i][DH:, :DH]
            mq = jnp.concatenate([(-r1[i][:DH, :DH]).astype(BF16), q_eff.astype(BF16)], axis=0)
            staged[pair] = (mq, r1[i][:DH, DH:], r1[i][DH:, DH:], jnp.exp(glast[i]))
        return staged

    state = [s_scr[hh] for hh in range(HEADS)]

    def advance(c, staged):
        r0 = c * c_
        for hh in range(HEADS):
            lo = hh * DH
            mq, n_mat, o_zero, e_last = staged[c, hh]
            r = _dot(mq, state[hh].astype(BF16))
            state[hh] = state[hh] * e_last + r[:DH] + n_mat
            og = _rms(r[DH:] + o_zero, gnorm) * zs_ref[r0:r0 + c_, lo:lo + DH].astype(F32)
            og_ref[r0:r0 + c_, lo:lo + DH] = og.astype(BF16)

    nchunk = rows // c_
    fillers = []
    for g0 in range(0, nchunk, GDN_GROUP):
        group = list(range(g0, g0 + GDN_GROUP))
        staged = _interleave(prepare(group), fillers)
        fillers = [functools.partial(advance, c, staged) for c in group]
    for f in fillers:
        f()
    for hh in range(HEADS):
        s_scr[hh] = state[hh]

    @pl.when(t == pl.num_programs(1) - 1)
    def _():
        state_ref[...] = s_scr[...]


def _gdn_call(qs, k, v, col, row, zs, gnorm):
    b, l, _ = qs.shape
    rows = GDN_ROWS
    nt = l // rows
    nchunk = rows // GDN_CHUNK
    tile = lambda w: pl.BlockSpec((None, rows, w), lambda i, j: (i, j, 0))
    return pl.pallas_call(
        _gdn_kernel,
        grid=(b, nt),
        in_specs=[tile(D_GDN)] * 3 + [
            tile(LANES),
            pl.BlockSpec((None, nchunk, 2 * HEADS, GDN_CHUNK), lambda i, j: (i, j, 0, 0)),
            tile(D_GDN),
            _resident(gnorm.shape),
        ],
        out_specs=[tile(D_GDN), pl.BlockSpec((None, HEADS, DH, DH), lambda i, j: (i, 0, 0, 0))],
        out_shape=[jax.ShapeDtypeStruct((b, l, D_GDN), BF16),
                   jax.ShapeDtypeStruct((b, HEADS, DH, DH), F32)],
        scratch_shapes=[pltpu.VMEM((HEADS, DH, DH), F32)],
        compiler_params=pltpu.CompilerParams(
            dimension_semantics=("arbitrary", "arbitrary"), vmem_limit_bytes=VMEM_LIMIT),
        name="gdn",
    )(qs, k, v, col, row, zs, gnorm)


def _tail_kernel(x_ref, og_ref, sg_ref, p_ref, xs_ref, sgs_ref, ps_ref,
                 s_ref, qt_ref, kt_ref, v_ref, col_ref, zs_ref, gnorm_ref,
                 wout_ref, gff_ref, wup_ref, wdown_ref, gple_ref, wple_ref, wgate_ref, gfin_ref,
                 y_ref, ys_ref, snew_ref, ogs_scr):
    def rows_through_tail(x_ref, og, sg_ref, p_ref, y_ref, side_job=None):
        x1 = x_ref[...] + _dot(og, wout_ref[0:D_GDN, :]) + _dot(sg_ref[...], wout_ref[D_GDN:, :])
        h2 = _rms(x1, gff_ref[...]).astype(BF16)
        y_ref[...] = x1
        ff_block = D_FF // 4
        for c in range(D_FF // ff_block):
            lo = c * ff_block
            hid = jnp.maximum(_dot(h2, wup_ref[:, lo:lo + ff_block]), 0.0)
            y_ref[...] += _dot((hid * hid).astype(BF16), wdown_ref[lo:lo + ff_block, :])
            if c == 0 and side_job is not None:
                side_job()
        acc = y_ref[...]
        h3 = _rms(acc, gple_ref[...]).astype(BF16)
        gate = _sigmoid(_dot(h3, wgate_ref[...]))
        x3 = acc + _dot(p_ref[...].astype(BF16), wple_ref[...]) * gate
        y_ref[...] = _rms(x3, gfin_ref[...])

    step = pl.program_id(0)
    last = pl.num_programs(0) - 1
    assert SAMPLE_BLOCK == 2 * SAMPLE_HALF

    @pl.when(step == 0)
    def _():
        ogs_scr[...] = jnp.zeros(ogs_scr.shape, ogs_scr.dtype)

    @pl.when(step < last)
    def _():
        first_row = pl.multiple_of((step // 2) * SAMPLE_BLOCK, SAMPLE_BLOCK)
        og_rows = ogs_scr.at[pl.ds(first_row, SAMPLE_BLOCK), :]
        side_job = functools.partial(_sample_delta_step, step % 2 == 1, step * SAMPLE_HALF, s_ref, qt_ref, kt_ref,
                                     v_ref, col_ref, zs_ref, gnorm_ref, og_rows, snew_ref)
        rows_through_tail(x_ref, og_ref[...], sg_ref, p_ref, y_ref, side_job)

    @pl.when(step == last)
    def _():
        rows_through_tail(xs_ref, ogs_scr[...].astype(BF16), sgs_ref, ps_ref, ys_ref)


def _tail_call(x, og, sg, p, xs, sgs, ps, state, q, kt, v, col, zs, gnorm,
               wout, gff, wup, wdown, gple, wple, wgate, gfin):
    n = x.shape[0]
    ns = xs.shape[0]
    rows = TAIL_ROWS
    steps = n // rows
    blk = SAMPLE_BLOCK
    per_block = SAMPLE_BLOCK // SAMPLE_HALF
    assert ns == steps * SAMPLE_HALF
    prompt_step = lambda i: jnp.minimum(i, steps - 1)
    tile = lambda w: pl.BlockSpec((rows, w), lambda i: (prompt_step(i), 0))
    srow = lambda w: pl.BlockSpec((blk, w), lambda i: (prompt_step(i) // per_block, 0))
    sample = [xs, sgs, ps]
    weights = [wout, gff, wup, wdown, gple, wple, wgate, gfin]
    state_block = pl.BlockSpec((blk, HEADS, DH, DH), lambda i: (prompt_step(i) // per_block, 0, 0, 0))
    cols4 = _resident(q.shape)
    return pl.pallas_call(
        _tail_kernel,
        grid=(steps + 1,),
        in_specs=[tile(D_MODEL), tile(D_GDN), tile(D_SGU), tile(D_PLE)]
        + [_resident(a.shape) for a in sample]
        + [state_block, cols4, cols4, srow(D_GDN), srow(LANES), srow(D_GDN), _resident(gnorm.shape)]
        + [_resident(a.shape) for a in weights],
        out_specs=[tile(D_MODEL), pl.BlockSpec(xs.shape, lambda i: (0, 0)), state_block],
        out_shape=[jax.ShapeDtypeStruct((n, D_MODEL), F32), jax.ShapeDtypeStruct(xs.shape, F32),
                   jax.ShapeDtypeStruct(state.shape, F32)],
        scratch_shapes=[pltpu.VMEM((ns, D_GDN), F32)],
        compiler_params=pltpu.CompilerParams(
            dimension_semantics=("arbitrary",), vmem_limit_bytes=VMEM_LIMIT),
        name="tail",
    )(x, og, sg, p, *sample, state, q, kt, v, col, zs, gnorm, *weights)


def _sample_proj_kernel(x_ref, hist_ref, gmix_ref, w_ref, wconv_ref,
                        alog_row_ref, dtb_row_ref, lng_ref, lnb_ref, ws_ref, bs_ref,
                        nconv_ref, vg_ref, sg_ref, zs_ref, v_ref, col_ref, qt_ref, kt_ref):
    n = x_ref.shape[0]
    h = _rms(x_ref[...], gmix_ref[...]).astype(BF16)
    pre = _dot(h, w_ref[:, COL_Q:COL_Q + D_QKV])
    y = pre * wconv_ref[CONV_W - 1:CONV_W, :]
    for j in range(CONV_W - 1):
        y = y + hist_ref[j] * wconv_ref[j:j + 1, :]
        if j > 0:
            nconv_ref[j - 1] = hist_ref[j]
    nconv_ref[CONV_W - 2] = pre
    y = _silu(y)
    v_ref[...] = y[:, 2 * D_GDN:]
    for hh in range(HEADS):
        lo = hh * DH
        qt_ref[hh] = jnp.transpose(_l2n(y[:, lo:lo + DH]) * (DH ** -0.5))
        kt_ref[hh] = jnp.transpose(_l2n(y[:, D_GDN + lo:D_GDN + lo + DH]))
    zs_ref[...] = _silu(_dot(h, w_ref[:, COL_Z:COL_Z + D_GDN]))
    ab = _dot(h, w_ref[:, COL_AB:COL_AB + LANES])
    lane = lax.broadcasted_iota(jnp.int32, ab.shape, 1)
    g_col = -jnp.exp(alog_row_ref[...]) * _softplus(ab + dtb_row_ref[...])
    col_ref[...] = jnp.where(lane < HEADS, _sigmoid(ab), g_col)
    uv = _gelu_tanh(_dot(h, w_ref[:, COL_U:COL_U + 2 * D_SGU]))
    vg = _layer_norm(uv[:, D_SGU:], lng_ref[...], lnb_ref[...])
    vg_ref[...] = vg
    for hh in range(HEADS):
        lo = hh * DH
        mix = ws_ref[hh, 0:1, 0:1] * vg[:, lo:lo + DH] + bs_ref[hh:hh + 1, 0:1]
        sg_ref[:, lo:lo + DH] = (uv[:, lo:lo + DH] * mix).astype(BF16)


def _sample_proj_call(x, hist, gmix, w_all, wconv, alog_row, dtb_row, lng, lnb, ws, bs):
    n = x.shape[0]
    out_shape = [
        jax.ShapeDtypeStruct((CONV_W - 1, n, D_QKV), F32),
        jax.ShapeDtypeStruct((n, D_SGU), F32),
        jax.ShapeDtypeStruct((n, D_SGU), BF16),
        jax.ShapeDtypeStruct((n, D_GDN), F32),
        jax.ShapeDtypeStruct((n, D_GDN), F32),
        jax.ShapeDtypeStruct((n, LANES), F32),
        jax.ShapeDtypeStruct((HEADS, DH, n), F32),
        jax.ShapeDtypeStruct((HEADS, DH, n), F32),
    ]
    return pl.pallas_call(
        _sample_proj_kernel,
        out_shape=out_shape,
        compiler_params=pltpu.CompilerParams(vmem_limit_bytes=VMEM_LIMIT),
        name="sample_proj",
    )(x, hist, gmix, w_all, wconv, alog_row, dtb_row, lng, lnb, ws, bs)


def _sample_delta_step(upper, lane0, s_ref, qt_ref, kt_ref, v_ref, col_ref, zs_ref, gnorm_ref, og_ref, snew_ref):
    gnorm = gnorm_ref[...]
    row0 = jnp.where(upper, SAMPLE_HALF, 0)
    pick = lambda blk, bb: jnp.where(upper, blk[SAMPLE_HALF + bb:SAMPLE_HALF + bb + 1], blk[bb:bb + 1])
    col_blk, v_blk, zs_blk = col_ref[...], v_ref[...], zs_ref[...]
    pairs = [(bb, hh) for bb in range(SAMPLE_HALF) for hh in range(HEADS)]
    n_seq = kt_ref.shape[-1]
    k_here = [pltpu.roll(kt_ref[hh], n_seq - lane0, axis=1) for hh in range(HEADS)]
    q_here = [pltpu.roll(qt_ref[hh], n_seq - lane0, axis=1) for hh in range(HEADS)]
    kcol = [k_here[hh][:, bb:bb + 1] for bb, hh in pairs]
    qcol = [q_here[hh][:, bb:bb + 1] for bb, hh in pairs]
    beta = [pick(col_blk[:, hh:hh + 1], bb) for bb, hh in pairs]
    eg = [jnp.exp(pick(col_blk[:, HEADS + hh:HEADS + hh + 1], bb)) for bb, hh in pairs]
    sd = [s_ref[row0 + bb, hh] * e for (bb, hh), e in zip(pairs, eg)]
    sk = [jnp.sum(s * kc, axis=0, keepdims=True) for s, kc in zip(sd, kcol)]
    v_new = [b * (pick(v_blk[:, hh * DH:(hh + 1) * DH], bb) - x) for (bb, hh), b, x in zip(pairs, beta, sk)]
    s1 = [s + kc * vn for s, kc, vn in zip(sd, kcol, v_new)]
    for (bb, hh), s in zip(pairs, s1):
        snew_ref[row0 + bb, hh] = s
    o = [jnp.sum(s * qc, axis=0, keepdims=True) for s, qc in zip(s1, qcol)]
    sub = lax.broadcasted_iota(jnp.int32, (SAMPLE_BLOCK, DH), 0)
    for hh in range(HEADS):
        lo = hh * DH
        merged = og_ref[:, lo:lo + DH]
        for bb in range(SAMPLE_HALF):
            og = _rms(o[bb * HEADS + hh], gnorm) * pick(zs_blk[:, lo:lo + DH], bb)
            merged = jnp.where(sub == row0 + bb, og.astype(og_ref.dtype), merged)
        og_ref[:, lo:lo + DH] = merged


def kernel(x_prompt, x_sample, state_conv, state_gdn, p_prompt, p_sample, g_mix, w_in, w_conv, a_log, dt_bias, gdn_norm, sgu_ln_g, sgu_ln_b, w_s, b_s, w_out, g_ff, w_up, w_down, g_ple, w_ple, w_ple_gate, g_final):
    depth = w_in.shape[0]
    assert depth == 1, "single-layer trunk only"
    b, l, d = x_prompt.shape
    n = x_sample.shape[0]
    assert x_sample.shape[1] == 1 and d == D_MODEL
    assert l % PROJ_ROWS == 0 and l % GDN_ROWS == 0 and (b * l) % TAIL_ROWS == 0 and n % SAMPLE_BLOCK == 0

    w_all, wabt = _prep_call(w_in)
    row2 = lambda a: a.reshape(1, -1)
    pad_lanes = lambda a: jnp.pad(a, (HEADS, LANES - 2 * HEADS)).reshape(1, LANES)
    pad_rows = lambda a: jnp.pad(a, (HEADS, 0)).reshape(2 * HEADS, 1)
    alog_row, dtb_row = pad_lanes(a_log[0]), pad_lanes(dt_bias[0])
    alog_col, dtb_col = pad_rows(a_log[0]), pad_rows(dt_bias[0])
    gmix, gff, gple, gfin = row2(g_mix[0]), row2(g_ff[0]), row2(g_ple[0]), row2(g_final)
    lng, lnb, gnorm = row2(sgu_ln_g[0]), row2(sgu_ln_b[0]), row2(gdn_norm[0])

    qs, k, v, zs, sg, col, row, conv_p, wout, wup, wdown, wple, wgate = _proj_call(
        x_prompt, gmix, w_all, wabt, w_conv[0], alog_row, dtb_row, alog_col, dtb_col,
        lng, lnb, w_s[0], b_s[0].T, (w_out, w_up, w_down, w_ple, w_ple_gate))
    tail_w = (wout, gff, wup, wdown, gple, wple, wgate, gfin)
    og, gdn_p = _gdn_call(qs, k, v, col, row, zs, gnorm)

    nconv, vg_s, sg_s, zs_s, v_s, col_s, q_s, kt = _sample_proj_call(
        x_sample.reshape(n, d), jnp.transpose(state_conv[0], (1, 0, 2)), gmix, w_all,
        w_conv[0], alog_row, dtb_row, lng, lnb, w_s[0], b_s[0])

    y_prompt, y_sample, gdn_s = _tail_call(
        x_prompt.reshape(b * l, d), og.reshape(b * l, D_GDN), sg.reshape(b * l, D_SGU),
        p_prompt[0].reshape(b * l, D_PLE),
        x_sample.reshape(n, d), sg_s, p_sample[0].reshape(n, D_PLE),
        state_gdn[0], q_s, kt, v_s, col_s, zs_s, gnorm, *tail_w)
    y_prompt, y_sample = y_prompt.reshape(b, l, d), y_sample.reshape(n, 1, d)

    return (y_prompt, y_sample,
            conv_p[None], gdn_p[None],
            jnp.transpose(nconv, (1, 0, 2))[None], gdn_s[None],
            vg_s.reshape(n, 1, D_SGU)[None])
```

```python
import functools

import jax
import jax.numpy as jnp
from jax import lax
from jax.experimental import pallas as pl
from jax.experimental.pallas import tpu as pltpu

F32 = jnp.float32
BF16 = jnp.bfloat16

D_MODEL = 1024
D_GDN = 512
D_SGU = 512
HEADS = 4
DH = 128
D_QKV = 3 * D_GDN
D_FF = 4 * D_MODEL
D_PLE = 256
CONV_W = 4
GDN_CHUNK = 64
SGU_CHUNK = 128
EPS = 1e-6
LANES = 128
SUBLANES = 8
VMEM_LIMIT = 56 * 1024 * 1024

COL_Q, COL_K, COL_V, COL_Z, COL_U, COL_VV, COL_AB = 0, 512, 1024, 1536, 2048, 2560, 3072
W_COLS = COL_AB + LANES

PROJ_ROWS = 1024
CONV_BLOCK = 8
GDN_ROWS = 2048
GDN_GROUP = 4
TAIL_ROWS = 512
SAMPLE_BLOCK = 8
SAMPLE_HALF = 4
PREP_STEPS = 8


def _dot(a, b):
    return jnp.dot(a, b, preferred_element_type=F32)


def _dot_nt(a, b):
    return lax.dot_general(a, b, (((1,), (1,)), ((), ())), preferred_element_type=F32)


def _dot_tn(a, b):
    return lax.dot_general(a, b, (((0,), (0,)), ((), ())), preferred_element_type=F32)


def _split3(x):
    hi = x.astype(BF16)
    r1 = x - hi.astype(F32)
    mid = r1.astype(BF16)
    lo = (r1 - mid.astype(F32)).astype(BF16)
    return hi, mid, lo


def _rms(x, gain):
    return x * lax.rsqrt(jnp.mean(x * x, axis=-1, keepdims=True) + EPS) * gain


def _sigmoid(x):
    return 1.0 / (1.0 + jnp.exp(-x))


def _silu(x):
    hx = 0.5 * x
    return hx + hx * jnp.tanh(hx)


def _softplus(x):
    return jnp.maximum(x, 0.0) + jnp.log(1.0 + jnp.exp(-jnp.abs(x)))


def _gelu_tanh(x):
    c = 0.7978845608028654
    hx = 0.5 * x
    return hx + hx * jnp.tanh(x * (c + (c * 0.044715) * (x * x)))


def _l2n(x):
    return x * lax.rsqrt(jnp.sum(x * x, axis=-1, keepdims=True) + 1e-6)


def _layer_norm(x, gain, bias):
    mu = jnp.mean(x, axis=-1, keepdims=True)
    xc = x - mu
    var = jnp.mean(xc * xc, axis=-1, keepdims=True)
    return xc * lax.rsqrt(var + EPS) * gain + bias


def _resident(shape):
    nd = len(shape)
    return pl.BlockSpec(shape, lambda *_: (0,) * nd, pipeline_mode=pl.Buffered(1))


def _prep_kernel(win_ref, wall_o, wabt_o):
    o1 = D_QKV + D_GDN
    o3 = o1 + 2 * HEADS

    def put(dst, src):
        wall_o[:, dst:dst + LANES] = jnp.transpose(win_ref[src:src + LANES, :]).astype(BF16)

    for j in range(o1 // LANES):
        put(j * LANES, j * LANES)
    for j in range(2 * D_SGU // LANES):
        put(COL_U + j * LANES, o3 + j * LANES)
    ab = win_ref[o1:o3, :]
    ab16 = jnp.concatenate([ab, jnp.zeros((2 * SUBLANES - 2 * HEADS, LANES), F32)], axis=0)
    wabt_o[...] = ab16.astype(BF16)
    ab128 = jnp.concatenate([ab, jnp.zeros((LANES - 2 * HEADS, LANES), F32)], axis=0)
    wall_o[:, COL_AB:] = jnp.transpose(ab128).astype(BF16)


def _prep_call(w_in):
    steps = PREP_STEPS
    d, d_in = w_in.shape[1], w_in.shape[2]
    assert d // steps == LANES
    win_t = jnp.transpose(w_in[0])
    return pl.pallas_call(
        _prep_kernel,
        grid=(steps,),
        in_specs=[pl.BlockSpec((d_in, LANES), lambda i: (0, i))],
        out_specs=[pl.BlockSpec((LANES, W_COLS), lambda i: (i, 0)),
                   pl.BlockSpec((2 * SUBLANES, LANES), lambda i: (0, i))],
        out_shape=[jax.ShapeDtypeStruct((d, W_COLS), BF16), jax.ShapeDtypeStruct((2 * SUBLANES, d), BF16)],
        compiler_params=pltpu.CompilerParams(
            dimension_semantics=("arbitrary",), vmem_limit_bytes=VMEM_LIMIT),
        name="prep",
    )(win_t)


def _proj_kernel(x_ref, gmix_ref, w_ref, wabt_ref, wconv_ref,
                 alog_row_ref, dtb_row_ref, alog_col_ref, dtb_col_ref, lng_ref, lnb_ref,
                 ws_ref, bst_ref, wout_ref, wup_ref, wdown_ref, wple_ref, wgate_ref,
                 qs_ref, k_ref, v_ref, zs_ref, sg_ref, col_ref, row_ref, cstate_ref,
                 wout_o, wup_o, wdown_o, wple_o, wgate_o,
                 cbuf):
    rows = x_ref.shape[0]
    first_tile = pl.program_id(1) == 0
    wout_o[...] = wout_ref[...].astype(BF16)
    wup_o[...] = wup_ref[...].astype(BF16)
    wdown_o[...] = wdown_ref[...].astype(BF16)
    wple_o[...] = wple_ref[...].astype(BF16)
    wgate_o[...] = wgate_ref[...].astype(BF16)
    h = _rms(x_ref[...], gmix_ref[...]).astype(BF16)

    def project(lo, width=D_GDN):
        return _dot(h, w_ref[:, lo:lo + width])

    def conv_silu(pre, lo):
        cols = slice(lo, lo + D_GDN)
        groups = rows // SUBLANES
        tap = lambda j: wconv_ref[j:j + 1, cols].reshape(1, 1, D_GDN)
        shape3 = (1, SUBLANES, D_GDN)
        last_p = jnp.where(first_tile, 0.0, cbuf[0:SUBLANES, cols]).reshape(shape3)
        last_b = jnp.where(first_tile, 0.0, cbuf[SUBLANES:, cols]).reshape(shape3)
        cstate_ref[:, cols] = pre[rows - (CONV_W - 1):rows, :]
        cur = pre.reshape(groups, SUBLANES, D_GDN)
        sub = lax.broadcasted_iota(jnp.int32, (CONV_BLOCK, SUBLANES, D_GDN), 1)

        def delay(a, a_last, s):
            rolled = pltpu.roll(a, s, axis=1)
            rolled_prev = jnp.concatenate([pltpu.roll(a_last, s, axis=1), rolled[:-1]], axis=0)
            return jnp.where(sub >= s, rolled, rolled_prev)

        assert CONV_W == 4
        out = []
        for g0 in range(0, groups, CONV_BLOCK):
            p = cur[g0:g0 + CONV_BLOCK]
            dp = delay(p, last_p, 1)
            b = p * tap(1) + dp * tap(0)
            y = p * tap(3) + dp * tap(2) + delay(b, last_b, 2)
            last_p, last_b = p[CONV_BLOCK - 1:], b[CONV_BLOCK - 1:]
            out.append(_silu(y))
        cbuf[0:SUBLANES, cols] = last_p.reshape(SUBLANES, D_GDN)
        cbuf[SUBLANES:, cols] = last_b.reshape(SUBLANES, D_GDN)
        return jnp.concatenate(out, axis=0).reshape(rows, D_GDN)

    pre_vv = project(COL_VV)
    pre_u = project(COL_U)
    vg = _layer_norm(_gelu_tanh(pre_vv), lng_ref[...], lnb_ref[...]).astype(BF16)
    ab = project(COL_AB, LANES)
    pre_q = project(COL_Q)
    u_act = _gelu_tanh(pre_u)
    ti = lax.broadcasted_iota(jnp.int32, (SGU_CHUNK, SGU_CHUNK), 0)
    si = lax.broadcasted_iota(jnp.int32, (SGU_CHUNK, SGU_CHUNK), 1)
    for hh in range(HEADS):
        lo = hh * DH
        w_tril = jnp.where(ti >= si, ws_ref[hh], 0.0).astype(BF16)
        bias = bst_ref[:, hh:hh + 1]
        for c in range(rows // SGU_CHUNK):
            r0 = c * SGU_CHUNK
            mix = _dot(w_tril, vg[r0:r0 + SGU_CHUNK, lo:lo + DH]) + bias
            sg_ref[r0:r0 + SGU_CHUNK, lo:lo + DH] = (u_act[r0:r0 + SGU_CHUNK, lo:lo + DH] * mix).astype(BF16)
    pre_k = project(COL_K)

    lane = lax.broadcasted_iota(jnp.int32, (GDN_CHUNK, LANES), 1)
    g_col = -jnp.exp(alog_row_ref[...]) * _softplus(ab + dtb_row_ref[...])
    abt = _dot_nt(wabt_ref[...], h)[0:2 * HEADS, :]
    sub = lax.broadcasted_iota(jnp.int32, (2 * HEADS, GDN_CHUNK), 0)
    g_row = -jnp.exp(alog_col_ref[...]) * _softplus(abt + dtb_col_ref[...])
    ri = lax.broadcasted_iota(jnp.int32, (GDN_CHUNK, GDN_CHUNK), 0)
    ci = lax.broadcasted_iota(jnp.int32, (GDN_CHUNK, GDN_CHUNK), 1)
    tri_l = (ri >= ci).astype(BF16)
    tri_u = (ri <= ci).astype(BF16)
    beta_col = _sigmoid(ab)
    beta_row = _sigmoid(abt)
    for c in range(rows // GDN_CHUNK):
        r0 = c * GDN_CHUNK
        p0, p1, p2 = _split3(g_col[r0:r0 + GDN_CHUNK, :])
        gc = _dot(tri_l, p0) + _dot(tri_l, p1) + _dot(tri_l, p2)
        col_ref[r0:r0 + GDN_CHUNK, :] = jnp.where(lane < HEADS, beta_col[r0:r0 + GDN_CHUNK, :], gc)
        p0, p1, p2 = _split3(g_row[:, r0:r0 + GDN_CHUNK])
        gr = _dot(p0, tri_u) + _dot(p1, tri_u) + _dot(p2, tri_u)
        row_ref[c] = jnp.where(sub < HEADS, beta_row[:, r0:r0 + GDN_CHUNK], gr)

    y = conv_silu(pre_q, COL_Q)
    for hh in range(HEADS):
        lo = hh * DH
        qs_ref[:, lo:lo + DH] = (_l2n(y[:, lo:lo + DH]) * (DH ** -0.5)).astype(BF16)
    pre_v = project(COL_V)
    y = conv_silu(pre_k, COL_K)
    for hh in range(HEADS):
        lo = hh * DH
        k_ref[:, lo:lo + DH] = _l2n(y[:, lo:lo + DH]).astype(BF16)
    pre_z = project(COL_Z)
    v_ref[...] = conv_silu(pre_v, COL_V).astype(BF16)
    zs_ref[...] = _silu(pre_z).astype(BF16)


def _proj_call(x, gmix, w_all, wabt, wconv, alog_row, dtb_row, alog_col, dtb_col, lng, lnb, ws, bst, tail_f32):
    b, l, _ = x.shape
    rows = PROJ_ROWS
    nt = l // rows
    steps = b * nt
    nchunk = rows // GDN_CHUNK
    tile = lambda w: pl.BlockSpec((None, rows, w), lambda i, j: (i, j, 0))
    ins = [x, gmix, w_all, wabt, wconv, alog_row, dtb_row, alog_col, dtb_col, lng, lnb, ws, bst]
    in_specs = [tile(D_MODEL)] + [_resident(a.shape) for a in ins[1:]]
    in_specs += [pl.BlockSpec((None, w.shape[1] // steps, w.shape[2]), lambda i, j: (0, i * nt + j, 0))
                 for w in tail_f32]
    out_shape = [jax.ShapeDtypeStruct((b, l, D_GDN), BF16)] * 5 + [
        jax.ShapeDtypeStruct((b, l, LANES), F32),
        jax.ShapeDtypeStruct((b, l // GDN_CHUNK, 2 * HEADS, GDN_CHUNK), F32),
        jax.ShapeDtypeStruct((b, CONV_W - 1, D_QKV), F32),
    ] + [jax.ShapeDtypeStruct(w.shape[1:], BF16) for w in tail_f32]
    out_specs = [tile(D_GDN)] * 5 + [
        tile(LANES),
        pl.BlockSpec((None, nchunk, 2 * HEADS, GDN_CHUNK), lambda i, j: (i, j, 0, 0)),
        pl.BlockSpec((None, CONV_W - 1, D_QKV), lambda i, j: (i, 0, 0)),
    ] + [pl.BlockSpec((w.shape[1] // steps, w.shape[2]), lambda i, j: (i * nt + j, 0)) for w in tail_f32]
    ins = ins + list(tail_f32)
    return pl.pallas_call(
        _proj_kernel,
        grid=(b, nt),
        in_specs=in_specs,
        out_specs=out_specs,
        out_shape=out_shape,
        scratch_shapes=[pltpu.VMEM((2 * SUBLANES, D_QKV), F32)],
        compiler_params=pltpu.CompilerParams(
            dimension_semantics=("arbitrary", "arbitrary"), vmem_limit_bytes=VMEM_LIMIT),
        name="proj",
    )(*ins)


def _neumann_many(a_list):
    c = a_list[0].shape[0]
    p = [-a for a in a_list]
    pb = [x.astype(BF16) for x in p]
    q = [_dot(x, x) for x in pb]
    yield
    power = 2
    while 2 * power < c:
        qb = [x.astype(BF16) for x in q]
        r = [_dot(jnp.concatenate([pi.astype(BF16), qi], axis=0), qi) for pi, qi in zip(p, qb)]
        yield
        p = [pi + qi + ri[:c] for pi, qi, ri in zip(p, q, r)]
        q = [ri[c:] for ri in r]
        power *= 2
    tail = [_dot(pi.astype(BF16), qi.astype(BF16)) for pi, qi in zip(p, q)]
    yield
    return [pi + qi + ti for pi, qi, ti in zip(p, q, tail)]


def _interleave(steps, fillers):
    fillers = list(fillers)
    count = 0
    while True:
        try:
            next(steps)
        except StopIteration as stop:
            for f in fillers:
                f()
            return stop.value
        count += 1
        if fillers and count % 2 == 0:
            fillers.pop(0)()


def _gdn_kernel(qs_ref, k_ref, v_ref, col_ref, row_ref, zs_ref, gnorm_ref,
                og_ref, state_ref, s_scr):
    rows = qs_ref.shape[0]
    c_ = GDN_CHUNK
    t = pl.program_id(1)

    @pl.when(t == 0)
    def _():
        s_scr[...] = jnp.zeros(s_scr.shape, F32)

    ri = lax.broadcasted_iota(jnp.int32, (c_, c_), 0)
    ci = lax.broadcasted_iota(jnp.int32, (c_, c_), 1)
    incl = ri >= ci
    strict = ri > ci
    gnorm = gnorm_ref[...]

    def prepare(chunks):
        pairs = [(c, hh) for c in chunks for hh in range(HEADS)]
        cols = {c: col_ref[c * c_:(c + 1) * c_, :] for c in chunks}
        rws = {c: row_ref[c] for c in chunks}
        beta = [cols[c][:, hh:hh + 1] for c, hh in pairs]
        gc = [cols[c][:, HEADS + hh:HEADS + hh + 1] for c, hh in pairs]
        gr = [rws[c][HEADS + hh:HEADS + hh + 1, :] for c, hh in pairs]
        glast = [cols[c][c_ - 1:c_, HEADS + hh:HEADS + hh + 1] for c, hh in pairs]
        kh = [k_ref[c * c_:(c + 1) * c_, hh * DH:(hh + 1) * DH] for c, hh in pairs]
        qh = [qs_ref[c * c_:(c + 1) * c_, hh * DH:(hh + 1) * DH] for c, hh in pairs]
        kf = [x.astype(F32) for x in kh]
        kb = [x * b for x, b in zip(kf, beta)]
        m1 = [_dot_nt(jnp.concatenate([x.astype(BF16), q], axis=0), k) for x, q, k in zip(kb, qh, kh)]
        yield
        decay = [jnp.exp(jnp.where(incl, a - b, -jnp.inf)) for a, b in zip(gc, gr)]
        attn = [(m[c_:] * d).astype(BF16) for m, d in zip(m1, decay)]
        p = yield from _neumann_many([jnp.where(strict, m[:c_] * d, 0.0) for m, d in zip(m1, decay)])
        eg = [jnp.exp(x) for x in gc]
        rhs = [jnp.concatenate([x * e, v_ref[c * c_:(c + 1) * c_, hh * DH:(hh + 1) * DH].astype(F32) * b], axis=1)
               for x, e, b, (c, hh) in zip(kb, eg, beta, pairs)]
        wu = [(x + _dot(pi.astype(BF16), x.astype(BF16))).astype(BF16) for x, pi in zip(rhs, p)]
        yield
        ke_t = [jnp.transpose((x * jnp.exp(gl - g)).astype(BF16)) for x, gl, g in zip(kf, glast, gc)]
        r1 = [_dot(jnp.concatenate([kt, a], axis=0), x) for kt, a, x in zip(ke_t, attn, wu)]
        yield
        staged = {}
        for i, pair in enumerate(pairs):
            q_eff = qh[i].astype(F32) * eg[i] - r1[i][DH:, :DH]
            mq = jnp.concatenate([(-r1[i][:DH, :DH]).astype(BF16), q_eff.astype(BF16)], axis=0)
            staged[pair] = (mq, r1[i][:DH, DH:], r1[i][DH:, DH:], jnp.exp(glast[i]))
        return staged

    state = [s_scr[hh] for hh in range(HEADS)]

    def advance(c, staged):
        r0 = c * c_
        for hh in range(HEADS):
            lo = hh * DH
            mq, n_mat, o_zero, e_last = staged[c, hh]
            r = _dot(mq, state[hh].astype(BF16))
            state[hh] = state[hh] * e_last + r[:DH] + n_mat
            og = _rms(r[DH:] + o_zero, gnorm) * zs_ref[r0:r0 + c_, lo:lo + DH].astype(F32)
            og_ref[r0:r0 + c_, lo:lo + DH] = og.astype(BF16)

    nchunk = rows // c_
    fillers = []
    for g0 in range(0, nchunk, GDN_GROUP):
        group = list(range(g0, g0 + GDN_GROUP))
        staged = _interleave(prepare(group), fillers)
        fillers = [functools.partial(advance, c, staged) for c in group]
    for f in fillers:
        f()
    for hh in range(HEADS):
        s_scr[hh] = state[hh]

    @pl.when(t == pl.num_programs(1) - 1)
    def _():
        state_ref[...] = s_scr[...]


def _gdn_call(qs, k, v, col, row, zs, gnorm):
    b, l, _ = qs.shape
    rows = GDN_ROWS
    nt = l // rows
    nchunk = rows // GDN_CHUNK
    tile = lambda w: pl.BlockSpec((None, rows, w), lambda i, j: (i, j, 0))
    return pl.pallas_call(
        _gdn_kernel,
        grid=(b, nt),
        in_specs=[tile(D_GDN)] * 3 + [
            tile(LANES),
            pl.BlockSpec((None, nchunk, 2 * HEADS, GDN_CHUNK), lambda i, j: (i, j, 0, 0)),
            tile(D_GDN),
            _resident(gnorm.shape),
        ],
        out_specs=[tile(D_GDN), pl.BlockSpec((None, HEADS, DH, DH), lambda i, j: (i, 0, 0, 0))],
        out_shape=[jax.ShapeDtypeStruct((b, l, D_GDN), BF16),
                   jax.ShapeDtypeStruct((b, HEADS, DH, DH), F32)],
        scratch_shapes=[pltpu.VMEM((HEADS, DH, DH), F32)],
        compiler_params=pltpu.CompilerParams(
            dimension_semantics=("arbitrary", "arbitrary"), vmem_limit_bytes=VMEM_LIMIT),
        name="gdn",
    )(qs, k, v, col, row, zs, gnorm)


def _tail_kernel(x_ref, og_ref, sg_ref, p_ref, xs_ref, sgs_ref, ps_ref,
                 s_ref, qt_ref, kt_ref, v_ref, col_ref, zs_ref, gnorm_ref,
                 wout_ref, gff_ref, wup_ref, wdown_ref, gple_ref, wple_ref, wgate_ref, gfin_ref,
                 y_ref, ys_ref, snew_ref, ogs_scr):
    def rows_through_tail(x_ref, og, sg_ref, p_ref, y_ref, side_job=None):
        x1 = x_ref[...] + _dot(og, wout_ref[0:D_GDN, :]) + _dot(sg_ref[...], wout_ref[D_GDN:, :])
        h2 = _rms(x1, gff_ref[...]).astype(BF16)
        y_ref[...] = x1
        ff_block = D_FF // 4
        for c in range(D_FF // ff_block):
            lo = c * ff_block
            hid = jnp.maximum(_dot(h2, wup_ref[:, lo:lo + ff_block]), 0.0)
            y_ref[...] += _dot((hid * hid).astype(BF16), wdown_ref[lo:lo + ff_block, :])
            if c == 0 and side_job is not None:
                side_job()
        acc = y_ref[...]
        h3 = _rms(acc, gple_ref[...]).astype(BF16)
        gate = _sigmoid(_dot(h3, wgate_ref[...]))
        x3 = acc + _dot(p_ref[...].astype(BF16), wple_ref[...]) * gate
        y_ref[...] = _rms(x3, gfin_ref[...])

    step = pl.program_id(0)
    last = pl.num_programs(0) - 1
    assert SAMPLE_BLOCK == 2 * SAMPLE_HALF

    @pl.when(step == 0)
    def _():
        ogs_scr[...] = jnp.zeros(ogs_scr.shape, ogs_scr.dtype)

    @pl.when(step < last)
    def _():
        first_row = pl.multiple_of((step // 2) * SAMPLE_BLOCK, SAMPLE_BLOCK)
        og_rows = ogs_scr.at[pl.ds(first_row, SAMPLE_BLOCK), :]
        side_job = functools.partial(_sample_delta_step, step % 2 == 1, step * SAMPLE_HALF, s_ref, qt_ref, kt_ref,
                                     v_ref, col_ref, zs_ref, gnorm_ref, og_rows, snew_ref)
        rows_through_tail(x_ref, og_ref[...], sg_ref, p_ref, y_ref, side_job)

    @pl.when(step == last)
    def _():
        rows_through_tail(xs_ref.at[:, 0, :], ogs_scr[...].astype(BF16), sgs_ref, ps_ref.at[:, 0, :],
                          ys_ref.at[:, 0, :])


def _tail_call(x, og, sg, p, xs, sgs, ps, state, q, kt, v, col, zs, gnorm,
               wout, gff, wup, wdown, gple, wple, wgate, gfin):
    n = x.shape[0]
    ns = xs.shape[0]
    rows = TAIL_ROWS
    steps = n // rows
    blk = SAMPLE_BLOCK
    per_block = SAMPLE_BLOCK // SAMPLE_HALF
    assert ns == steps * SAMPLE_HALF
    prompt_step = lambda i: jnp.minimum(i, steps - 1)
    tile = lambda w: pl.BlockSpec((rows, w), lambda i: (prompt_step(i), 0))
    srow = lambda w: pl.BlockSpec((blk, w), lambda i: (prompt_step(i) // per_block, 0))
    sample = [xs, sgs, ps]
    weights = [wout, gff, wup, wdown, gple, wple, wgate, gfin]
    state_block = pl.BlockSpec((blk, HEADS, DH, DH), lambda i: (prompt_step(i) // per_block, 0, 0, 0))
    cols4 = _resident(q.shape)
    return pl.pallas_call(
        _tail_kernel,
        grid=(steps + 1,),
        in_specs=[tile(D_MODEL), tile(D_GDN), tile(D_SGU), tile(D_PLE)]
        + [_resident(a.shape) for a in sample]
        + [state_block, cols4, cols4, srow(D_GDN), srow(LANES), srow(D_GDN), _resident(gnorm.shape)]
        + [_resident(a.shape) for a in weights],
        out_specs=[tile(D_MODEL), pl.BlockSpec(xs.shape, lambda i: (0, 0, 0)), state_block],
        out_shape=[jax.ShapeDtypeStruct((n, D_MODEL), F32), jax.ShapeDtypeStruct(xs.shape, F32),
                   jax.ShapeDtypeStruct(state.shape, F32)],
        scratch_shapes=[pltpu.VMEM((ns, D_GDN), F32)],
        compiler_params=pltpu.CompilerParams(
            dimension_semantics=("arbitrary",), vmem_limit_bytes=VMEM_LIMIT),
        name="tail",
    )(x, og, sg, p, *sample, state, q, kt, v, col, zs, gnorm, *weights)


def _sample_proj_kernel(x_ref, hist_ref, gmix_ref, w_ref, wconv_ref,
                        alog_row_ref, dtb_row_ref, lng_ref, lnb_ref, ws_ref, bs_ref,
                        nconv_ref, vg_ref, sg_ref, zs_ref, v_ref, col_ref, qt_ref, kt_ref):
    h = _rms(x_ref[:, 0, :], gmix_ref[...]).astype(BF16)
    pre = _dot(h, w_ref[:, COL_Q:COL_Q + D_QKV])
    y = pre * wconv_ref[CONV_W - 1:CONV_W, :]
    for j in range(CONV_W - 1):
        y = y + hist_ref[j] * wconv_ref[j:j + 1, :]
        if j > 0:
            nconv_ref[j - 1] = hist_ref[j]
    nconv_ref[CONV_W - 2] = pre
    y = _silu(y)
    v_ref[...] = y[:, 2 * D_GDN:]
    for hh in range(HEADS):
        lo = hh * DH
        qt_ref[hh] = jnp.transpose(_l2n(y[:, lo:lo + DH]) * (DH ** -0.5))
        kt_ref[hh] = jnp.transpose(_l2n(y[:, D_GDN + lo:D_GDN + lo + DH]))
    zs_ref[...] = _silu(_dot(h, w_ref[:, COL_Z:COL_Z + D_GDN]))
    ab = _dot(h, w_ref[:, COL_AB:COL_AB + LANES])
    lane = lax.broadcasted_iota(jnp.int32, ab.shape, 1)
    g_col = -jnp.exp(alog_row_ref[...]) * _softplus(ab + dtb_row_ref[...])
    col_ref[...] = jnp.where(lane < HEADS, _sigmoid(ab), g_col)
    uv = _gelu_tanh(_dot(h, w_ref[:, COL_U:COL_U + 2 * D_SGU]))
    vg = _layer_norm(uv[:, D_SGU:], lng_ref[...], lnb_ref[...])
    vg_ref[:, 0, :] = vg
    for hh in range(HEADS):
        lo = hh * DH
        mix = ws_ref[hh, 0:1, 0:1] * vg[:, lo:lo + DH] + bs_ref[hh:hh + 1, 0:1]
        sg_ref[:, lo:lo + DH] = (uv[:, lo:lo + DH] * mix).astype(BF16)


def _sample_proj_call(x, hist, gmix, w_all, wconv, alog_row, dtb_row, lng, lnb, ws, bs):
    n = x.shape[0]
    out_shape = [
        jax.ShapeDtypeStruct((CONV_W - 1, n, D_QKV), F32),
        jax.ShapeDtypeStruct((n, 1, D_SGU), F32),
        jax.ShapeDtypeStruct((n, D_SGU), BF16),
        jax.ShapeDtypeStruct((n, D_GDN), F32),
        jax.ShapeDtypeStruct((n, D_GDN), F32),
        jax.ShapeDtypeStruct((n, LANES), F32),
        jax.ShapeDtypeStruct((HEADS, DH, n), F32),
        jax.ShapeDtypeStruct((HEADS, DH, n), F32),
    ]
    return pl.pallas_call(
        _sample_proj_kernel,
        out_shape=out_shape,
        compiler_params=pltpu.CompilerParams(vmem_limit_bytes=VMEM_LIMIT),
        name="sample_proj",
    )(x, hist, gmix, w_all, wconv, alog_row, dtb_row, lng, lnb, ws, bs)


def _sample_delta_step(upper, lane0, s_ref, qt_ref, kt_ref, v_ref, col_ref, zs_ref, gnorm_ref, og_ref, snew_ref):
    gnorm = gnorm_ref[...]
    row0 = jnp.where(upper, SAMPLE_HALF, 0)
    pick = lambda blk, bb: jnp.where(upper, blk[SAMPLE_HALF + bb:SAMPLE_HALF + bb + 1], blk[bb:bb + 1])
    col_blk, v_blk, zs_blk = col_ref[...], v_ref[...], zs_ref[...]
    pairs = [(bb, hh) for bb in range(SAMPLE_HALF) for hh in range(HEADS)]
    n_seq = kt_ref.shape[-1]
    k_here = [pltpu.roll(kt_ref[hh], n_seq - lane0, axis=1) for hh in range(HEADS)]
    q_here = [pltpu.roll(qt_ref[hh], n_seq - lane0, axis=1) for hh in range(HEADS)]
    kcol = [k_here[hh][:, bb:bb + 1] for bb, hh in pairs]
    qcol = [q_here[hh][:, bb:bb + 1] for bb, hh in pairs]
    beta = [pick(col_blk[:, hh:hh + 1], bb) for bb, hh in pairs]
    eg = [jnp.exp(pick(col_blk[:, HEADS + hh:HEADS + hh + 1], bb)) for bb, hh in pairs]
    sd = [s_ref[row0 + bb, hh] * e for (bb, hh), e in zip(pairs, eg)]
    sk = [jnp.sum(s * kc, axis=0, keepdims=True) for s, kc in zip(sd, kcol)]
    v_new = [b * (pick(v_blk[:, hh * DH:(hh + 1) * DH], bb) - x) for (bb, hh), b, x in zip(pairs, beta, sk)]
    s1 = [s + kc * vn for s, kc, vn in zip(sd, kcol, v_new)]
    for (bb, hh), s in zip(pairs, s1):
        snew_ref[row0 + bb, hh] = s
    o = [jnp.sum(s * qc, axis=0, keepdims=True) for s, qc in zip(s1, qcol)]
    sub = lax.broadcasted_iota(jnp.int32, (SAMPLE_BLOCK, DH), 0)
    for hh in range(HEADS):
        lo = hh * DH
        merged = og_ref[:, lo:lo + DH]
        for bb in range(SAMPLE_HALF):
            og = _rms(o[bb * HEADS + hh], gnorm) * pick(zs_blk[:, lo:lo + DH], bb)
            merged = jnp.where(sub == row0 + bb, og.astype(og_ref.dtype), merged)
        og_ref[:, lo:lo + DH] = merged


def kernel(x_prompt, x_sample, state_conv, state_gdn, p_prompt, p_sample, g_mix, w_in, w_conv, a_log, dt_bias, gdn_norm, sgu_ln_g, sgu_ln_b, w_s, b_s, w_out, g_ff, w_up, w_down, g_ple, w_ple, w_ple_gate, g_final):
    depth = w_in.shape[0]
    assert depth == 1, "single-layer trunk only"
    b, l, d = x_prompt.shape
    n = x_sample.shape[0]
    assert x_sample.shape[1] == 1 and d == D_MODEL
    assert l % PROJ_ROWS == 0 and l % GDN_ROWS == 0 and (b * l) % TAIL_ROWS == 0 and n % SAMPLE_BLOCK == 0

    w_all, wabt = _prep_call(w_in)
    row2 = lambda a: a.reshape(1, -1)
    pad_lanes = lambda a: jnp.pad(a, (HEADS, LANES - 2 * HEADS)).reshape(1, LANES)
    pad_rows = lambda a: jnp.pad(a, (HEADS, 0)).reshape(2 * HEADS, 1)
    alog_row, dtb_row = pad_lanes(a_log[0]), pad_lanes(dt_bias[0])
    alog_col, dtb_col = pad_rows(a_log[0]), pad_rows(dt_bias[0])
    gmix, gff, gple, gfin = row2(g_mix[0]), row2(g_ff[0]), row2(g_ple[0]), row2(g_final)
    lng, lnb, gnorm = row2(sgu_ln_g[0]), row2(sgu_ln_b[0]), row2(gdn_norm[0])

    qs, k, v, zs, sg, col, row, conv_p, wout, wup, wdown, wple, wgate = _proj_call(
        x_prompt, gmix, w_all, wabt, w_conv[0], alog_row, dtb_row, alog_col, dtb_col,
        lng, lnb, w_s[0], b_s[0].T, (w_out, w_up, w_down, w_ple, w_ple_gate))
    tail_w = (wout, gff, wup, wdown, gple, wple, wgate, gfin)
    og, gdn_p = _gdn_call(qs, k, v, col, row, zs, gnorm)

    nconv, vg_s, sg_s, zs_s, v_s, col_s, q_s, kt = _sample_proj_call(
        x_sample, jnp.transpose(state_conv[0], (1, 0, 2)), gmix, w_all,
        w_conv[0], alog_row, dtb_row, lng, lnb, w_s[0], b_s[0])

    y_prompt, y_sample, gdn_s = _tail_call(
        x_prompt.reshape(b * l, d), og.reshape(b * l, D_GDN), sg.reshape(b * l, D_SGU),
        p_prompt[0].reshape(b * l, D_PLE),
        x_sample, sg_s, p_sample[0],
        state_gdn[0], q_s, kt, v_s, col_s, zs_s, gnorm, *tail_w)
    y_prompt = y_prompt.reshape(b, l, d)

    return (y_prompt, y_sample,
            conv_p[None], gdn_p[None],
            jnp.transpose(nconv, (1, 0, 2))[None], gdn_s[None],
            vg_s[None])
```

```python
import functools

import jax
import jax.numpy as jnp
from jax import lax
from jax.experimental import pallas as pl
from jax.experimental.pallas import tpu as pltpu

F32 = jnp.float32
BF16 = jnp.bfloat16

D_MODEL = 1024
D_GDN = 512
D_SGU = 512
HEADS = 4
DH = 128
D_QKV = 3 * D_GDN
D_FF = 4 * D_MODEL
D_PLE = 256
CONV_W = 4
GDN_CHUNK = 64
SGU_CHUNK = 128
EPS = 1e-6
LANES = 128
SUBLANES = 8
VMEM_LIMIT = 56 * 1024 * 1024

COL_Q, COL_K, COL_V, COL_Z, COL_U, COL_VV, COL_AB = 0, 512, 1024, 1536, 2048, 2560, 3072
W_COLS = COL_AB + LANES

PROJ_ROWS = 1024
CONV_BLOCK = 8
GDN_ROWS = 2048
GDN_GROUP = 4
TAIL_ROWS = 512
SAMPLE_BLOCK = 8
SAMPLE_HALF = 4
PREP_STEPS = 8


def _dot(a, b):
    return jnp.dot(a, b, preferred_element_type=F32)


def _dot_nt(a, b):
    return lax.dot_general(a, b, (((1,), (1,)), ((), ())), preferred_element_type=F32)


def _dot_tn(a, b):
    return lax.dot_general(a, b, (((0,), (0,)), ((), ())), preferred_element_type=F32)


def _split3(x):
    hi = x.astype(BF16)
    r1 = x - hi.astype(F32)
    mid = r1.astype(BF16)
    lo = (r1 - mid.astype(F32)).astype(BF16)
    return hi, mid, lo


def _rms(x, gain):
    return x * lax.rsqrt(jnp.mean(x * x, axis=-1, keepdims=True) + EPS) * gain


def _sigmoid(x):
    return 1.0 / (1.0 + jnp.exp(-x))


def _silu(x):
    hx = 0.5 * x
    return hx + hx * jnp.tanh(hx)


def _softplus(x):
    return jnp.maximum(x, 0.0) + jnp.log(1.0 + jnp.exp(-jnp.abs(x)))


def _gelu_tanh(x):
    c = 0.7978845608028654
    hx = 0.5 * x
    return hx + hx * jnp.tanh(x * (c + (c * 0.044715) * (x * x)))


def _l2n(x):
    return x * lax.rsqrt(jnp.sum(x * x, axis=-1, keepdims=True) + 1e-6)


def _layer_norm(x, gain, bias):
    mu = jnp.mean(x, axis=-1, keepdims=True)
    xc = x - mu
    var = jnp.mean(xc * xc, axis=-1, keepdims=True)
    return xc * lax.rsqrt(var + EPS) * gain + bias


def _decay_slots(scalar_ref, shape, axis):
    idx = lax.broadcasted_iota(jnp.int32, shape, axis)
    out = jnp.zeros(shape, F32)
    for hh in range(HEADS):
        out = jnp.where(idx == HEADS + hh, scalar_ref[hh], out)
    return out


def _resident(shape):
    nd = len(shape)
    return pl.BlockSpec(shape, lambda *_: (0,) * nd, pipeline_mode=pl.Buffered(1))


def _prep_kernel(win_ref, wall_o, wabt_o):
    o1 = D_QKV + D_GDN
    o3 = o1 + 2 * HEADS

    def put(dst, src):
        wall_o[:, dst:dst + LANES] = jnp.transpose(win_ref[src:src + LANES, :]).astype(BF16)

    for j in range(o1 // LANES):
        put(j * LANES, j * LANES)
    for j in range(2 * D_SGU // LANES):
        put(COL_U + j * LANES, o3 + j * LANES)
    ab = win_ref[o1:o3, :]
    ab16 = jnp.concatenate([ab, jnp.zeros((2 * SUBLANES - 2 * HEADS, LANES), F32)], axis=0)
    wabt_o[...] = ab16.astype(BF16)
    ab128 = jnp.concatenate([ab, jnp.zeros((LANES - 2 * HEADS, LANES), F32)], axis=0)
    wall_o[:, COL_AB:] = jnp.transpose(ab128).astype(BF16)


def _prep_call(w_in):
    steps = PREP_STEPS
    d, d_in = w_in.shape[1], w_in.shape[2]
    assert d // steps == LANES
    win_t = jnp.transpose(w_in[0])
    return pl.pallas_call(
        _prep_kernel,
        grid=(steps,),
        in_specs=[pl.BlockSpec((d_in, LANES), lambda i: (0, i))],
        out_specs=[pl.BlockSpec((LANES, W_COLS), lambda i: (i, 0)),
                   pl.BlockSpec((2 * SUBLANES, LANES), lambda i: (0, i))],
        out_shape=[jax.ShapeDtypeStruct((d, W_COLS), BF16), jax.ShapeDtypeStruct((2 * SUBLANES, d), BF16)],
        compiler_params=pltpu.CompilerParams(
            dimension_semantics=("arbitrary",), vmem_limit_bytes=VMEM_LIMIT),
        name="prep",
    )(win_t)


def _proj_kernel(x_ref, gmix_ref, w_ref, wabt_ref, wconv_ref,
                 alog_ref, dtb_ref, lng_ref, lnb_ref,
                 ws_ref, bst_ref, wout_ref, wup_ref, wdown_ref, wple_ref, wgate_ref,
                 qs_ref, k_ref, v_ref, zs_ref, sg_ref, col_ref, row_ref, cstate_ref,
                 wout_o, wup_o, wdown_o, wple_o, wgate_o,
                 cbuf):
    rows = x_ref.shape[0]
    first_tile = pl.program_id(1) == 0
    wout_o[...] = wout_ref[...].astype(BF16)
    wup_o[...] = wup_ref[...].astype(BF16)
    wdown_o[...] = wdown_ref[...].astype(BF16)
    wple_o[...] = wple_ref[...].astype(BF16)
    wgate_o[...] = wgate_ref[...].astype(BF16)
    h = _rms(x_ref[...], gmix_ref[...]).astype(BF16)

    def project(lo, width=D_GDN):
        return _dot(h, w_ref[:, lo:lo + width])

    def conv_silu(pre, lo):
        cols = slice(lo, lo + D_GDN)
        groups = rows // SUBLANES
        tap = lambda j: wconv_ref[j:j + 1, cols].reshape(1, 1, D_GDN)
        shape3 = (1, SUBLANES, D_GDN)
        last_p = jnp.where(first_tile, 0.0, cbuf[0:SUBLANES, cols]).reshape(shape3)
        last_b = jnp.where(first_tile, 0.0, cbuf[SUBLANES:, cols]).reshape(shape3)
        cstate_ref[:, cols] = pre[rows - (CONV_W - 1):rows, :]
        cur = pre.reshape(groups, SUBLANES, D_GDN)
        sub = lax.broadcasted_iota(jnp.int32, (CONV_BLOCK, SUBLANES, D_GDN), 1)

        def delay(a, a_last, s):
            rolled = pltpu.roll(a, s, axis=1)
            rolled_prev = jnp.concatenate([pltpu.roll(a_last, s, axis=1), rolled[:-1]], axis=0)
            return jnp.where(sub >= s, rolled, rolled_prev)

        assert CONV_W == 4
        out = []
        for g0 in range(0, groups, CONV_BLOCK):
            p = cur[g0:g0 + CONV_BLOCK]
            dp = delay(p, last_p, 1)
            b = p * tap(1) + dp * tap(0)
            y = p * tap(3) + dp * tap(2) + delay(b, last_b, 2)
            last_p, last_b = p[CONV_BLOCK - 1:], b[CONV_BLOCK - 1:]
            out.append(_silu(y))
        cbuf[0:SUBLANES, cols] = last_p.reshape(SUBLANES, D_GDN)
        cbuf[SUBLANES:, cols] = last_b.reshape(SUBLANES, D_GDN)
        return jnp.concatenate(out, axis=0).reshape(rows, D_GDN)

    pre_vv = project(COL_VV)
    pre_u = project(COL_U)
    vg = _layer_norm(_gelu_tanh(pre_vv), lng_ref[...], lnb_ref[...]).astype(BF16)
    ab = project(COL_AB, LANES)
    pre_q = project(COL_Q)
    u_act = _gelu_tanh(pre_u)
    ti = lax.broadcasted_iota(jnp.int32, (SGU_CHUNK, SGU_CHUNK), 0)
    si = lax.broadcasted_iota(jnp.int32, (SGU_CHUNK, SGU_CHUNK), 1)
    for hh in range(HEADS):
        lo = hh * DH
        w_tril = jnp.where(ti >= si, ws_ref[hh], 0.0).astype(BF16)
        bias = bst_ref[:, hh:hh + 1]
        for c in range(rows // SGU_CHUNK):
            r0 = c * SGU_CHUNK
            mix = _dot(w_tril, vg[r0:r0 + SGU_CHUNK, lo:lo + DH]) + bias
            sg_ref[r0:r0 + SGU_CHUNK, lo:lo + DH] = (u_act[r0:r0 + SGU_CHUNK, lo:lo + DH] * mix).astype(BF16)
    pre_k = project(COL_K)

    lane = lax.broadcasted_iota(jnp.int32, (GDN_CHUNK, LANES), 1)
    g_col = (-jnp.exp(_decay_slots(alog_ref, (1, LANES), 1))
             * _softplus(ab + _decay_slots(dtb_ref, (1, LANES), 1)))
    abt = _dot_nt(wabt_ref[...], h)[0:2 * HEADS, :]
    sub = lax.broadcasted_iota(jnp.int32, (2 * HEADS, GDN_CHUNK), 0)
    g_row = (-jnp.exp(_decay_slots(alog_ref, (2 * HEADS, 1), 0))
             * _softplus(abt + _decay_slots(dtb_ref, (2 * HEADS, 1), 0)))
    ri = lax.broadcasted_iota(jnp.int32, (GDN_CHUNK, GDN_CHUNK), 0)
    ci = lax.broadcasted_iota(jnp.int32, (GDN_CHUNK, GDN_CHUNK), 1)
    tri_l = (ri >= ci).astype(BF16)
    tri_u = (ri <= ci).astype(BF16)
    beta_col = _sigmoid(ab)
    beta_row = _sigmoid(abt)
    for c in range(rows // GDN_CHUNK):
        r0 = c * GDN_CHUNK
        p0, p1, p2 = _split3(g_col[r0:r0 + GDN_CHUNK, :])
        gc = _dot(tri_l, p0) + _dot(tri_l, p1) + _dot(tri_l, p2)
        col_ref[r0:r0 + GDN_CHUNK, :] = jnp.where(lane < HEADS, beta_col[r0:r0 + GDN_CHUNK, :], gc)
        p0, p1, p2 = _split3(g_row[:, r0:r0 + GDN_CHUNK])
        gr = _dot(p0, tri_u) + _dot(p1, tri_u) + _dot(p2, tri_u)
        row_ref[c] = jnp.where(sub < HEADS, beta_row[:, r0:r0 + GDN_CHUNK], gr)

    y = conv_silu(pre_q, COL_Q)
    for hh in range(HEADS):
        lo = hh * DH
        qs_ref[:, lo:lo + DH] = (_l2n(y[:, lo:lo + DH]) * (DH ** -0.5)).astype(BF16)
    pre_v = project(COL_V)
    y = conv_silu(pre_k, COL_K)
    for hh in range(HEADS):
        lo = hh * DH
        k_ref[:, lo:lo + DH] = _l2n(y[:, lo:lo + DH]).astype(BF16)
    pre_z = project(COL_Z)
    v_ref[...] = conv_silu(pre_v, COL_V).astype(BF16)
    zs_ref[...] = _silu(pre_z).astype(BF16)


def _proj_call(x, gmix, w_all, wabt, wconv, alog, dtb, lng, lnb, ws, bst, tail_f32):
    b, l, _ = x.shape
    rows = PROJ_ROWS
    nt = l // rows
    steps = b * nt
    nchunk = rows // GDN_CHUNK
    tile = lambda w: pl.BlockSpec((None, rows, w), lambda i, j: (i, j, 0))
    ins = [x, gmix, w_all, wabt, wconv, alog, dtb, lng, lnb, ws, bst]
    smem = pl.BlockSpec(memory_space=pltpu.SMEM)
    in_specs = [tile(D_MODEL)] + [smem if a is alog or a is dtb else _resident(a.shape) for a in ins[1:]]
    in_specs += [pl.BlockSpec((None, w.shape[1] // steps, w.shape[2]), lambda i, j: (0, i * nt + j, 0))
                 for w in tail_f32]
    out_shape = [jax.ShapeDtypeStruct((b, l, D_GDN), BF16)] * 5 + [
        jax.ShapeDtypeStruct((b, l, LANES), F32),
        jax.ShapeDtypeStruct((b, l // GDN_CHUNK, 2 * HEADS, GDN_CHUNK), F32),
        jax.ShapeDtypeStruct((b, CONV_W - 1, D_QKV), F32),
    ] + [jax.ShapeDtypeStruct(w.shape[1:], BF16) for w in tail_f32]
    out_specs = [tile(D_GDN)] * 5 + [
        tile(LANES),
        pl.BlockSpec((None, nchunk, 2 * HEADS, GDN_CHUNK), lambda i, j: (i, j, 0, 0)),
        pl.BlockSpec((None, CONV_W - 1, D_QKV), lambda i, j: (i, 0, 0)),
    ] + [pl.BlockSpec((w.shape[1] // steps, w.shape[2]), lambda i, j: (i * nt + j, 0)) for w in tail_f32]
    ins = ins + list(tail_f32)
    return pl.pallas_call(
        _proj_kernel,
        grid=(b, nt),
        in_specs=in_specs,
        out_specs=out_specs,
        out_shape=out_shape,
        scratch_shapes=[pltpu.VMEM((2 * SUBLANES, D_QKV), F32)],
        compiler_params=pltpu.CompilerParams(
            dimension_semantics=("arbitrary", "arbitrary"), vmem_limit_bytes=VMEM_LIMIT),
        name="proj",
    )(*ins)


def _neumann_many(a_list):
    c = a_list[0].shape[0]
    p = [-a for a in a_list]
    pb = [x.astype(BF16) for x in p]
    q = [_dot(x, x) for x in pb]
    yield
    power = 2
    while 2 * power < c:
        qb = [x.astype(BF16) for x in q]
        r = [_dot(jnp.concatenate([pi.astype(BF16), qi], axis=0), qi) for pi, qi in zip(p, qb)]
        yield
        p = [pi + qi + ri[:c] for pi, qi, ri in zip(p, q, r)]
        q = [ri[c:] for ri in r]
        power *= 2
    tail = [_dot(pi.astype(BF16), qi.astype(BF16)) for pi, qi in zip(p, q)]
    yield
    return [pi + qi + ti for pi, qi, ti in zip(p, q, tail)]


def _interleave(steps, fillers):
    fillers = list(fillers)
    count = 0
    while True:
        try:
            next(steps)
        except StopIteration as stop:
            for f in fillers:
                f()
            return stop.value
        count += 1
        if fillers and count % 2 == 0:
            fillers.pop(0)()


def _gdn_kernel(qs_ref, k_ref, v_ref, col_ref, row_ref, zs_ref, gnorm_ref,
                og_ref, state_ref, s_scr):
    rows = qs_ref.shape[0]
    c_ = GDN_CHUNK
    t = pl.program_id(1)

    @pl.when(t == 0)
    def _():
        s_scr[...] = jnp.zeros(s_scr.shape, F32)

    ri = lax.broadcasted_iota(jnp.int32, (c_, c_), 0)
    ci = lax.broadcasted_iota(jnp.int32, (c_, c_), 1)
    incl = ri >= ci
    strict = ri > ci
    gnorm = gnorm_ref[...]

    def prepare(chunks):
        pairs = [(c, hh) for c in chunks for hh in range(HEADS)]
        cols = {c: col_ref[c * c_:(c + 1) * c_, :] for c in chunks}
        rws = {c: row_ref[c] for c in chunks}
        beta = [cols[c][:, hh:hh + 1] for c, hh in pairs]
        gc = [cols[c][:, HEADS + hh:HEADS + hh + 1] for c, hh in pairs]
        gr = [rws[c][HEADS + hh:HEADS + hh + 1, :] for c, hh in pairs]
        glast = [cols[c][c_ - 1:c_, HEADS + hh:HEADS + hh + 1] for c, hh in pairs]
        kh = [k_ref[c * c_:(c + 1) * c_, hh * DH:(hh + 1) * DH] for c, hh in pairs]
        qh = [qs_ref[c * c_:(c + 1) * c_, hh * DH:(hh + 1) * DH] for c, hh in pairs]
        kf = [x.astype(F32) for x in kh]
        kb = [x * b for x, b in zip(kf, beta)]
        m1 = [_dot_nt(jnp.concatenate([x.astype(BF16), q], axis=0), k) for x, q, k in zip(kb, qh, kh)]
        yield
        decay = [jnp.exp(jnp.where(incl, a - b, -jnp.inf)) for a, b in zip(gc, gr)]
        attn = [(m[c_:] * d).astype(BF16) for m, d in zip(m1, decay)]
        p = yield from _neumann_many([jnp.where(strict, m[:c_] * d, 0.0) for m, d in zip(m1, decay)])
        eg = [jnp.exp(x) for x in gc]
        rhs = [jnp.concatenate([x * e, v_ref[c * c_:(c + 1) * c_, hh * DH:(hh + 1) * DH].astype(F32) * b], axis=1)
               for x, e, b, (c, hh) in zip(kb, eg, beta, pairs)]
        wu = [(x + _dot(pi.astype(BF16), x.astype(BF16))).astype(BF16) for x, pi in zip(rhs, p)]
        yield
        ke_t = [jnp.transpose((x * jnp.exp(gl - g)).astype(BF16)) for x, gl, g in zip(kf, glast, gc)]
        r1 = [_dot(jnp.concatenate([kt, a], axis=0), x) for kt, a, x in zip(ke_t, attn, wu)]
        yield
        staged = {}
        for i, pair in enumerate(pairs):
            q_eff = qh[i].astype(F32) * eg[i] - r1[i][DH:, :DH]
            mq = jnp.concatenate([(-r1[i][:DH, :DH]).astype(BF16), q_eff.astype(BF16)], axis=0)
            staged[pair] = (mq, r1[i][:DH, DH:], r1[i][DH:, DH:], jnp.exp(glast[i]))
        return staged

    state = [s_scr[hh] for hh in range(HEADS)]

    def advance(c, staged):
        r0 = c * c_
        for hh in range(HEADS):
            lo = hh * DH
            mq, n_mat, o_zero, e_last = staged[c, hh]
            r = _dot(mq, state[hh].astype(BF16))
            state[hh] = state[hh] * e_last + r[:DH] + n_mat
            og = _rms(r[DH:] + o_zero, gnorm) * zs_ref[r0:r0 + c_, lo:lo + DH].astype(F32)
            og_ref[r0:r0 + c_, lo:lo + DH] = og.astype(BF16)

    nchunk = rows // c_
    fillers = []
    for g0 in range(0, nchunk, GDN_GROUP):
        group = list(range(g0, g0 + GDN_GROUP))
        staged = _interleave(prepare(group), fillers)
        fillers = [functools.partial(advance, c, staged) for c in group]
    for f in fillers:
        f()
    for hh in range(HEADS):
        s_scr[hh] = state[hh]

    @pl.when(t == pl.num_programs(1) - 1)
    def _():
        state_ref[...] = s_scr[...]


def _gdn_call(qs, k, v, col, row, zs, gnorm):
    b, l, _ = qs.shape
    rows = GDN_ROWS
    nt = l // rows
    nchunk = rows // GDN_CHUNK
    tile = lambda w: pl.BlockSpec((None, rows, w), lambda i, j: (i, j, 0))
    return pl.pallas_call(
        _gdn_kernel,
        grid=(b, nt),
        in_specs=[tile(D_GDN)] * 3 + [
            tile(LANES),
            pl.BlockSpec((None, nchunk, 2 * HEADS, GDN_CHUNK), lambda i, j: (i, j, 0, 0)),
            tile(D_GDN),
            _resident(gnorm.shape),
        ],
        out_specs=[tile(D_GDN), pl.BlockSpec((None, HEADS, DH, DH), lambda i, j: (i, 0, 0, 0))],
        out_shape=[jax.ShapeDtypeStruct((b, l, D_GDN), BF16),
                   jax.ShapeDtypeStruct((b, HEADS, DH, DH), F32)],
        scratch_shapes=[pltpu.VMEM((HEADS, DH, DH), F32)],
        compiler_params=pltpu.CompilerParams(
            dimension_semantics=("arbitrary", "arbitrary"), vmem_limit_bytes=VMEM_LIMIT),
        name="gdn",
    )(qs, k, v, col, row, zs, gnorm)


def _tail_kernel(x_ref, og_ref, sg_ref, p_ref, xs_ref, sgs_ref, ps_ref,
                 s_ref, qt_ref, kt_ref, v_ref, col_ref, zs_ref, gnorm_ref,
                 wout_ref, gff_ref, wup_ref, wdown_ref, gple_ref, wple_ref, wgate_ref, gfin_ref,
                 y_ref, ys_ref, snew_ref, ogs_scr):
    def rows_through_tail(x_ref, og, sg_ref, p_ref, y_ref, side_job=None):
        x1 = x_ref[...] + _dot(og, wout_ref[0:D_GDN, :]) + _dot(sg_ref[...], wout_ref[D_GDN:, :])
        h2 = _rms(x1, gff_ref[...]).astype(BF16)
        y_ref[...] = x1
        ff_block = D_FF // 4
        for c in range(D_FF // ff_block):
            lo = c * ff_block
            hid = jnp.maximum(_dot(h2, wup_ref[:, lo:lo + ff_block]), 0.0)
            y_ref[...] += _dot((hid * hid).astype(BF16), wdown_ref[lo:lo + ff_block, :])
            if c == 0 and side_job is not None:
                side_job()
        acc = y_ref[...]
        h3 = _rms(acc, gple_ref[...]).astype(BF16)
        gate = _sigmoid(_dot(h3, wgate_ref[...]))
        x3 = acc + _dot(p_ref[...].astype(BF16), wple_ref[...]) * gate
        y_ref[...] = _rms(x3, gfin_ref[...])

    step = pl.program_id(0)
    last = pl.num_programs(0) - 1
    assert SAMPLE_BLOCK == 2 * SAMPLE_HALF

    @pl.when(step == 0)
    def _():
        ogs_scr[...] = jnp.zeros(ogs_scr.shape, ogs_scr.dtype)

    @pl.when(step < last)
    def _():
        first_row = pl.multiple_of((step // 2) * SAMPLE_BLOCK, SAMPLE_BLOCK)
        og_rows = ogs_scr.at[pl.ds(first_row, SAMPLE_BLOCK), :]
        side_job = functools.partial(_sample_delta_step, step % 2 == 1, step * SAMPLE_HALF, s_ref, qt_ref, kt_ref,
                                     v_ref, col_ref, zs_ref, gnorm_ref, og_rows, snew_ref)
        rows_through_tail(x_ref, og_ref[...], sg_ref, p_ref, y_ref, side_job)

    @pl.when(step == last)
    def _():
        rows_through_tail(xs_ref.at[:, 0, :], ogs_scr[...].astype(BF16), sgs_ref, ps_ref.at[:, 0, :],
                          ys_ref.at[:, 0, :])


def _tail_call(x, og, sg, p, xs, sgs, ps, state, q, kt, v, col, zs, gnorm,
               wout, gff, wup, wdown, gple, wple, wgate, gfin):
    n = x.shape[0]
    ns = xs.shape[0]
    rows = TAIL_ROWS
    steps = n // rows
    blk = SAMPLE_BLOCK
    per_block = SAMPLE_BLOCK // SAMPLE_HALF
    assert ns == steps * SAMPLE_HALF
    prompt_step = lambda i: jnp.minimum(i, steps - 1)
    tile = lambda w: pl.BlockSpec((rows, w), lambda i: (prompt_step(i), 0))
    srow = lambda w: pl.BlockSpec((blk, w), lambda i: (prompt_step(i) // per_block, 0))
    sample = [xs, sgs, ps]
    weights = [wout, gff, wup, wdown, gple, wple, wgate, gfin]
    state_block = pl.BlockSpec((blk, HEADS, DH, DH), lambda i: (prompt_step(i) // per_block, 0, 0, 0))
    cols4 = _resident(q.shape)
    return pl.pallas_call(
        _tail_kernel,
        grid=(steps + 1,),
        in_specs=[tile(D_MODEL), tile(D_GDN), tile(D_SGU), tile(D_PLE)]
        + [_resident(a.shape) for a in sample]
        + [state_block, cols4, cols4, srow(D_GDN), srow(LANES), srow(D_GDN), _resident(gnorm.shape)]
        + [_resident(a.shape) for a in weights],
        out_specs=[tile(D_MODEL), pl.BlockSpec(xs.shape, lambda i: (0, 0, 0)), state_block],
        out_shape=[jax.ShapeDtypeStruct((n, D_MODEL), F32), jax.ShapeDtypeStruct(xs.shape, F32),
                   jax.ShapeDtypeStruct(state.shape, F32)],
        scratch_shapes=[pltpu.VMEM((ns, D_GDN), F32)],
        compiler_params=pltpu.CompilerParams(
            dimension_semantics=("arbitrary",), vmem_limit_bytes=VMEM_LIMIT),
        name="tail",
    )(x, og, sg, p, *sample, state, q, kt, v, col, zs, gnorm, *weights)


def _sample_proj_kernel(x_ref, hist_ref, gmix_ref, w_ref, wconv_ref,
                        alog_ref, dtb_ref, lng_ref, lnb_ref, ws_ref, bs_ref,
                        nconv_ref, vg_ref, sg_ref, zs_ref, v_ref, col_ref, qt_ref, kt_ref):
    h = _rms(x_ref[:, 0, :], gmix_ref[...]).astype(BF16)
    pre = _dot(h, w_ref[:, COL_Q:COL_Q + D_QKV])
    y = pre * wconv_ref[CONV_W - 1:CONV_W, :]
    for j in range(CONV_W - 1):
        y = y + hist_ref[j] * wconv_ref[j:j + 1, :]
        if j > 0:
            nconv_ref[j - 1] = hist_ref[j]
    nconv_ref[CONV_W - 2] = pre
    y = _silu(y)
    v_ref[...] = y[:, 2 * D_GDN:]
    for hh in range(HEADS):
        lo = hh * DH
        qt_ref[hh] = jnp.transpose(_l2n(y[:, lo:lo + DH]) * (DH ** -0.5))
        kt_ref[hh] = jnp.transpose(_l2n(y[:, D_GDN + lo:D_GDN + lo + DH]))
    zs_ref[...] = _silu(_dot(h, w_ref[:, COL_Z:COL_Z + D_GDN]))
    ab = _dot(h, w_ref[:, COL_AB:COL_AB + LANES])
    lane = lax.broadcasted_iota(jnp.int32, ab.shape, 1)
    g_col = (-jnp.exp(_decay_slots(alog_ref, (1, LANES), 1))
             * _softplus(ab + _decay_slots(dtb_ref, (1, LANES), 1)))
    col_ref[...] = jnp.where(lane < HEADS, _sigmoid(ab), g_col)
    uv = _gelu_tanh(_dot(h, w_ref[:, COL_U:COL_U + 2 * D_SGU]))
    vg = _layer_norm(uv[:, D_SGU:], lng_ref[...], lnb_ref[...])
    vg_ref[:, 0, :] = vg
    for hh in range(HEADS):
        lo = hh * DH
        mix = ws_ref[hh, 0:1, 0:1] * vg[:, lo:lo + DH] + bs_ref[hh:hh + 1, 0:1]
        sg_ref[:, lo:lo + DH] = (uv[:, lo:lo + DH] * mix).astype(BF16)


def _sample_proj_call(x, hist, gmix, w_all, wconv, alog, dtb, lng, lnb, ws, bs):
    n = x.shape[0]
    ins = [x, hist, gmix, w_all, wconv, alog, dtb, lng, lnb, ws, bs]
    vmem, smem = pl.BlockSpec(memory_space=pltpu.VMEM), pl.BlockSpec(memory_space=pltpu.SMEM)
    out_shape = [
        jax.ShapeDtypeStruct((CONV_W - 1, n, D_QKV), F32),
        jax.ShapeDtypeStruct((n, 1, D_SGU), F32),
        jax.ShapeDtypeStruct((n, D_SGU), BF16),
        jax.ShapeDtypeStruct((n, D_GDN), F32),
        jax.ShapeDtypeStruct((n, D_GDN), F32),
        jax.ShapeDtypeStruct((n, LANES), F32),
        jax.ShapeDtypeStruct((HEADS, DH, n), F32),
        jax.ShapeDtypeStruct((HEADS, DH, n), F32),
    ]
    return pl.pallas_call(
        _sample_proj_kernel,
        in_specs=[smem if a is alog or a is dtb else vmem for a in ins],
        out_shape=out_shape,
        compiler_params=pltpu.CompilerParams(vmem_limit_bytes=VMEM_LIMIT),
        name="sample_proj",
    )(*ins)


def _sample_delta_step(upper, lane0, s_ref, qt_ref, kt_ref, v_ref, col_ref, zs_ref, gnorm_ref, og_ref, snew_ref):
    gnorm = gnorm_ref[...]
    row0 = jnp.where(upper, SAMPLE_HALF, 0)
    pick = lambda blk, bb: jnp.where(upper, blk[SAMPLE_HALF + bb:SAMPLE_HALF + bb + 1], blk[bb:bb + 1])
    col_blk, v_blk, zs_blk = col_ref[...], v_ref[...], zs_ref[...]
    pairs = [(bb, hh) for bb in range(SAMPLE_HALF) for hh in range(HEADS)]
    n_seq = kt_ref.shape[-1]
    k_here = [pltpu.roll(kt_ref[hh], n_seq - lane0, axis=1) for hh in range(HEADS)]
    q_here = [pltpu.roll(qt_ref[hh], n_seq - lane0, axis=1) for hh in range(HEADS)]
    kcol = [k_here[hh][:, bb:bb + 1] for bb, hh in pairs]
    qcol = [q_here[hh][:, bb:bb + 1] for bb, hh in pairs]
    beta = [pick(col_blk[:, hh:hh + 1], bb) for bb, hh in pairs]
    eg = [jnp.exp(pick(col_blk[:, HEADS + hh:HEADS + hh + 1], bb)) for bb, hh in pairs]
    sd = [s_ref[row0 + bb, hh] * e for (bb, hh), e in zip(pairs, eg)]
    sk = [jnp.sum(s * kc, axis=0, keepdims=True) for s, kc in zip(sd, kcol)]
    v_new = [b * (pick(v_blk[:, hh * DH:(hh + 1) * DH], bb) - x) for (bb, hh), b, x in zip(pairs, beta, sk)]
    s1 = [s + kc * vn for s, kc, vn in zip(sd, kcol, v_new)]
    for (bb, hh), s in zip(pairs, s1):
        snew_ref[row0 + bb, hh] = s
    o = [jnp.sum(s * qc, axis=0, keepdims=True) for s, qc in zip(s1, qcol)]
    sub = lax.broadcasted_iota(jnp.int32, (SAMPLE_BLOCK, DH), 0)
    for hh in range(HEADS):
        lo = hh * DH
        merged = og_ref[:, lo:lo + DH]
        for bb in range(SAMPLE_HALF):
            og = _rms(o[bb * HEADS + hh], gnorm) * pick(zs_blk[:, lo:lo + DH], bb)
            merged = jnp.where(sub == row0 + bb, og.astype(og_ref.dtype), merged)
        og_ref[:, lo:lo + DH] = merged


def kernel(x_prompt, x_sample, state_conv, state_gdn, p_prompt, p_sample, g_mix, w_in, w_conv, a_log, dt_bias, gdn_norm, sgu_ln_g, sgu_ln_b, w_s, b_s, w_out, g_ff, w_up, w_down, g_ple, w_ple, w_ple_gate, g_final):
    depth = w_in.shape[0]
    assert depth == 1, "single-layer trunk only"
    b, l, d = x_prompt.shape
    n = x_sample.shape[0]
    assert x_sample.shape[1] == 1 and d == D_MODEL
    assert l % PROJ_ROWS == 0 and l % GDN_ROWS == 0 and (b * l) % TAIL_ROWS == 0 and n % SAMPLE_BLOCK == 0

    w_all, wabt = _prep_call(w_in)
    row2 = lambda a: a.reshape(1, -1)
    alog, dtb = a_log[0], dt_bias[0]
    gmix, gff, gple, gfin = row2(g_mix[0]), row2(g_ff[0]), row2(g_ple[0]), row2(g_final)
    lng, lnb, gnorm = row2(sgu_ln_g[0]), row2(sgu_ln_b[0]), row2(gdn_norm[0])

    qs, k, v, zs, sg, col, row, conv_p, wout, wup, wdown, wple, wgate = _proj_call(
        x_prompt, gmix, w_all, wabt, w_conv[0], alog, dtb,
        lng, lnb, w_s[0], b_s[0].T, (w_out, w_up, w_down, w_ple, w_ple_gate))
    tail_w = (wout, gff, wup, wdown, gple, wple, wgate, gfin)
    og, gdn_p = _gdn_call(qs, k, v, col, row, zs, gnorm)

    nconv, vg_s, sg_s, zs_s, v_s, col_s, q_s, kt = _sample_proj_call(
        x_sample, jnp.transpose(state_conv[0], (1, 0, 2)), gmix, w_all,
        w_conv[0], alog, dtb, lng, lnb, w_s[0], b_s[0])

    y_prompt, y_sample, gdn_s = _tail_call(
        x_prompt.reshape(b * l, d), og.reshape(b * l, D_GDN), sg.reshape(b * l, D_SGU),
        p_prompt[0].reshape(b * l, D_PLE),
        x_sample, sg_s, p_sample[0],
        state_gdn[0], q_s, kt, v_s, col_s, zs_s, gnorm, *tail_w)
    y_prompt = y_prompt.reshape(b, l, d)

    return (y_prompt, y_sample,
            conv_p[None], gdn_p[None],
            jnp.transpose(nconv, (1, 0, 2))[None], gdn_s[None],
            vg_s[None])
```

```python
import functools

import jax
import jax.numpy as jnp
from jax import lax
from jax.experimental import pallas as pl
from jax.experimental.pallas import tpu as pltpu

F32 = jnp.float32
BF16 = jnp.bfloat16

D_MODEL = 1024
D_GDN = 512
D_SGU = 512
HEADS = 4
DH = 128
D_QKV = 3 * D_GDN
D_FF = 4 * D_MODEL
D_PLE = 256
CONV_W = 4
GDN_CHUNK = 64
SGU_CHUNK = 128
EPS = 1e-6
LANES = 128
SUBLANES = 8
VMEM_LIMIT = 56 * 1024 * 1024

COL_Q, COL_K, COL_V, COL_Z, COL_U, COL_VV, COL_AB = 0, 512, 1024, 1536, 2048, 2560, 3072
W_COLS = COL_AB + LANES

PROJ_ROWS = 1024
CONV_BLOCK = 8
GDN_ROWS = 2048
GDN_GROUP = 4
TAIL_ROWS = 512
SAMPLE_BLOCK = 8
SAMPLE_HALF = 4
PREP_STEPS = 8


def _dot(a, b):
    return jnp.dot(a, b, preferred_element_type=F32)


def _dot_nt(a, b):
    return lax.dot_general(a, b, (((1,), (1,)), ((), ())), preferred_element_type=F32)


def _split3(x):
    hi = x.astype(BF16)
    r1 = x - hi.astype(F32)
    mid = r1.astype(BF16)
    lo = (r1 - mid.astype(F32)).astype(BF16)
    return hi, mid, lo


def _rms(x, gain):
    return x * lax.rsqrt(jnp.mean(x * x, axis=-1, keepdims=True) + EPS) * gain


def _sigmoid(x):
    return 1.0 / (1.0 + jnp.exp(-x))


def _silu(x):
    hx = 0.5 * x
    return hx + hx * jnp.tanh(hx)


def _softplus(x):
    return jnp.maximum(x, 0.0) + jnp.log(1.0 + jnp.exp(-jnp.abs(x)))


def _gelu_tanh(x):
    c = 0.7978845608028654
    hx = 0.5 * x
    return hx + hx * jnp.tanh(x * (c + (c * 0.044715) * (x * x)))


def _l2n(x):
    return x * lax.rsqrt(jnp.sum(x * x, axis=-1, keepdims=True) + 1e-6)


def _layer_norm(x, gain, bias):
    mu = jnp.mean(x, axis=-1, keepdims=True)
    xc = x - mu
    var = jnp.mean(xc * xc, axis=-1, keepdims=True)
    return xc * lax.rsqrt(var + EPS) * gain + bias


def _decay_slots(scalar_ref, shape, axis):
    idx = lax.broadcasted_iota(jnp.int32, shape, axis)
    out = jnp.zeros(shape, F32)
    for hh in range(HEADS):
        out = jnp.where(idx == HEADS + hh, scalar_ref[hh], out)
    return out


def _resident(shape):
    nd = len(shape)
    return pl.BlockSpec(shape, lambda *_: (0,) * nd, pipeline_mode=pl.Buffered(1))


def _prep_kernel(win_ref, wall_o, wabt_o):
    o1 = D_QKV + D_GDN
    o3 = o1 + 2 * HEADS

    def put(dst, src):
        wall_o[:, dst:dst + LANES] = jnp.transpose(win_ref[src:src + LANES, :]).astype(BF16)

    for j in range(o1 // LANES):
        put(j * LANES, j * LANES)
    for j in range(2 * D_SGU // LANES):
        put(COL_U + j * LANES, o3 + j * LANES)
    ab = win_ref[o1:o3, :]
    ab16 = jnp.concatenate([ab, jnp.zeros((2 * SUBLANES - 2 * HEADS, LANES), F32)], axis=0)
    wabt_o[...] = ab16.astype(BF16)
    ab128 = jnp.concatenate([ab, jnp.zeros((LANES - 2 * HEADS, LANES), F32)], axis=0)
    wall_o[:, COL_AB:] = jnp.transpose(ab128).astype(BF16)


def _prep_call(w_in):
    steps = PREP_STEPS
    d, d_in = w_in.shape[1], w_in.shape[2]
    assert d // steps == LANES
    win_t = jnp.transpose(w_in[0])
    return pl.pallas_call(
        _prep_kernel,
        grid=(steps,),
        in_specs=[pl.BlockSpec((d_in, LANES), lambda i: (0, i))],
        out_specs=[pl.BlockSpec((LANES, W_COLS), lambda i: (i, 0)),
                   pl.BlockSpec((2 * SUBLANES, LANES), lambda i: (0, i))],
        out_shape=[jax.ShapeDtypeStruct((d, W_COLS), BF16), jax.ShapeDtypeStruct((2 * SUBLANES, d), BF16)],
        compiler_params=pltpu.CompilerParams(
            dimension_semantics=("arbitrary",), vmem_limit_bytes=VMEM_LIMIT),
        name="prep",
    )(win_t)


def _proj_kernel(x_ref, gmix_ref, w_ref, wabt_ref, wconv_ref,
                 alog_ref, dtb_ref, lng_ref, lnb_ref,
                 ws_ref, bst_ref, wout_ref, wup_ref, wdown_ref, wple_ref, wgate_ref,
                 qs_ref, k_ref, v_ref, zs_ref, sg_ref, col_ref, row_ref, cstate_ref,
                 wout_o, wup_o, wdown_o, wple_o, wgate_o,
                 cbuf):
    rows = x_ref.shape[0]
    first_tile = pl.program_id(1) == 0
    wout_o[...] = wout_ref[...].astype(BF16)
    wup_o[...] = wup_ref[...].astype(BF16)
    wdown_o[...] = wdown_ref[...].astype(BF16)
    wple_o[...] = wple_ref[...].astype(BF16)
    wgate_o[...] = wgate_ref[...].astype(BF16)
    h = _rms(x_ref[...], gmix_ref[...]).astype(BF16)

    def project(lo, width=D_GDN):
        return _dot(h, w_ref[:, lo:lo + width])

    def conv_silu(pre, lo):
        cols = slice(lo, lo + D_GDN)
        groups = rows // SUBLANES
        tap = lambda j: wconv_ref[j:j + 1, cols].reshape(1, 1, D_GDN)
        shape3 = (1, SUBLANES, D_GDN)
        last_p = jnp.where(first_tile, 0.0, cbuf[0:SUBLANES, cols]).reshape(shape3)
        last_b = jnp.where(first_tile, 0.0, cbuf[SUBLANES:, cols]).reshape(shape3)
        cstate_ref[:, cols] = pre[rows - (CONV_W - 1):rows, :]
        cur = pre.reshape(groups, SUBLANES, D_GDN)
        sub = lax.broadcasted_iota(jnp.int32, (CONV_BLOCK, SUBLANES, D_GDN), 1)

        def delay(a, a_last, s):
            rolled = pltpu.roll(a, s, axis=1)
            rolled_prev = jnp.concatenate([pltpu.roll(a_last, s, axis=1), rolled[:-1]], axis=0)
            return jnp.where(sub >= s, rolled, rolled_prev)

        assert CONV_W == 4
        out = []
        for g0 in range(0, groups, CONV_BLOCK):
            p = cur[g0:g0 + CONV_BLOCK]
            dp = delay(p, last_p, 1)
            b = p * tap(1) + dp * tap(0)
            y = p * tap(3) + dp * tap(2) + delay(b, last_b, 2)
            last_p, last_b = p[CONV_BLOCK - 1:], b[CONV_BLOCK - 1:]
            out.append(_silu(y))
        cbuf[0:SUBLANES, cols] = last_p.reshape(SUBLANES, D_GDN)
        cbuf[SUBLANES:, cols] = last_b.reshape(SUBLANES, D_GDN)
        return jnp.concatenate(out, axis=0).reshape(rows, D_GDN)

    pre_vv = project(COL_VV)
    pre_u = project(COL_U)
    vg = _layer_norm(_gelu_tanh(pre_vv), lng_ref[...], lnb_ref[...]).astype(BF16)
    ab = project(COL_AB, LANES)
    pre_q = project(COL_Q)
    u_act = _gelu_tanh(pre_u)
    ti = lax.broadcasted_iota(jnp.int32, (SGU_CHUNK, SGU_CHUNK), 0)
    si = lax.broadcasted_iota(jnp.int32, (SGU_CHUNK, SGU_CHUNK), 1)
    for hh in range(HEADS):
        lo = hh * DH
        w_tril = jnp.where(ti >= si, ws_ref[hh], 0.0).astype(BF16)
        bias = bst_ref[:, hh:hh + 1]
        for c in range(rows // SGU_CHUNK):
            r0 = c * SGU_CHUNK
            mix = _dot(w_tril, vg[r0:r0 + SGU_CHUNK, lo:lo + DH]) + bias
            sg_ref[r0:r0 + SGU_CHUNK, lo:lo + DH] = (u_act[r0:r0 + SGU_CHUNK, lo:lo + DH] * mix).astype(BF16)
    pre_k = project(COL_K)

    lane = lax.broadcasted_iota(jnp.int32, (GDN_CHUNK, LANES), 1)
    g_col = (-jnp.exp(_decay_slots(alog_ref, (1, LANES), 1))
             * _softplus(ab + _decay_slots(dtb_ref, (1, LANES), 1)))
    abt = _dot_nt(wabt_ref[...], h)[0:2 * HEADS, :]
    sub = lax.broadcasted_iota(jnp.int32, (2 * HEADS, GDN_CHUNK), 0)
    g_row = (-jnp.exp(_decay_slots(alog_ref, (2 * HEADS, 1), 0))
             * _softplus(abt + _decay_slots(dtb_ref, (2 * HEADS, 1), 0)))
    ri = lax.broadcasted_iota(jnp.int32, (GDN_CHUNK, GDN_CHUNK), 0)
    ci = lax.broadcasted_iota(jnp.int32, (GDN_CHUNK, GDN_CHUNK), 1)
    tri_l = (ri >= ci).astype(BF16)
    tri_u = (ri <= ci).astype(BF16)
    beta_col = _sigmoid(ab)
    beta_row = _sigmoid(abt)
    for c in range(rows // GDN_CHUNK):
        r0 = c * GDN_CHUNK
        p0, p1, p2 = _split3(g_col[r0:r0 + GDN_CHUNK, :])
        gc = _dot(tri_l, p0) + _dot(tri_l, p1) + _dot(tri_l, p2)
        col_ref[r0:r0 + GDN_CHUNK, :] = jnp.where(lane < HEADS, beta_col[r0:r0 + GDN_CHUNK, :], gc)
        p0, p1, p2 = _split3(g_row[:, r0:r0 + GDN_CHUNK])
        gr = _dot(p0, tri_u) + _dot(p1, tri_u) + _dot(p2, tri_u)
        row_ref[c] = jnp.where(sub < HEADS, beta_row[:, r0:r0 + GDN_CHUNK], gr)

    y = conv_silu(pre_q, COL_Q)
    for hh in range(HEADS):
        lo = hh * DH
        qs_ref[:, lo:lo + DH] = (_l2n(y[:, lo:lo + DH]) * (DH ** -0.5)).astype(BF16)
    pre_v = project(COL_V)
    y = conv_silu(pre_k, COL_K)
    for hh in range(HEADS):
        lo = hh * DH
        k_ref[:, lo:lo + DH] = _l2n(y[:, lo:lo + DH]).astype(BF16)
    pre_z = project(COL_Z)
    v_ref[...] = conv_silu(pre_v, COL_V).astype(BF16)
    zs_ref[...] = _silu(pre_z).astype(BF16)


def _proj_call(x, gmix, w_all, wabt, wconv, alog, dtb, lng, lnb, ws, bst, tail_f32):
    b, l, _ = x.shape
    rows = PROJ_ROWS
    nt = l // rows
    steps = b * nt
    nchunk = rows // GDN_CHUNK
    tile = lambda w: pl.BlockSpec((None, rows, w), lambda i, j: (i, j, 0))
    ins = [x, gmix, w_all, wabt, wconv, alog, dtb, lng, lnb, ws, bst]
    smem = pl.BlockSpec(memory_space=pltpu.SMEM)
    in_specs = [tile(D_MODEL)] + [smem if a is alog or a is dtb else _resident(a.shape) for a in ins[1:]]
    in_specs += [pl.BlockSpec((None, w.shape[1] // steps, w.shape[2]), lambda i, j: (0, i * nt + j, 0))
                 for w in tail_f32]
    out_shape = [jax.ShapeDtypeStruct((b, l, D_GDN), BF16)] * 5 + [
        jax.ShapeDtypeStruct((b, l, LANES), F32),
        jax.ShapeDtypeStruct((b, l // GDN_CHUNK, 2 * HEADS, GDN_CHUNK), F32),
        jax.ShapeDtypeStruct((b, CONV_W - 1, D_QKV), F32),
    ] + [jax.ShapeDtypeStruct(w.shape[1:], BF16) for w in tail_f32]
    out_specs = [tile(D_GDN)] * 5 + [
        tile(LANES),
        pl.BlockSpec((None, nchunk, 2 * HEADS, GDN_CHUNK), lambda i, j: (i, j, 0, 0)),
        pl.BlockSpec((None, CONV_W - 1, D_QKV), lambda i, j: (i, 0, 0)),
    ] + [pl.BlockSpec((w.shape[1] // steps, w.shape[2]), lambda i, j: (i * nt + j, 0)) for w in tail_f32]
    ins = ins + list(tail_f32)
    return pl.pallas_call(
        _proj_kernel,
        grid=(b, nt),
        in_specs=in_specs,
        out_specs=out_specs,
        out_shape=out_shape,
        scratch_shapes=[pltpu.VMEM((2 * SUBLANES, D_QKV), F32)],
        compiler_params=pltpu.CompilerParams(
            dimension_semantics=("arbitrary", "arbitrary"), vmem_limit_bytes=VMEM_LIMIT),
        name="proj",
    )(*ins)


def _neumann_many(a_list):
    c = a_list[0].shape[0]
    p = [-a for a in a_list]
    pb = [x.astype(BF16) for x in p]
    q = [_dot(x, x) for x in pb]
    yield
    power = 2
    while 2 * power < c:
        qb = [x.astype(BF16) for x in q]
        r = [_dot(jnp.concatenate([pi.astype(BF16), qi], axis=0), qi) for pi, qi in zip(p, qb)]
        yield
        p = [pi + qi + ri[:c] for pi, qi, ri in zip(p, q, r)]
        q = [ri[c:] for ri in r]
        power *= 2
    tail = [_dot(pi.astype(BF16), qi.astype(BF16)) for pi, qi in zip(p, q)]
    yield
    return [pi + qi + ti for pi, qi, ti in zip(p, q, tail)]


def _interleave(steps, fillers):
    fillers = list(fillers)
    count = 0
    while True:
        try:
            next(steps)
        except StopIteration as stop:
            for f in fillers:
                f()
            return stop.value
        count += 1
        if fillers and count % 2 == 0:
            fillers.pop(0)()


def _gdn_kernel(qs_ref, k_ref, v_ref, col_ref, row_ref, zs_ref, gnorm_ref,
                og_ref, state_ref, s_scr):
    rows = qs_ref.shape[0]
    c_ = GDN_CHUNK
    t = pl.program_id(1)

    @pl.when(t == 0)
    def _():
        s_scr[...] = jnp.zeros(s_scr.shape, F32)

    ri = lax.broadcasted_iota(jnp.int32, (c_, c_), 0)
    ci = lax.broadcasted_iota(jnp.int32, (c_, c_), 1)
    incl = ri >= ci
    strict = ri > ci
    gnorm = gnorm_ref[...]

    def prepare(chunks):
        pairs = [(c, hh) for c in chunks for hh in range(HEADS)]
        cols = {c: col_ref[c * c_:(c + 1) * c_, :] for c in chunks}
        rws = {c: row_ref[c] for c in chunks}
        beta = [cols[c][:, hh:hh + 1] for c, hh in pairs]
        gc = [cols[c][:, HEADS + hh:HEADS + hh + 1] for c, hh in pairs]
        gr = [rws[c][HEADS + hh:HEADS + hh + 1, :] for c, hh in pairs]
        glast = [cols[c][c_ - 1:c_, HEADS + hh:HEADS + hh + 1] for c, hh in pairs]
        kh = [k_ref[c * c_:(c + 1) * c_, hh * DH:(hh + 1) * DH] for c, hh in pairs]
        qh = [qs_ref[c * c_:(c + 1) * c_, hh * DH:(hh + 1) * DH] for c, hh in pairs]
        kf = [x.astype(F32) for x in kh]
        kb = [x * b for x, b in zip(kf, beta)]
        m1 = [_dot_nt(jnp.concatenate([x.astype(BF16), q], axis=0), k) for x, q, k in zip(kb, qh, kh)]
        yield
        decay = [jnp.exp(jnp.where(incl, a - b, -jnp.inf)) for a, b in zip(gc, gr)]
        attn = [(m[c_:] * d).astype(BF16) for m, d in zip(m1, decay)]
        p = yield from _neumann_many([jnp.where(strict, m[:c_] * d, 0.0) for m, d in zip(m1, decay)])
        eg = [jnp.exp(x) for x in gc]
        rhs = [jnp.concatenate([x * e, v_ref[c * c_:(c + 1) * c_, hh * DH:(hh + 1) * DH].astype(F32) * b], axis=1)
               for x, e, b, (c, hh) in zip(kb, eg, beta, pairs)]
        wu = [(x + _dot(pi.astype(BF16), x.astype(BF16))).astype(BF16) for x, pi in zip(rhs, p)]
        yield
        ke_t = [jnp.transpose((x * jnp.exp(gl - g)).astype(BF16)) for x, gl, g in zip(kf, glast, gc)]
        r1 = [_dot(jnp.concatenate([kt, a], axis=0), x) for kt, a, x in zip(ke_t, attn, wu)]
        yield
        staged = {}
        for i, pair in enumerate(pairs):
            q_eff = qh[i].astype(F32) * eg[i] - r1[i][DH:, :DH]
            mq = jnp.concatenate([(-r1[i][:DH, :DH]).astype(BF16), q_eff.astype(BF16)], axis=0)
            staged[pair] = (mq, r1[i][:DH, DH:], r1[i][DH:, DH:], jnp.exp(glast[i]))
        return staged

    state = [s_scr[hh] for hh in range(HEADS)]

    def advance(c, staged):
        r0 = c * c_
        for hh in range(HEADS):
            lo = hh * DH
            mq, n_mat, o_zero, e_last = staged[c, hh]
            r = _dot(mq, state[hh].astype(BF16))
            state[hh] = state[hh] * e_last + r[:DH] + n_mat
            og = _rms(r[DH:] + o_zero, gnorm) * zs_ref[r0:r0 + c_, lo:lo + DH].astype(F32)
            og_ref[r0:r0 + c_, lo:lo + DH] = og.astype(BF16)

    nchunk = rows // c_
    fillers = []
    for g0 in range(0, nchunk, GDN_GROUP):
        group = list(range(g0, g0 + GDN_GROUP))
        staged = _interleave(prepare(group), fillers)
        fillers = [functools.partial(advance, c, staged) for c in group]
    for f in fillers:
        f()
    for hh in range(HEADS):
        s_scr[hh] = state[hh]

    @pl.when(t == pl.num_programs(1) - 1)
    def _():
        state_ref[...] = s_scr[...]


def _gdn_call(qs, k, v, col, row, zs, gnorm):
    b, l, _ = qs.shape
    rows = GDN_ROWS
    nt = l // rows
    nchunk = rows // GDN_CHUNK
    tile = lambda w: pl.BlockSpec((None, rows, w), lambda i, j: (i, j, 0))
    return pl.pallas_call(
        _gdn_kernel,
        grid=(b, nt),
        in_specs=[tile(D_GDN)] * 3 + [
            tile(LANES),
            pl.BlockSpec((None, nchunk, 2 * HEADS, GDN_CHUNK), lambda i, j: (i, j, 0, 0)),
            tile(D_GDN),
            _resident(gnorm.shape),
        ],
        out_specs=[tile(D_GDN), pl.BlockSpec((None, HEADS, DH, DH), lambda i, j: (i, 0, 0, 0))],
        out_shape=[jax.ShapeDtypeStruct((b, l, D_GDN), BF16),
                   jax.ShapeDtypeStruct((b, HEADS, DH, DH), F32)],
        scratch_shapes=[pltpu.VMEM((HEADS, DH, DH), F32)],
        compiler_params=pltpu.CompilerParams(
            dimension_semantics=("arbitrary", "arbitrary"), vmem_limit_bytes=VMEM_LIMIT),
        name="gdn",
    )(qs, k, v, col, row, zs, gnorm)


def _tail_kernel(x_ref, og_ref, sg_ref, p_ref, xs_ref, sgs_ref, ps_ref,
                 s_ref, qt_ref, kt_ref, v_ref, col_ref, zs_ref, gnorm_ref,
                 wout_ref, gff_ref, wup_ref, wdown_ref, gple_ref, wple_ref, wgate_ref, gfin_ref,
                 y_ref, ys_ref, snew_ref, ogs_scr):
    def rows_through_tail(x_ref, og, sg_ref, p_ref, y_ref, side_job=None):
        x1 = x_ref[...] + _dot(og, wout_ref[0:D_GDN, :]) + _dot(sg_ref[...], wout_ref[D_GDN:, :])
        h2 = _rms(x1, gff_ref[...]).astype(BF16)
        y_ref[...] = x1
        ff_block = D_FF // 4
        for c in range(D_FF // ff_block):
            lo = c * ff_block
            hid = jnp.maximum(_dot(h2, wup_ref[:, lo:lo + ff_block]), 0.0)
            y_ref[...] += _dot((hid * hid).astype(BF16), wdown_ref[lo:lo + ff_block, :])
            if c == 0 and side_job is not None:
                side_job()
        acc = y_ref[...]
        h3 = _rms(acc, gple_ref[...]).astype(BF16)
        gate = _sigmoid(_dot(h3, wgate_ref[...]))
        x3 = acc + _dot(p_ref[...].astype(BF16), wple_ref[...]) * gate
        y_ref[...] = _rms(x3, gfin_ref[...])

    step = pl.program_id(0)
    last = pl.num_programs(0) - 1
    assert SAMPLE_BLOCK == 2 * SAMPLE_HALF

    @pl.when(step == 0)
    def _():
        ogs_scr[...] = jnp.zeros(ogs_scr.shape, ogs_scr.dtype)

    @pl.when(step < last)
    def _():
        first_row = pl.multiple_of((step // 2) * SAMPLE_BLOCK, SAMPLE_BLOCK)
        og_rows = ogs_scr.at[pl.ds(first_row, SAMPLE_BLOCK), :]
        side_job = functools.partial(_sample_delta_step, step % 2 == 1, step * SAMPLE_HALF, s_ref, qt_ref, kt_ref,
                                     v_ref, col_ref, zs_ref, gnorm_ref, og_rows, snew_ref)
        rows_through_tail(x_ref, og_ref[...], sg_ref, p_ref, y_ref, side_job)

    @pl.when(step == last)
    def _():
        rows_through_tail(xs_ref.at[:, 0, :], ogs_scr[...].astype(BF16), sgs_ref, ps_ref.at[:, 0, :],
                          ys_ref.at[:, 0, :])


def _tail_call(x, og, sg, p, xs, sgs, ps, state, q, kt, v, col, zs, gnorm,
               wout, gff, wup, wdown, gple, wple, wgate, gfin):
    n = x.shape[0]
    ns = xs.shape[0]
    rows = TAIL_ROWS
    steps = n // rows
    blk = SAMPLE_BLOCK
    per_block = SAMPLE_BLOCK // SAMPLE_HALF
    assert ns == steps * SAMPLE_HALF
    prompt_step = lambda i: jnp.minimum(i, steps - 1)
    tile = lambda w: pl.BlockSpec((rows, w), lambda i: (prompt_step(i), 0))
    srow = lambda w: pl.BlockSpec((blk, w), lambda i: (prompt_step(i) // per_block, 0))
    sample = [xs, sgs, ps]
    weights = [wout, gff, wup, wdown, gple, wple, wgate, gfin]
    state_block = pl.BlockSpec((blk, HEADS, DH, DH), lambda i: (prompt_step(i) // per_block, 0, 0, 0))
    cols4 = _resident(q.shape)
    return pl.pallas_call(
        _tail_kernel,
        grid=(steps + 1,),
        in_specs=[tile(D_MODEL), tile(D_GDN), tile(D_SGU), tile(D_PLE)]
        + [_resident(a.shape) for a in sample]
        + [state_block, cols4, cols4, srow(D_GDN), srow(LANES), srow(D_GDN), _resident(gnorm.shape)]
        + [_resident(a.shape) for a in weights],
        out_specs=[tile(D_MODEL), pl.BlockSpec(xs.shape, lambda i: (0, 0, 0)), state_block],
        out_shape=[jax.ShapeDtypeStruct((n, D_MODEL), F32), jax.ShapeDtypeStruct(xs.shape, F32),
                   jax.ShapeDtypeStruct(state.shape, F32)],
        scratch_shapes=[pltpu.VMEM((ns, D_GDN), F32)],
        compiler_params=pltpu.CompilerParams(
            dimension_semantics=("arbitrary",), vmem_limit_bytes=VMEM_LIMIT),
        name="tail",
    )(x, og, sg, p, *sample, state, q, kt, v, col, zs, gnorm, *weights)


def _sample_proj_kernel(x_ref, hist_ref, gmix_ref, w_ref, wconv_ref,
                        alog_ref, dtb_ref, lng_ref, lnb_ref, ws_ref, bs_ref,
                        nconv_ref, vg_ref, sg_ref, zs_ref, v_ref, col_ref, qt_ref, kt_ref):
    h = _rms(x_ref[:, 0, :], gmix_ref[...]).astype(BF16)
    pre = _dot(h, w_ref[:, COL_Q:COL_Q + D_QKV])
    y = pre * wconv_ref[CONV_W - 1:CONV_W, :]
    for j in range(CONV_W - 1):
        y = y + hist_ref[j] * wconv_ref[j:j + 1, :]
        if j > 0:
            nconv_ref[j - 1] = hist_ref[j]
    nconv_ref[CONV_W - 2] = pre
    y = _silu(y)
    v_ref[...] = y[:, 2 * D_GDN:]
    for hh in range(HEADS):
        lo = hh * DH
        qt_ref[hh] = jnp.transpose(_l2n(y[:, lo:lo + DH]) * (DH ** -0.5))
        kt_ref[hh] = jnp.transpose(_l2n(y[:, D_GDN + lo:D_GDN + lo + DH]))
    zs_ref[...] = _silu(_dot(h, w_ref[:, COL_Z:COL_Z + D_GDN]))
    ab = _dot(h, w_ref[:, COL_AB:COL_AB + LANES])
    lane = lax.broadcasted_iota(jnp.int32, ab.shape, 1)
    g_col = (-jnp.exp(_decay_slots(alog_ref, (1, LANES), 1))
             * _softplus(ab + _decay_slots(dtb_ref, (1, LANES), 1)))
    col_ref[...] = jnp.where(lane < HEADS, _sigmoid(ab), g_col)
    uv = _gelu_tanh(_dot(h, w_ref[:, COL_U:COL_U + 2 * D_SGU]))
    vg = _layer_norm(uv[:, D_SGU:], lng_ref[...], lnb_ref[...])
    vg_ref[:, 0, :] = vg
    for hh in range(HEADS):
        lo = hh * DH
        mix = ws_ref[hh, 0:1, 0:1] * vg[:, lo:lo + DH] + bs_ref[hh:hh + 1, 0:1]
        sg_ref[:, lo:lo + DH] = (uv[:, lo:lo + DH] * mix).astype(BF16)


def _sample_proj_call(x, hist, gmix, w_all, wconv, alog, dtb, lng, lnb, ws, bs):
    n = x.shape[0]
    ins = [x, hist, gmix, w_all, wconv, alog, dtb, lng, lnb, ws, bs]
    vmem, smem = pl.BlockSpec(memory_space=pltpu.VMEM), pl.BlockSpec(memory_space=pltpu.SMEM)
    out_shape = [
        jax.ShapeDtypeStruct((CONV_W - 1, n, D_QKV), F32),
        jax.ShapeDtypeStruct((n, 1, D_SGU), F32),
        jax.ShapeDtypeStruct((n, D_SGU), BF16),
        jax.ShapeDtypeStruct((n, D_GDN), F32),
        jax.ShapeDtypeStruct((n, D_GDN), F32),
        jax.ShapeDtypeStruct((n, LANES), F32),
        jax.ShapeDtypeStruct((HEADS, DH, n), F32),
        jax.ShapeDtypeStruct((HEADS, DH, n), F32),
    ]
    return pl.pallas_call(
        _sample_proj_kernel,
        in_specs=[smem if a is alog or a is dtb else vmem for a in ins],
        out_shape=out_shape,
        compiler_params=pltpu.CompilerParams(vmem_limit_bytes=VMEM_LIMIT),
        name="sample_proj",
    )(*ins)


def _sample_delta_step(upper, lane0, s_ref, qt_ref, kt_ref, v_ref, col_ref, zs_ref, gnorm_ref, og_ref, snew_ref):
    gnorm = gnorm_ref[...]
    row0 = jnp.where(upper, SAMPLE_HALF, 0)
    pick = lambda blk, bb: jnp.where(upper, blk[SAMPLE_HALF + bb:SAMPLE_HALF + bb + 1], blk[bb:bb + 1])
    col_blk, v_blk, zs_blk = col_ref[...], v_ref[...], zs_ref[...]
    pairs = [(bb, hh) for bb in range(SAMPLE_HALF) for hh in range(HEADS)]
    n_seq = kt_ref.shape[-1]
    k_here = [pltpu.roll(kt_ref[hh], n_seq - lane0, axis=1) for hh in range(HEADS)]
    q_here = [pltpu.roll(qt_ref[hh], n_seq - lane0, axis=1) for hh in range(HEADS)]
    kcol = [k_here[hh][:, bb:bb + 1] for bb, hh in pairs]
    qcol = [q_here[hh][:, bb:bb + 1] for bb, hh in pairs]
    beta = [pick(col_blk[:, hh:hh + 1], bb) for bb, hh in pairs]
    eg = [jnp.exp(pick(col_blk[:, HEADS + hh:HEADS + hh + 1], bb)) for bb, hh in pairs]
    sd = [s_ref[row0 + bb, hh] * e for (bb, hh), e in zip(pairs, eg)]
    sk = [jnp.sum(s * kc, axis=0, keepdims=True) for s, kc in zip(sd, kcol)]
    v_new = [b * (pick(v_blk[:, hh * DH:(hh + 1) * DH], bb) - x) for (bb, hh), b, x in zip(pairs, beta, sk)]
    s1 = [s + kc * vn for s, kc, vn in zip(sd, kcol, v_new)]
    for (bb, hh), s in zip(pairs, s1):
        snew_ref[row0 + bb, hh] = s
    o = [jnp.sum(s * qc, axis=0, keepdims=True) for s, qc in zip(s1, qcol)]
    sub = lax.broadcasted_iota(jnp.int32, (SAMPLE_BLOCK, DH), 0)
    for hh in range(HEADS):
        lo = hh * DH
        merged = og_ref[:, lo:lo + DH]
        for bb in range(SAMPLE_HALF):
            og = _rms(o[bb * HEADS + hh], gnorm) * pick(zs_blk[:, lo:lo + DH], bb)
            merged = jnp.where(sub == row0 + bb, og.astype(og_ref.dtype), merged)
        og_ref[:, lo:lo + DH] = merged


def kernel(x_prompt, x_sample, state_conv, state_gdn, p_prompt, p_sample, g_mix, w_in, w_conv, a_log, dt_bias, gdn_norm, sgu_ln_g, sgu_ln_b, w_s, b_s, w_out, g_ff, w_up, w_down, g_ple, w_ple, w_ple_gate, g_final):
    depth = w_in.shape[0]
    assert depth == 1, "single-layer trunk only"
    b, l, d = x_prompt.shape
    n = x_sample.shape[0]
    assert x_sample.shape[1] == 1 and d == D_MODEL
    assert l % PROJ_ROWS == 0 and l % GDN_ROWS == 0 and (b * l) % TAIL_ROWS == 0 and n % SAMPLE_BLOCK == 0

    w_all, wabt = _prep_call(w_in)
    row2 = lambda a: a.reshape(1, -1)
    alog, dtb = a_log[0], dt_bias[0]
    gmix, gff, gple, gfin = row2(g_mix[0]), row2(g_ff[0]), row2(g_ple[0]), row2(g_final)
    lng, lnb, gnorm = row2(sgu_ln_g[0]), row2(sgu_ln_b[0]), row2(gdn_norm[0])

    qs, k, v, zs, sg, col, row, conv_p, wout, wup, wdown, wple, wgate = _proj_call(
        x_prompt, gmix, w_all, wabt, w_conv[0], alog, dtb,
        lng, lnb, w_s[0], b_s[0].T, (w_out, w_up, w_down, w_ple, w_ple_gate))
    tail_w = (wout, gff, wup, wdown, gple, wple, wgate, gfin)
    og, gdn_p = _gdn_call(qs, k, v, col, row, zs, gnorm)

    nconv, vg_s, sg_s, zs_s, v_s, col_s, q_s, kt = _sample_proj_call(
        x_sample, jnp.transpose(state_conv[0], (1, 0, 2)), gmix, w_all,
        w_conv[0], alog, dtb, lng, lnb, w_s[0], b_s[0])

    y_prompt, y_sample, gdn_s = _tail_call(
        x_prompt.reshape(b * l, d), og.reshape(b * l, D_GDN), sg.reshape(b * l, D_SGU),
        p_prompt[0].reshape(b * l, D_PLE),
        x_sample, sg_s, p_sample[0],
        state_gdn[0], q_s, kt, v_s, col_s, zs_s, gnorm, *tail_w)
    y_prompt = y_prompt.reshape(b, l, d)

    return (y_prompt, y_sample,
            conv_p[None], gdn_p[None],
            jnp.transpose(nconv, (1, 0, 2))[None], gdn_s[None],
            vg_s[None])
```

```python
import functools

import jax
import jax.numpy as jnp
from jax import lax
from jax.experimental import pallas as pl
from jax.experimental.pallas import tpu as pltpu

F32 = jnp.float32
BF16 = jnp.bfloat16

D_MODEL = 1024
D_GDN = 512
D_SGU = 512
HEADS = 4
DH = 128
D_QKV = 3 * D_GDN
D_FF = 4 * D_MODEL
D_PLE = 256
CONV_W = 4
GDN_CHUNK = 64
SGU_CHUNK = 128
EPS = 1e-6
LANES = 128
SUBLANES = 8
VMEM_LIMIT = 56 * 1024 * 1024

COL_Q, COL_K, COL_V, COL_Z, COL_U, COL_VV, COL_AB = 0, 512, 1024, 1536, 2048, 2560, 3072
W_COLS = COL_AB + LANES

PROJ_ROWS = 1024
CONV_BLOCK = 8
GDN_ROWS = 2048
GDN_GROUP = 4
TAIL_ROWS = 512
SAMPLE_BLOCK = 8
SAMPLE_HALF = 4
PREP_STEPS = 8


def _dot(a, b):
    return jnp.dot(a, b, preferred_element_type=F32)


def _dot_nt(a, b):
    return lax.dot_general(a, b, (((1,), (1,)), ((), ())), preferred_element_type=F32)


def _split3(x):
    hi = x.astype(BF16)
    r1 = x - hi.astype(F32)
    mid = r1.astype(BF16)
    lo = (r1 - mid.astype(F32)).astype(BF16)
    return hi, mid, lo


def _rms(x, gain):
    return x * lax.rsqrt(jnp.mean(x * x, axis=-1, keepdims=True) + EPS) * gain


def _sigmoid(x):
    return 1.0 / (1.0 + jnp.exp(-x))


def _silu(x):
    hx = 0.5 * x
    return hx + hx * jnp.tanh(hx)


def _softplus(x):
    return jnp.maximum(x, 0.0) + jnp.log(1.0 + jnp.exp(-jnp.abs(x)))


def _gelu_tanh(x):
    c = 0.7978845608028654
    hx = 0.5 * x
    return hx + hx * jnp.tanh(x * (c + (c * 0.044715) * (x * x)))


def _l2n(x):
    return x * lax.rsqrt(jnp.sum(x * x, axis=-1, keepdims=True) + 1e-6)


def _layer_norm(x, gain, bias):
    mu = jnp.mean(x, axis=-1, keepdims=True)
    xc = x - mu
    var = jnp.mean(xc * xc, axis=-1, keepdims=True)
    return xc * lax.rsqrt(var + EPS) * gain + bias


def _decay_slots(scalar_ref, shape, axis):
    idx = lax.broadcasted_iota(jnp.int32, shape, axis)
    out = jnp.zeros(shape, F32)
    for hh in range(HEADS):
        out = jnp.where(idx == HEADS + hh, scalar_ref[hh], out)
    return out


def _resident(shape):
    nd = len(shape)
    return pl.BlockSpec(shape, lambda *_: (0,) * nd, pipeline_mode=pl.Buffered(1))


def _prep_kernel(win_ref, wall_o, wabt_o):
    o1 = D_QKV + D_GDN
    o3 = o1 + 2 * HEADS

    def put(dst, src):
        wall_o[:, dst:dst + LANES] = jnp.transpose(win_ref[src:src + LANES, :]).astype(BF16)

    for j in range(o1 // LANES):
        put(j * LANES, j * LANES)
    for j in range(2 * D_SGU // LANES):
        put(COL_U + j * LANES, o3 + j * LANES)
    ab = win_ref[o1:o3, :]
    ab16 = jnp.concatenate([ab, jnp.zeros((2 * SUBLANES - 2 * HEADS, LANES), F32)], axis=0)
    wabt_o[...] = ab16.astype(BF16)
    ab128 = jnp.concatenate([ab, jnp.zeros((LANES - 2 * HEADS, LANES), F32)], axis=0)
    wall_o[:, COL_AB:] = jnp.transpose(ab128).astype(BF16)


def _prep_call(w_in):
    steps = PREP_STEPS
    d, d_in = w_in.shape[1], w_in.shape[2]
    assert d // steps == LANES
    win_t = jnp.transpose(w_in[0])
    return pl.pallas_call(
        _prep_kernel,
        grid=(steps,),
        in_specs=[pl.BlockSpec((d_in, LANES), lambda i: (0, i))],
        out_specs=[pl.BlockSpec((LANES, W_COLS), lambda i: (i, 0)),
                   pl.BlockSpec((2 * SUBLANES, LANES), lambda i: (0, i))],
        out_shape=[jax.ShapeDtypeStruct((d, W_COLS), BF16), jax.ShapeDtypeStruct((2 * SUBLANES, d), BF16)],
        compiler_params=pltpu.CompilerParams(
            dimension_semantics=("arbitrary",), vmem_limit_bytes=VMEM_LIMIT),
        name="prep",
    )(win_t)


def _proj_kernel(x_ref, gmix_ref, w_ref, wabt_ref, wconv_ref,
                 alog_ref, dtb_ref, lng_ref, lnb_ref,
                 ws_ref, bst_ref, wout_ref, wup_ref, wdown_ref, wple_ref, wgate_ref,
                 qs_ref, k_ref, v_ref, zs_ref, sg_ref, col_ref, row_ref, cstate_ref,
                 wout_o, wup_o, wdown_o, wple_o, wgate_o,
                 cbuf):
    rows = x_ref.shape[0]
    first_tile = pl.program_id(1) == 0
    wout_o[...] = wout_ref[...].astype(BF16)
    wup_o[...] = wup_ref[...].astype(BF16)
    wdown_o[...] = wdown_ref[...].astype(BF16)
    wple_o[...] = wple_ref[...].astype(BF16)
    wgate_o[...] = wgate_ref[...].astype(BF16)
    h = _rms(x_ref[...], gmix_ref[...]).astype(BF16)

    def project(lo, width=D_GDN):
        return _dot(h, w_ref[:, lo:lo + width])

    def conv_silu(pre, lo):
        cols = slice(lo, lo + D_GDN)
        groups = rows // SUBLANES
        tap = lambda j: wconv_ref[j:j + 1, cols].reshape(1, 1, D_GDN)
        shape3 = (1, SUBLANES, D_GDN)
        last_p = jnp.where(first_tile, 0.0, cbuf[0:SUBLANES, cols]).reshape(shape3)
        last_b = jnp.where(first_tile, 0.0, cbuf[SUBLANES:, cols]).reshape(shape3)
        cstate_ref[:, cols] = pre[rows - (CONV_W - 1):rows, :]
        cur = pre.reshape(groups, SUBLANES, D_GDN)
        sub = lax.broadcasted_iota(jnp.int32, (CONV_BLOCK, SUBLANES, D_GDN), 1)

        def delay(a, a_last, s):
            rolled = pltpu.roll(a, s, axis=1)
            rolled_prev = jnp.concatenate([pltpu.roll(a_last, s, axis=1), rolled[:-1]], axis=0)
            return jnp.where(sub >= s, rolled, rolled_prev)

        assert CONV_W == 4
        out = []
        for g0 in range(0, groups, CONV_BLOCK):
            p = cur[g0:g0 + CONV_BLOCK]
            dp = delay(p, last_p, 1)
            b = p * tap(1) + dp * tap(0)
            y = p * tap(3) + dp * tap(2) + delay(b, last_b, 2)
            last_p, last_b = p[CONV_BLOCK - 1:], b[CONV_BLOCK - 1:]
            out.append(_silu(y))
        cbuf[0:SUBLANES, cols] = last_p.reshape(SUBLANES, D_GDN)
        cbuf[SUBLANES:, cols] = last_b.reshape(SUBLANES, D_GDN)
        return jnp.concatenate(out, axis=0).reshape(rows, D_GDN)

    pre_vv = project(COL_VV)
    pre_u = project(COL_U)
    vg = _layer_norm(_gelu_tanh(pre_vv), lng_ref[...], lnb_ref[...]).astype(BF16)
    ab = project(COL_AB, LANES)
    pre_q = project(COL_Q)
    u_act = _gelu_tanh(pre_u)
    ti = lax.broadcasted_iota(jnp.int32, (SGU_CHUNK, SGU_CHUNK), 0)
    si = lax.broadcasted_iota(jnp.int32, (SGU_CHUNK, SGU_CHUNK), 1)
    for hh in range(HEADS):
        lo = hh * DH
        w_tril = jnp.where(ti >= si, ws_ref[hh], 0.0).astype(BF16)
        bias = bst_ref[:, hh:hh + 1]
        for c in range(rows // SGU_CHUNK):
            r0 = c * SGU_CHUNK
            mix = _dot(w_tril, vg[r0:r0 + SGU_CHUNK, lo:lo + DH]) + bias
            sg_ref[r0:r0 + SGU_CHUNK, lo:lo + DH] = (u_act[r0:r0 + SGU_CHUNK, lo:lo + DH] * mix).astype(BF16)
    pre_k = project(COL_K)

    lane = lax.broadcasted_iota(jnp.int32, (GDN_CHUNK, LANES), 1)
    g_col = (-jnp.exp(_decay_slots(alog_ref, (1, LANES), 1))
             * _softplus(ab + _decay_slots(dtb_ref, (1, LANES), 1)))
    abt = _dot_nt(wabt_ref[...], h)[0:2 * HEADS, :]
    sub = lax.broadcasted_iota(jnp.int32, (2 * HEADS, GDN_CHUNK), 0)
    g_row = (-jnp.exp(_decay_slots(alog_ref, (2 * HEADS, 1), 0))
             * _softplus(abt + _decay_slots(dtb_ref, (2 * HEADS, 1), 0)))
    ri = lax.broadcasted_iota(jnp.int32, (GDN_CHUNK, GDN_CHUNK), 0)
    ci = lax.broadcasted_iota(jnp.int32, (GDN_CHUNK, GDN_CHUNK), 1)
    tri_l = (ri >= ci).astype(BF16)
    tri_u = (ri <= ci).astype(BF16)
    beta_col = _sigmoid(ab)
    beta_row = _sigmoid(abt)
    for c in range(rows // GDN_CHUNK):
        r0 = c * GDN_CHUNK
        p0, p1, p2 = _split3(g_col[r0:r0 + GDN_CHUNK, :])
        gc = _dot(tri_l, p0) + _dot(tri_l, p1) + _dot(tri_l, p2)
        col_ref[r0:r0 + GDN_CHUNK, :] = jnp.where(lane < HEADS, beta_col[r0:r0 + GDN_CHUNK, :], gc)
        p0, p1, p2 = _split3(g_row[:, r0:r0 + GDN_CHUNK])
        gr = _dot(p0, tri_u) + _dot(p1, tri_u) + _dot(p2, tri_u)
        row_ref[c] = jnp.where(sub < HEADS, beta_row[:, r0:r0 + GDN_CHUNK], gr)

    y = conv_silu(pre_q, COL_Q)
    for hh in range(HEADS):
        lo = hh * DH
        qs_ref[:, lo:lo + DH] = (_l2n(y[:, lo:lo + DH]) * (DH ** -0.5)).astype(BF16)
    pre_v = project(COL_V)
    y = conv_silu(pre_k, COL_K)
    for hh in range(HEADS):
        lo = hh * DH
        k_ref[:, lo:lo + DH] = _l2n(y[:, lo:lo + DH]).astype(BF16)
    pre_z = project(COL_Z)
    v_ref[...] = conv_silu(pre_v, COL_V).astype(BF16)
    zs_ref[...] = _silu(pre_z).astype(BF16)


def _proj_call(x, gmix, w_all, wabt, wconv, alog, dtb, lng, lnb, ws, bst, tail_f32):
    b, l, _ = x.shape
    rows = PROJ_ROWS
    nt = l // rows
    steps = b * nt
    nchunk = rows // GDN_CHUNK
    tile = lambda w: pl.BlockSpec((None, rows, w), lambda i, j: (i, j, 0))
    ins = [x, gmix, w_all, wabt, wconv, alog, dtb, lng, lnb, ws, bst]
    smem = pl.BlockSpec(memory_space=pltpu.SMEM)
    in_specs = [tile(D_MODEL)] + [smem if a is alog or a is dtb else _resident(a.shape) for a in ins[1:]]
    in_specs += [pl.BlockSpec((None, w.shape[1] // steps, w.shape[2]), lambda i, j: (0, i * nt + j, 0))
                 for w in tail_f32]
    out_shape = [jax.ShapeDtypeStruct((b, l, D_GDN), BF16)] * 5 + [
        jax.ShapeDtypeStruct((b, l, LANES), F32),
        jax.ShapeDtypeStruct((b, l // GDN_CHUNK, 2 * HEADS, GDN_CHUNK), F32),
        jax.ShapeDtypeStruct((b, CONV_W - 1, D_QKV), F32),
    ] + [jax.ShapeDtypeStruct(w.shape[1:], BF16) for w in tail_f32]
    out_specs = [tile(D_GDN)] * 5 + [
        tile(LANES),
        pl.BlockSpec((None, nchunk, 2 * HEADS, GDN_CHUNK), lambda i, j: (i, j, 0, 0)),
        pl.BlockSpec((None, CONV_W - 1, D_QKV), lambda i, j: (i, 0, 0)),
    ] + [pl.BlockSpec((w.shape[1] // steps, w.shape[2]), lambda i, j: (i * nt + j, 0)) for w in tail_f32]
    ins = ins + list(tail_f32)
    return pl.pallas_call(
        _proj_kernel,
        grid=(b, nt),
        in_specs=in_specs,
        out_specs=out_specs,
        out_shape=out_shape,
        scratch_shapes=[pltpu.VMEM((2 * SUBLANES, D_QKV), F32)],
        compiler_params=pltpu.CompilerParams(
            dimension_semantics=("arbitrary", "arbitrary"), vmem_limit_bytes=VMEM_LIMIT),
        name="proj",
    )(*ins)


def _neumann_many(a_list):
    c = a_list[0].shape[0]
    p = [-a for a in a_list]
    pb = [x.astype(BF16) for x in p]
    q = [_dot(x, x) for x in pb]
    yield
    power = 2
    while 2 * power < c:
        qb = [x.astype(BF16) for x in q]
        r = [_dot(jnp.concatenate([pi.astype(BF16), qi], axis=0), qi) for pi, qi in zip(p, qb)]
        yield
        p = [pi + qi + ri[:c] for pi, qi, ri in zip(p, q, r)]
        q = [ri[c:] for ri in r]
        power *= 2
    tail = [_dot(pi.astype(BF16), qi.astype(BF16)) for pi, qi in zip(p, q)]
    yield
    return [pi + qi + ti for pi, qi, ti in zip(p, q, tail)]


def _interleave(steps, fillers):
    fillers = list(fillers)
    while True:
        try:
            next(steps)
        except StopIteration as stop:
            for f in fillers:
                f()
            return stop.value
        if fillers:
            fillers.pop(0)()


def _gdn_kernel(qs_ref, k_ref, v_ref, col_ref, row_ref, zs_ref, gnorm_ref,
                og_ref, state_ref, s_scr):
    rows = qs_ref.shape[0]
    c_ = GDN_CHUNK
    t = pl.program_id(1)

    @pl.when(t == 0)
    def _():
        s_scr[...] = jnp.zeros(s_scr.shape, F32)

    ri = lax.broadcasted_iota(jnp.int32, (c_, c_), 0)
    ci = lax.broadcasted_iota(jnp.int32, (c_, c_), 1)
    incl = ri >= ci
    strict = ri > ci
    gnorm = gnorm_ref[...]

    def prepare(chunks):
        pairs = [(c, hh) for c in chunks for hh in range(HEADS)]
        cols = {c: col_ref[c * c_:(c + 1) * c_, :] for c in chunks}
        rws = {c: row_ref[c] for c in chunks}
        beta = [cols[c][:, hh:hh + 1] for c, hh in pairs]
        gc = [cols[c][:, HEADS + hh:HEADS + hh + 1] for c, hh in pairs]
        gr = [rws[c][HEADS + hh:HEADS + hh + 1, :] for c, hh in pairs]
        glast = [cols[c][c_ - 1:c_, HEADS + hh:HEADS + hh + 1] for c, hh in pairs]
        kh = [k_ref[c * c_:(c + 1) * c_, hh * DH:(hh + 1) * DH] for c, hh in pairs]
        qh = [qs_ref[c * c_:(c + 1) * c_, hh * DH:(hh + 1) * DH] for c, hh in pairs]
        kf = [x.astype(F32) for x in kh]
        kb = [x * b for x, b in zip(kf, beta)]
        m1 = [_dot_nt(jnp.concatenate([x.astype(BF16), q], axis=0), k) for x, q, k in zip(kb, qh, kh)]
        yield
        decay = [jnp.exp(jnp.where(incl, a - b, -jnp.inf)) for a, b in zip(gc, gr)]
        attn = [(m[c_:] * d).astype(BF16) for m, d in zip(m1, decay)]
        p = yield from _neumann_many([jnp.where(strict, m[:c_] * d, 0.0) for m, d in zip(m1, decay)])
        eg = [jnp.exp(x) for x in gc]
        rhs = [jnp.concatenate([x * e, v_ref[c * c_:(c + 1) * c_, hh * DH:(hh + 1) * DH].astype(F32) * b], axis=1)
               for x, e, b, (c, hh) in zip(kb, eg, beta, pairs)]
        wu = [(x + _dot(pi.astype(BF16), x.astype(BF16))).astype(BF16) for x, pi in zip(rhs, p)]
        yield
        ke_t = [jnp.transpose((x * jnp.exp(gl - g)).astype(BF16)) for x, gl, g in zip(kf, glast, gc)]
        r1 = [_dot(jnp.concatenate([kt, a], axis=0), x) for kt, a, x in zip(ke_t, attn, wu)]
        yield
        staged = {}
        for i, pair in enumerate(pairs):
            q_eff = qh[i].astype(F32) * eg[i] - r1[i][DH:, :DH]
            mq = jnp.concatenate([(-r1[i][:DH, :DH]).astype(BF16), q_eff.astype(BF16)], axis=0)
            staged[pair] = (mq, r1[i][:DH, DH:], r1[i][DH:, DH:], jnp.exp(glast[i]))
        return staged

    state = [s_scr[hh] for hh in range(HEADS)]

    def advance(c, staged):
        r0 = c * c_
        for hh in range(HEADS):
            lo = hh * DH
            mq, n_mat, o_zero, e_last = staged[c, hh]
            r = _dot(mq, state[hh].astype(BF16))
            state[hh] = state[hh] * e_last + r[:DH] + n_mat
            og = _rms(r[DH:] + o_zero, gnorm) * zs_ref[r0:r0 + c_, lo:lo + DH].astype(F32)
            og_ref[r0:r0 + c_, lo:lo + DH] = og.astype(BF16)

    nchunk = rows // c_
    fillers = []
    for g0 in range(0, nchunk, GDN_GROUP):
        group = list(range(g0, g0 + GDN_GROUP))
        staged = _interleave(prepare(group), fillers)
        fillers = [functools.partial(advance, c, staged) for c in group]
    for f in fillers:
        f()
    for hh in range(HEADS):
        s_scr[hh] = state[hh]

    @pl.when(t == pl.num_programs(1) - 1)
    def _():
        state_ref[...] = s_scr[...]


def _gdn_call(qs, k, v, col, row, zs, gnorm):
    b, l, _ = qs.shape
    rows = GDN_ROWS
    nt = l // rows
    nchunk = rows // GDN_CHUNK
    tile = lambda w: pl.BlockSpec((None, rows, w), lambda i, j: (i, j, 0))
    return pl.pallas_call(
        _gdn_kernel,
        grid=(b, nt),
        in_specs=[tile(D_GDN)] * 3 + [
            tile(LANES),
            pl.BlockSpec((None, nchunk, 2 * HEADS, GDN_CHUNK), lambda i, j: (i, j, 0, 0)),
            tile(D_GDN),
            _resident(gnorm.shape),
        ],
        out_specs=[tile(D_GDN), pl.BlockSpec((None, HEADS, DH, DH), lambda i, j: (i, 0, 0, 0))],
        out_shape=[jax.ShapeDtypeStruct((b, l, D_GDN), BF16),
                   jax.ShapeDtypeStruct((b, HEADS, DH, DH), F32)],
        scratch_shapes=[pltpu.VMEM((HEADS, DH, DH), F32)],
        compiler_params=pltpu.CompilerParams(
            dimension_semantics=("arbitrary", "arbitrary"), vmem_limit_bytes=VMEM_LIMIT),
        name="gdn",
    )(qs, k, v, col, row, zs, gnorm)


def _tail_kernel(x_ref, og_ref, sg_ref, p_ref, xs_ref, sgs_ref, ps_ref,
                 s_ref, qt_ref, kt_ref, v_ref, col_ref, zs_ref, gnorm_ref,
                 wout_ref, gff_ref, wup_ref, wdown_ref, gple_ref, wple_ref, wgate_ref, gfin_ref,
                 y_ref, ys_ref, snew_ref, ogs_scr):
    def rows_through_tail(x_ref, og, sg_ref, p_ref, y_ref, side_job=None):
        x1 = x_ref[...] + _dot(og, wout_ref[0:D_GDN, :]) + _dot(sg_ref[...], wout_ref[D_GDN:, :])
        h2 = _rms(x1, gff_ref[...]).astype(BF16)
        y_ref[...] = x1
        ff_block = D_FF // 4
        for c in range(D_FF // ff_block):
            lo = c * ff_block
            hid = jnp.maximum(_dot(h2, wup_ref[:, lo:lo + ff_block]), 0.0)
            y_ref[...] += _dot((hid * hid).astype(BF16), wdown_ref[lo:lo + ff_block, :])
            if c == 0 and side_job is not None:
                side_job()
        acc = y_ref[...]
        h3 = _rms(acc, gple_ref[...]).astype(BF16)
        gate = _sigmoid(_dot(h3, wgate_ref[...]))
        x3 = acc + _dot(p_ref[...].astype(BF16), wple_ref[...]) * gate
        y_ref[...] = _rms(x3, gfin_ref[...])

    step = pl.program_id(0)
    last = pl.num_programs(0) - 1
    assert SAMPLE_BLOCK == 2 * SAMPLE_HALF

    @pl.when(step == 0)
    def _():
        ogs_scr[...] = jnp.zeros(ogs_scr.shape, ogs_scr.dtype)

    @pl.when(step < last)
    def _():
        first_row = pl.multiple_of((step // 2) * SAMPLE_BLOCK, SAMPLE_BLOCK)
        og_rows = ogs_scr.at[pl.ds(first_row, SAMPLE_BLOCK), :]
        side_job = functools.partial(_sample_delta_step, step % 2 == 1, step * SAMPLE_HALF, s_ref, qt_ref, kt_ref,
                                     v_ref, col_ref, zs_ref, gnorm_ref, og_rows, snew_ref)
        rows_through_tail(x_ref, og_ref[...], sg_ref, p_ref, y_ref, side_job)

    @pl.when(step == last)
    def _():
        rows_through_tail(xs_ref.at[:, 0, :], ogs_scr[...].astype(BF16), sgs_ref, ps_ref.at[:, 0, :],
                          ys_ref.at[:, 0, :])


def _tail_call(x, og, sg, p, xs, sgs, ps, state, q, kt, v, col, zs, gnorm,
               wout, gff, wup, wdown, gple, wple, wgate, gfin):
    n = x.shape[0]
    ns = xs.shape[0]
    rows = TAIL_ROWS
    steps = n // rows
    blk = SAMPLE_BLOCK
    per_block = SAMPLE_BLOCK // SAMPLE_HALF
    assert ns == steps * SAMPLE_HALF
    prompt_step = lambda i: jnp.minimum(i, steps - 1)
    tile = lambda w: pl.BlockSpec((rows, w), lambda i: (prompt_step(i), 0))
    srow = lambda w: pl.BlockSpec((blk, w), lambda i: (prompt_step(i) // per_block, 0))
    sample = [xs, sgs, ps]
    weights = [wout, gff, wup, wdown, gple, wple, wgate, gfin]
    state_block = pl.BlockSpec((blk, HEADS, DH, DH), lambda i: (prompt_step(i) // per_block, 0, 0, 0))
    cols4 = _resident(q.shape)
    return pl.pallas_call(
        _tail_kernel,
        grid=(steps + 1,),
        in_specs=[tile(D_MODEL), tile(D_GDN), tile(D_SGU), tile(D_PLE)]
        + [_resident(a.shape) for a in sample]
        + [state_block, cols4, cols4, srow(D_GDN), srow(LANES), srow(D_GDN), _resident(gnorm.shape)]
        + [_resident(a.shape) for a in weights],
        out_specs=[tile(D_MODEL), pl.BlockSpec(xs.shape, lambda i: (0, 0, 0)), state_block],
        out_shape=[jax.ShapeDtypeStruct((n, D_MODEL), F32), jax.ShapeDtypeStruct(xs.shape, F32),
                   jax.ShapeDtypeStruct(state.shape, F32)],
        scratch_shapes=[pltpu.VMEM((ns, D_GDN), F32)],
        compiler_params=pltpu.CompilerParams(
            dimension_semantics=("arbitrary",), vmem_limit_bytes=VMEM_LIMIT),
        name="tail",
    )(x, og, sg, p, *sample, state, q, kt, v, col, zs, gnorm, *weights)


def _sample_proj_kernel(x_ref, hist_ref, gmix_ref, w_ref, wconv_ref,
                        alog_ref, dtb_ref, lng_ref, lnb_ref, ws_ref, bs_ref,
                        nconv_ref, vg_ref, sg_ref, zs_ref, v_ref, col_ref, qt_ref, kt_ref):
    h = _rms(x_ref[:, 0, :], gmix_ref[...]).astype(BF16)
    pre = _dot(h, w_ref[:, COL_Q:COL_Q + D_QKV])
    y = pre * wconv_ref[CONV_W - 1:CONV_W, :]
    for j in range(CONV_W - 1):
        y = y + hist_ref[j] * wconv_ref[j:j + 1, :]
        if j > 0:
            nconv_ref[j - 1] = hist_ref[j]
    nconv_ref[CONV_W - 2] = pre
    y = _silu(y)
    v_ref[...] = y[:, 2 * D_GDN:]
    for hh in range(HEADS):
        lo = hh * DH
        qt_ref[hh] = jnp.transpose(_l2n(y[:, lo:lo + DH]) * (DH ** -0.5))
        kt_ref[hh] = jnp.transpose(_l2n(y[:, D_GDN + lo:D_GDN + lo + DH]))
    zs_ref[...] = _silu(_dot(h, w_ref[:, COL_Z:COL_Z + D_GDN]))
    ab = _dot(h, w_ref[:, COL_AB:COL_AB + LANES])
    lane = lax.broadcasted_iota(jnp.int32, ab.shape, 1)
    g_col = (-jnp.exp(_decay_slots(alog_ref, (1, LANES), 1))
             * _softplus(ab + _decay_slots(dtb_ref, (1, LANES), 1)))
    col_ref[...] = jnp.where(lane < HEADS, _sigmoid(ab), g_col)
    uv = _gelu_tanh(_dot(h, w_ref[:, COL_U:COL_U + 2 * D_SGU]))
    vg = _layer_norm(uv[:, D_SGU:], lng_ref[...], lnb_ref[...])
    vg_ref[:, 0, :] = vg
    for hh in range(HEADS):
        lo = hh * DH
        mix = ws_ref[hh, 0:1, 0:1] * vg[:, lo:lo + DH] + bs_ref[hh:hh + 1, 0:1]
        sg_ref[:, lo:lo + DH] = (uv[:, lo:lo + DH] * mix).astype(BF16)


def _sample_proj_call(x, hist, gmix, w_all, wconv, alog, dtb, lng, lnb, ws, bs):
    n = x.shape[0]
    ins = [x, hist, gmix, w_all, wconv, alog, dtb, lng, lnb, ws, bs]
    vmem, smem = pl.BlockSpec(memory_space=pltpu.VMEM), pl.BlockSpec(memory_space=pltpu.SMEM)
    out_shape = [
        jax.ShapeDtypeStruct((CONV_W - 1, n, D_QKV), F32),
        jax.ShapeDtypeStruct((n, 1, D_SGU), F32),
        jax.ShapeDtypeStruct((n, D_SGU), BF16),
        jax.ShapeDtypeStruct((n, D_GDN), F32),
        jax.ShapeDtypeStruct((n, D_GDN), F32),
        jax.ShapeDtypeStruct((n, LANES), F32),
        jax.ShapeDtypeStruct((HEADS, DH, n), F32),
        jax.ShapeDtypeStruct((HEADS, DH, n), F32),
    ]
    return pl.pallas_call(
        _sample_proj_kernel,
        in_specs=[smem if a is alog or a is dtb else vmem for a in ins],
        out_shape=out_shape,
        compiler_params=pltpu.CompilerParams(vmem_limit_bytes=VMEM_LIMIT),
        name="sample_proj",
    )(*ins)


def _sample_delta_step(upper, lane0, s_ref, qt_ref, kt_ref, v_ref, col_ref, zs_ref, gnorm_ref, og_ref, snew_ref):
    gnorm = gnorm_ref[...]
    row0 = jnp.where(upper, SAMPLE_HALF, 0)
    pick = lambda blk, bb: jnp.where(upper, blk[SAMPLE_HALF + bb:SAMPLE_HALF + bb + 1], blk[bb:bb + 1])
    col_blk, v_blk, zs_blk = col_ref[...], v_ref[...], zs_ref[...]
    pairs = [(bb, hh) for bb in range(SAMPLE_HALF) for hh in range(HEADS)]
    n_seq = kt_ref.shape[-1]
    k_here = [pltpu.roll(kt_ref[hh], n_seq - lane0, axis=1) for hh in range(HEADS)]
    q_here = [pltpu.roll(qt_ref[hh], n_seq - lane0, axis=1) for hh in range(HEADS)]
    kcol = [k_here[hh][:, bb:bb + 1] for bb, hh in pairs]
    qcol = [q_here[hh][:, bb:bb + 1] for bb, hh in pairs]
    beta = [pick(col_blk[:, hh:hh + 1], bb) for bb, hh in pairs]
    eg = [jnp.exp(pick(col_blk[:, HEADS + hh:HEADS + hh + 1], bb)) for bb, hh in pairs]
    sd = [s_ref[row0 + bb, hh] * e for (bb, hh), e in zip(pairs, eg)]
    sk = [jnp.sum(s * kc, axis=0, keepdims=True) for s, kc in zip(sd, kcol)]
    v_new = [b * (pick(v_blk[:, hh * DH:(hh + 1) * DH], bb) - x) for (bb, hh), b, x in zip(pairs, beta, sk)]
    s1 = [s + kc * vn for s, kc, vn in zip(sd, kcol, v_new)]
    for (bb, hh), s in zip(pairs, s1):
        snew_ref[row0 + bb, hh] = s
    o = [jnp.sum(s * qc, axis=0, keepdims=True) for s, qc in zip(s1, qcol)]
    sub = lax.broadcasted_iota(jnp.int32, (SAMPLE_BLOCK, DH), 0)
    for hh in range(HEADS):
        lo = hh * DH
        merged = og_ref[:, lo:lo + DH]
        for bb in range(SAMPLE_HALF):
            og = _rms(o[bb * HEADS + hh], gnorm) * pick(zs_blk[:, lo:lo + DH], bb)
            merged = jnp.where(sub == row0 + bb, og.astype(og_ref.dtype), merged)
        og_ref[:, lo:lo + DH] = merged


def kernel(x_prompt, x_sample, state_conv, state_gdn, p_prompt, p_sample, g_mix, w_in, w_conv, a_log, dt_bias, gdn_norm, sgu_ln_g, sgu_ln_b, w_s, b_s, w_out, g_ff, w_up, w_down, g_ple, w_ple, w_ple_gate, g_final):
    depth = w_in.shape[0]
    assert depth == 1, "single-layer trunk only"
    b, l, d = x_prompt.shape
    n = x_sample.shape[0]
    assert x_sample.shape[1] == 1 and d == D_MODEL
    assert l % PROJ_ROWS == 0 and l % GDN_ROWS == 0 and (b * l) % TAIL_ROWS == 0 and n % SAMPLE_BLOCK == 0

    w_all, wabt = _prep_call(w_in)
    row2 = lambda a: a.reshape(1, -1)
    alog, dtb = a_log[0], dt_bias[0]
    gmix, gff, gple, gfin = row2(g_mix[0]), row2(g_ff[0]), row2(g_ple[0]), row2(g_final)
    lng, lnb, gnorm = row2(sgu_ln_g[0]), row2(sgu_ln_b[0]), row2(gdn_norm[0])

    qs, k, v, zs, sg, col, row, conv_p, wout, wup, wdown, wple, wgate = _proj_call(
        x_prompt, gmix, w_all, wabt, w_conv[0], alog, dtb,
        lng, lnb, w_s[0], b_s[0].T, (w_out, w_up, w_down, w_ple, w_ple_gate))
    tail_w = (wout, gff, wup, wdown, gple, wple, wgate, gfin)
    og, gdn_p = _gdn_call(qs, k, v, col, row, zs, gnorm)

    nconv, vg_s, sg_s, zs_s, v_s, col_s, q_s, kt = _sample_proj_call(
        x_sample, jnp.transpose(state_conv[0], (1, 0, 2)), gmix, w_all,
        w_conv[0], alog, dtb, lng, lnb, w_s[0], b_s[0])

    y_prompt, y_sample, gdn_s = _tail_call(
        x_prompt.reshape(b * l, d), og.reshape(b * l, D_GDN), sg.reshape(b * l, D_SGU),
        p_prompt[0].reshape(b * l, D_PLE),
        x_sample, sg_s, p_sample[0],
        state_gdn[0], q_s, kt, v_s, col_s, zs_s, gnorm, *tail_w)
    y_prompt = y_prompt.reshape(b, l, d)

    return (y_prompt, y_sample,
            conv_p[None], gdn_p[None],
            jnp.transpose(nconv, (1, 0, 2))[None], gdn_s[None],
            vg_s[None])
```

```python
import functools

import jax
import jax.numpy as jnp
from jax import lax
from jax.experimental import pallas as pl
from jax.experimental.pallas import tpu as pltpu

F32 = jnp.float32
BF16 = jnp.bfloat16

D_MODEL = 1024
D_GDN = 512
D_SGU = 512
HEADS = 4
DH = 128
D_QKV = 3 * D_GDN
D_FF = 4 * D_MODEL
D_PLE = 256
CONV_W = 4
GDN_CHUNK = 64
SGU_CHUNK = 128
EPS = 1e-6
LANES = 128
SUBLANES = 8
VMEM_LIMIT = 56 * 1024 * 1024

COL_Q, COL_K, COL_V, COL_Z, COL_U, COL_VV, COL_AB = 0, 512, 1024, 1536, 2048, 2560, 3072
W_COLS = COL_AB + LANES

PROJ_ROWS = 1024
CONV_BLOCK = 8
GDN_ROWS = 2048
GDN_GROUP = 4
TAIL_ROWS = 512
SAMPLE_BLOCK = 8
SAMPLE_HALF = 4
PREP_STEPS = 8


def _dot(a, b):
    return jnp.dot(a, b, preferred_element_type=F32)


def _dot_nt(a, b):
    return lax.dot_general(a, b, (((1,), (1,)), ((), ())), preferred_element_type=F32)


def _split3(x):
    hi = x.astype(BF16)
    r1 = x - hi.astype(F32)
    mid = r1.astype(BF16)
    lo = (r1 - mid.astype(F32)).astype(BF16)
    return hi, mid, lo


def _rms(x, gain):
    return x * lax.rsqrt(jnp.mean(x * x, axis=-1, keepdims=True) + EPS) * gain


def _unit_rms(x):
    return x * lax.rsqrt(jnp.mean(x * x, axis=-1, keepdims=True) + EPS)


def _sigmoid(x):
    return 1.0 / (1.0 + jnp.exp(-x))


def _silu(x):
    hx = 0.5 * x
    return hx + hx * jnp.tanh(hx)


def _softplus(x):
    return jnp.maximum(x, 0.0) + jnp.log(1.0 + jnp.exp(-jnp.abs(x)))


def _gelu_tanh(x):
    c = 0.7978845608028654
    hx = 0.5 * x
    return hx + hx * jnp.tanh(x * (c + (c * 0.044715) * (x * x)))


def _l2n(x, scale=1.0):
    inv = lax.rsqrt(jnp.sum(x * x, axis=-1, keepdims=True) + 1e-6)
    return x * (inv * scale if scale != 1.0 else inv)


def _layer_norm(x, gain, bias):
    mu = jnp.mean(x, axis=-1, keepdims=True)
    xc = x - mu
    var = jnp.mean(xc * xc, axis=-1, keepdims=True)
    return xc * lax.rsqrt(var + EPS) * gain + bias


def _decay_slots(scalar_ref, shape, axis):
    idx = lax.broadcasted_iota(jnp.int32, shape, axis)
    out = jnp.zeros(shape, F32)
    for hh in range(HEADS):
        out = jnp.where(idx == HEADS + hh, scalar_ref[hh], out)
    return out


def _resident(shape):
    nd = len(shape)
    return pl.BlockSpec(shape, lambda *_: (0,) * nd, pipeline_mode=pl.Buffered(1))


def _prep_kernel(win_ref, gmix_ref, wall_o, wabt_o):
    o1 = D_QKV + D_GDN
    o3 = o1 + 2 * HEADS
    gain = gmix_ref[...]

    def put(dst, src):
        wall_o[:, dst:dst + LANES] = jnp.transpose(win_ref[src:src + LANES, :] * gain).astype(BF16)

    for j in range(o1 // LANES):
        put(j * LANES, j * LANES)
    for j in range(2 * D_SGU // LANES):
        put(COL_U + j * LANES, o3 + j * LANES)
    ab = win_ref[o1:o3, :] * gain
    ab16 = jnp.concatenate([ab, jnp.zeros((2 * SUBLANES - 2 * HEADS, LANES), F32)], axis=0)
    wabt_o[...] = ab16.astype(BF16)
    ab128 = jnp.concatenate([ab, jnp.zeros((LANES - 2 * HEADS, LANES), F32)], axis=0)
    wall_o[:, COL_AB:] = jnp.transpose(ab128).astype(BF16)


def _prep_call(w_in, gmix):
    steps = PREP_STEPS
    d, d_in = w_in.shape[1], w_in.shape[2]
    assert d // steps == LANES
    win_t = jnp.transpose(w_in[0])
    return pl.pallas_call(
        _prep_kernel,
        grid=(steps,),
        in_specs=[pl.BlockSpec((d_in, LANES), lambda i: (0, i)), pl.BlockSpec((1, LANES), lambda i: (0, i))],
        out_specs=[pl.BlockSpec((LANES, W_COLS), lambda i: (i, 0)),
                   pl.BlockSpec((2 * SUBLANES, LANES), lambda i: (0, i))],
        out_shape=[jax.ShapeDtypeStruct((d, W_COLS), BF16), jax.ShapeDtypeStruct((2 * SUBLANES, d), BF16)],
        compiler_params=pltpu.CompilerParams(
            dimension_semantics=("arbitrary",), vmem_limit_bytes=VMEM_LIMIT),
        name="prep",
    )(win_t, gmix)


def _proj_kernel(x_ref, w_ref, wabt_ref, wconv_ref,
                 alog_ref, dtb_ref, lng_ref, lnb_ref,
                 ws_ref, bst_ref, wout_ref, wup_ref, wdown_ref, wple_ref, wgate_ref,
                 qs_ref, k_ref, v_ref, zs_ref, sg_ref, col_ref, row_ref, cstate_ref,
                 wout_o, wup_o, wdown_o, wple_o, wgate_o,
                 cbuf):
    rows = x_ref.shape[0]
    first_tile = pl.program_id(1) == 0
    wout_o[...] = wout_ref[...].astype(BF16)
    wup_o[...] = wup_ref[...].astype(BF16)
    wdown_o[...] = wdown_ref[...].astype(BF16)
    wple_o[...] = wple_ref[...].astype(BF16)
    wgate_o[...] = wgate_ref[...].astype(BF16)
    h = _unit_rms(x_ref[...]).astype(BF16)

    def project(lo, width=D_GDN):
        return _dot(h, w_ref[:, lo:lo + width])

    def conv_silu(pre, lo):
        cols = slice(lo, lo + D_GDN)
        groups = rows // SUBLANES
        tap = lambda j: wconv_ref[j:j + 1, cols].reshape(1, 1, D_GDN)
        shape3 = (1, SUBLANES, D_GDN)
        last_p = jnp.where(first_tile, 0.0, cbuf[0:SUBLANES, cols]).reshape(shape3)
        last_b = jnp.where(first_tile, 0.0, cbuf[SUBLANES:, cols]).reshape(shape3)
        cstate_ref[:, cols] = pre[rows - (CONV_W - 1):rows, :]
        cur = pre.reshape(groups, SUBLANES, D_GDN)
        sub = lax.broadcasted_iota(jnp.int32, (CONV_BLOCK, SUBLANES, D_GDN), 1)

        def delay(a, a_last, s):
            rolled = pltpu.roll(a, s, axis=1)
            rolled_prev = jnp.concatenate([pltpu.roll(a_last, s, axis=1), rolled[:-1]], axis=0)
            return jnp.where(sub >= s, rolled, rolled_prev)

        assert CONV_W == 4
        out = []
        for g0 in range(0, groups, CONV_BLOCK):
            p = cur[g0:g0 + CONV_BLOCK]
            dp = delay(p, last_p, 1)
            b = p * tap(1) + dp * tap(0)
            y = p * tap(3) + dp * tap(2) + delay(b, last_b, 2)
            last_p, last_b = p[CONV_BLOCK - 1:], b[CONV_BLOCK - 1:]
            out.append(_silu(y))
        cbuf[0:SUBLANES, cols] = last_p.reshape(SUBLANES, D_GDN)
        cbuf[SUBLANES:, cols] = last_b.reshape(SUBLANES, D_GDN)
        return jnp.concatenate(out, axis=0).reshape(rows, D_GDN)

    pre_vv = project(COL_VV)
    pre_u = project(COL_U)
    vg = _layer_norm(_gelu_tanh(pre_vv), lng_ref[...], lnb_ref[...]).astype(BF16)
    ab = project(COL_AB, LANES)
    pre_q = project(COL_Q)
    u_act = _gelu_tanh(pre_u)
    ti = lax.broadcasted_iota(jnp.int32, (SGU_CHUNK, SGU_CHUNK), 0)
    si = lax.broadcasted_iota(jnp.int32, (SGU_CHUNK, SGU_CHUNK), 1)
    for hh in range(HEADS):
        lo = hh * DH
        w_tril = jnp.where(ti >= si, ws_ref[hh], 0.0).astype(BF16)
        bias = bst_ref[:, hh:hh + 1]
        for c in range(rows // SGU_CHUNK):
            r0 = c * SGU_CHUNK
            mix = _dot(w_tril, vg[r0:r0 + SGU_CHUNK, lo:lo + DH]) + bias
            sg_ref[r0:r0 + SGU_CHUNK, lo:lo + DH] = (u_act[r0:r0 + SGU_CHUNK, lo:lo + DH] * mix).astype(BF16)
    pre_k = project(COL_K)

    lane = lax.broadcasted_iota(jnp.int32, (GDN_CHUNK, LANES), 1)
    g_col = (-jnp.exp(_decay_slots(alog_ref, (1, LANES), 1))
             * _softplus(ab + _decay_slots(dtb_ref, (1, LANES), 1)))
    abt = _dot_nt(wabt_ref[...], h)[0:2 * HEADS, :]
    sub = lax.broadcasted_iota(jnp.int32, (2 * HEADS, GDN_CHUNK), 0)
    g_row = (-jnp.exp(_decay_slots(alog_ref, (2 * HEADS, 1), 0))
             * _softplus(abt + _decay_slots(dtb_ref, (2 * HEADS, 1), 0)))
    ri = lax.broadcasted_iota(jnp.int32, (GDN_CHUNK, GDN_CHUNK), 0)
    ci = lax.broadcasted_iota(jnp.int32, (GDN_CHUNK, GDN_CHUNK), 1)
    tri_l = (ri >= ci).astype(BF16)
    tri_u = (ri <= ci).astype(BF16)
    beta_col = _sigmoid(ab)
    beta_row = _sigmoid(abt)
    for c in range(rows // GDN_CHUNK):
        r0 = c * GDN_CHUNK
        p0, p1, p2 = _split3(g_col[r0:r0 + GDN_CHUNK, :])
        gc = _dot(tri_l, p0) + _dot(tri_l, p1) + _dot(tri_l, p2)
        col_ref[r0:r0 + GDN_CHUNK, :] = jnp.where(lane < HEADS, beta_col[r0:r0 + GDN_CHUNK, :], gc)
        p0, p1, p2 = _split3(g_row[:, r0:r0 + GDN_CHUNK])
        gr = _dot(p0, tri_u) + _dot(p1, tri_u) + _dot(p2, tri_u)
        row_ref[c] = jnp.where(sub < HEADS, beta_row[:, r0:r0 + GDN_CHUNK], gr)

    y = conv_silu(pre_q, COL_Q)
    for hh in range(HEADS):
        lo = hh * DH
        qs_ref[:, lo:lo + DH] = _l2n(y[:, lo:lo + DH], DH ** -0.5).astype(BF16)
    pre_v = project(COL_V)
    y = conv_silu(pre_k, COL_K)
    for hh in range(HEADS):
        lo = hh * DH
        k_ref[:, lo:lo + DH] = _l2n(y[:, lo:lo + DH]).astype(BF16)
    pre_z = project(COL_Z)
    v_ref[...] = conv_silu(pre_v, COL_V).astype(BF16)
    zs_ref[...] = _silu(pre_z).astype(BF16)


def _proj_call(x, w_all, wabt, wconv, alog, dtb, lng, lnb, ws, bst, tail_f32):
    b, l, _ = x.shape
    rows = PROJ_ROWS
    nt = l // rows
    steps = b * nt
    nchunk = rows // GDN_CHUNK
    tile = lambda w: pl.BlockSpec((None, rows, w), lambda i, j: (i, j, 0))
    ins = [x, w_all, wabt, wconv, alog, dtb, lng, lnb, ws, bst]
    smem = pl.BlockSpec(memory_space=pltpu.SMEM)
    in_specs = [tile(D_MODEL)] + [smem if a is alog or a is dtb else _resident(a.shape) for a in ins[1:]]
    in_specs += [pl.BlockSpec((None, w.shape[1] // steps, w.shape[2]), lambda i, j: (0, i * nt + j, 0))
                 for w in tail_f32]
    out_shape = [jax.ShapeDtypeStruct((b, l, D_GDN), BF16)] * 5 + [
        jax.ShapeDtypeStruct((b, l, LANES), F32),
        jax.ShapeDtypeStruct((b, l // GDN_CHUNK, 2 * HEADS, GDN_CHUNK), F32),
        jax.ShapeDtypeStruct((b, CONV_W - 1, D_QKV), F32),
    ] + [jax.ShapeDtypeStruct(w.shape[1:], BF16) for w in tail_f32]
    out_specs = [tile(D_GDN)] * 5 + [
        tile(LANES),
        pl.BlockSpec((None, nchunk, 2 * HEADS, GDN_CHUNK), lambda i, j: (i, j, 0, 0)),
        pl.BlockSpec((None, CONV_W - 1, D_QKV), lambda i, j: (i, 0, 0)),
    ] + [pl.BlockSpec((w.shape[1] // steps, w.shape[2]), lambda i, j: (i * nt + j, 0)) for w in tail_f32]
    ins = ins + list(tail_f32)
    return pl.pallas_call(
        _proj_kernel,
        grid=(b, nt),
        in_specs=in_specs,
        out_specs=out_specs,
        out_shape=out_shape,
        scratch_shapes=[pltpu.VMEM((2 * SUBLANES, D_QKV), F32)],
        compiler_params=pltpu.CompilerParams(
            dimension_semantics=("arbitrary", "arbitrary"), vmem_limit_bytes=VMEM_LIMIT),
        name="proj",
    )(*ins)


def _neumann_many(a_list):
    c = a_list[0].shape[0]
    p = [-a for a in a_list]
    pb = [x.astype(BF16) for x in p]
    q = [_dot(x, x) for x in pb]
    yield
    power = 2
    while 2 * power < c:
        qb = [x.astype(BF16) for x in q]
        r = [_dot(jnp.concatenate([pi.astype(BF16), qi], axis=0), qi) for pi, qi in zip(p, qb)]
        yield
        p = [pi + qi + ri[:c] for pi, qi, ri in zip(p, q, r)]
        q = [ri[c:] for ri in r]
        power *= 2
    tail = [_dot(pi.astype(BF16), qi.astype(BF16)) for pi, qi in zip(p, q)]
    yield
    return [pi + qi + ti for pi, qi, ti in zip(p, q, tail)]


def _interleave(steps, fillers):
    fillers = list(fillers)
    while True:
        try:
            next(steps)
        except StopIteration as stop:
            for f in fillers:
                f()
            return stop.value
        if fillers:
            fillers.pop(0)()


def _gdn_kernel(qs_ref, k_ref, v_ref, col_ref, row_ref, zs_ref, gnorm_ref,
                og_ref, state_ref, s_scr):
    rows = qs_ref.shape[0]
    c_ = GDN_CHUNK
    t = pl.program_id(1)

    @pl.when(t == 0)
    def _():
        s_scr[...] = jnp.zeros(s_scr.shape, F32)

    ri = lax.broadcasted_iota(jnp.int32, (c_, c_), 0)
    ci = lax.broadcasted_iota(jnp.int32, (c_, c_), 1)
    incl = ri >= ci
    strict = ri > ci
    gnorm = gnorm_ref[...]

    def prepare(chunks):
        pairs = [(c, hh) for c in chunks for hh in range(HEADS)]
        cols = {c: col_ref[c * c_:(c + 1) * c_, :] for c in chunks}
        rws = {c: row_ref[c] for c in chunks}
        beta = [cols[c][:, hh:hh + 1] for c, hh in pairs]
        gc = [cols[c][:, HEADS + hh:HEADS + hh + 1] for c, hh in pairs]
        gr = [rws[c][HEADS + hh:HEADS + hh + 1, :] for c, hh in pairs]
        glast = [cols[c][c_ - 1:c_, HEADS + hh:HEADS + hh + 1] for c, hh in pairs]
        kh = [k_ref[c * c_:(c + 1) * c_, hh * DH:(hh + 1) * DH] for c, hh in pairs]
        qh = [qs_ref[c * c_:(c + 1) * c_, hh * DH:(hh + 1) * DH] for c, hh in pairs]
        kf = [x.astype(F32) for x in kh]
        kb = [x * b for x, b in zip(kf, beta)]
        m1 = [_dot_nt(jnp.concatenate([x.astype(BF16), q], axis=0), k) for x, q, k in zip(kb, qh, kh)]
        yield
        decay = [jnp.exp(jnp.where(incl, a - b, -jnp.inf)) for a, b in zip(gc, gr)]
        attn = [(m[c_:] * d).astype(BF16) for m, d in zip(m1, decay)]
        p = yield from _neumann_many([jnp.where(strict, m[:c_] * d, 0.0) for m, d in zip(m1, decay)])
        eg = [jnp.exp(x) for x in gc]
        rhs = [jnp.concatenate([x * e, v_ref[c * c_:(c + 1) * c_, hh * DH:(hh + 1) * DH].astype(F32) * b], axis=1)
               for x, e, b, (c, hh) in zip(kb, eg, beta, pairs)]
        wu = [(x + _dot(pi.astype(BF16), x.astype(BF16))).astype(BF16) for x, pi in zip(rhs, p)]
        yield
        ke_t = [jnp.transpose((x * jnp.exp(gl - g)).astype(BF16)) for x, gl, g in zip(kf, glast, gc)]
        r1 = [_dot(jnp.concatenate([kt, a], axis=0), x) for kt, a, x in zip(ke_t, attn, wu)]
        yield
        staged = {}
        for i, pair in enumerate(pairs):
            q_eff = qh[i].astype(F32) * eg[i] - r1[i][DH:, :DH]
            mq = jnp.concatenate([(-r1[i][:DH, :DH]).astype(BF16), q_eff.astype(BF16)], axis=0)
            staged[pair] = (mq, r1[i][:DH, DH:], r1[i][DH:, DH:], jnp.exp(glast[i]))
        return staged

    state = [s_scr[hh] for hh in range(HEADS)]

    def advance(c, staged):
        r0 = c * c_
        for hh in range(HEADS):
            lo = hh * DH
            mq, n_mat, o_zero, e_last = staged[c, hh]
            r = _dot(mq, state[hh].astype(BF16))
            state[hh] = state[hh] * e_last + r[:DH] + n_mat
            og = _rms(r[DH:] + o_zero, gnorm) * zs_ref[r0:r0 + c_, lo:lo + DH].astype(F32)
            og_ref[r0:r0 + c_, lo:lo + DH] = og.astype(BF16)

    nchunk = rows // c_
    fillers = []
    for g0 in range(0, nchunk, GDN_GROUP):
        group = list(range(g0, g0 + GDN_GROUP))
        staged = _interleave(prepare(group), fillers)
        fillers = [functools.partial(advance, c, staged) for c in group]
    for f in fillers:
        f()
    for hh in range(HEADS):
        s_scr[hh] = state[hh]

    @pl.when(t == pl.num_programs(1) - 1)
    def _():
        state_ref[...] = s_scr[...]


def _gdn_call(qs, k, v, col, row, zs, gnorm):
    b, l, _ = qs.shape
    rows = GDN_ROWS
    nt = l // rows
    nchunk = rows // GDN_CHUNK
    tile = lambda w: pl.BlockSpec((None, rows, w), lambda i, j: (i, j, 0))
    return pl.pallas_call(
        _gdn_kernel,
        grid=(b, nt),
        in_specs=[tile(D_GDN)] * 3 + [
            tile(LANES),
            pl.BlockSpec((None, nchunk, 2 * HEADS, GDN_CHUNK), lambda i, j: (i, j, 0, 0)),
            tile(D_GDN),
            _resident(gnorm.shape),
        ],
        out_specs=[tile(D_GDN), pl.BlockSpec((None, HEADS, DH, DH), lambda i, j: (i, 0, 0, 0))],
        out_shape=[jax.ShapeDtypeStruct((b, l, D_GDN), BF16),
                   jax.ShapeDtypeStruct((b, HEADS, DH, DH), F32)],
        scratch_shapes=[pltpu.VMEM((HEADS, DH, DH), F32)],
        compiler_params=pltpu.CompilerParams(
            dimension_semantics=("arbitrary", "arbitrary"), vmem_limit_bytes=VMEM_LIMIT),
        name="gdn",
    )(qs, k, v, col, row, zs, gnorm)


def _tail_kernel(x_ref, og_ref, sg_ref, p_ref, xs_ref, sgs_ref, ps_ref,
                 s_ref, qt_ref, kt_ref, v_ref, col_ref, zs_ref, gnorm_ref,
                 wout_ref, gff_ref, wup_ref, wdown_ref, gple_ref, wple_ref, wgate_ref, gfin_ref,
                 y_ref, ys_ref, snew_ref, ogs_scr):
    def rows_through_tail(x_ref, og, sg_ref, p_ref, y_ref, side_job=None):
        x1 = x_ref[...] + _dot(og, wout_ref[0:D_GDN, :]) + _dot(sg_ref[...], wout_ref[D_GDN:, :])
        h2 = _rms(x1, gff_ref[...]).astype(BF16)
        y_ref[...] = x1
        ff_block = D_FF // 4
        for c in range(D_FF // ff_block):
            lo = c * ff_block
            hid = jnp.maximum(_dot(h2, wup_ref[:, lo:lo + ff_block]), 0.0)
            y_ref[...] += _dot((hid * hid).astype(BF16), wdown_ref[lo:lo + ff_block, :])
            if c == 0 and side_job is not None:
                side_job()
        acc = y_ref[...]
        h3 = _rms(acc, gple_ref[...]).astype(BF16)
        gate = _sigmoid(_dot(h3, wgate_ref[...]))
        x3 = acc + _dot(p_ref[...].astype(BF16), wple_ref[...]) * gate
        y_ref[...] = _rms(x3, gfin_ref[...])

    step = pl.program_id(0)
    last = pl.num_programs(0) - 1
    assert SAMPLE_BLOCK == 2 * SAMPLE_HALF

    @pl.when(step == 0)
    def _():
        ogs_scr[...] = jnp.zeros(ogs_scr.shape, ogs_scr.dtype)

    @pl.when(step < last)
    def _():
        first_row = pl.multiple_of((step // 2) * SAMPLE_BLOCK, SAMPLE_BLOCK)
        og_rows = ogs_scr.at[pl.ds(first_row, SAMPLE_BLOCK), :]
        side_job = functools.partial(_sample_delta_step, step % 2 == 1, step * SAMPLE_HALF, s_ref, qt_ref, kt_ref,
                                     v_ref, col_ref, zs_ref, gnorm_ref, og_rows, snew_ref)
        rows_through_tail(x_ref, og_ref[...], sg_ref, p_ref, y_ref, side_job)

    @pl.when(step == last)
    def _():
        rows_through_tail(xs_ref.at[:, 0, :], ogs_scr[...].astype(BF16), sgs_ref, ps_ref.at[:, 0, :],
                          ys_ref.at[:, 0, :])


def _tail_call(x, og, sg, p, xs, sgs, ps, state, q, kt, v, col, zs, gnorm,
               wout, gff, wup, wdown, gple, wple, wgate, gfin):
    n = x.shape[0]
    ns = xs.shape[0]
    rows = TAIL_ROWS
    steps = n // rows
    blk = SAMPLE_BLOCK
    per_block = SAMPLE_BLOCK // SAMPLE_HALF
    assert ns == steps * SAMPLE_HALF
    prompt_step = lambda i: jnp.minimum(i, steps - 1)
    tile = lambda w: pl.BlockSpec((rows, w), lambda i: (prompt_step(i), 0))
    srow = lambda w: pl.BlockSpec((blk, w), lambda i: (prompt_step(i) // per_block, 0))
    sample = [xs, sgs, ps]
    weights = [wout, gff, wup, wdown, gple, wple, wgate, gfin]
    state_block = pl.BlockSpec((blk, HEADS, DH, DH), lambda i: (prompt_step(i) // per_block, 0, 0, 0))
    cols4 = _resident(q.shape)
    return pl.pallas_call(
        _tail_kernel,
        grid=(steps + 1,),
        in_specs=[tile(D_MODEL), tile(D_GDN), tile(D_SGU), tile(D_PLE)]
        + [_resident(a.shape) for a in sample]
        + [state_block, cols4, cols4, srow(D_GDN), srow(LANES), srow(D_GDN), _resident(gnorm.shape)]
        + [_resident(a.shape) for a in weights],
        out_specs=[tile(D_MODEL), pl.BlockSpec(xs.shape, lambda i: (0, 0, 0)), state_block],
        out_shape=[jax.ShapeDtypeStruct((n, D_MODEL), F32), jax.ShapeDtypeStruct(xs.shape, F32),
                   jax.ShapeDtypeStruct(state.shape, F32)],
        scratch_shapes=[pltpu.VMEM((ns, D_GDN), F32)],
        compiler_params=pltpu.CompilerParams(
            dimension_semantics=("arbitrary",), vmem_limit_bytes=VMEM_LIMIT),
        name="tail",
    )(x, og, sg, p, *sample, state, q, kt, v, col, zs, gnorm, *weights)


def _sample_proj_kernel(x_ref, hist_ref, w_ref, wconv_ref,
                        alog_ref, dtb_ref, lng_ref, lnb_ref, ws_ref, bs_ref,
                        nconv_ref, vg_ref, sg_ref, zs_ref, v_ref, col_ref, qt_ref, kt_ref):
    h = _unit_rms(x_ref[:, 0, :]).astype(BF16)
    pre = _dot(h, w_ref[:, COL_Q:COL_Q + D_QKV])
    y = pre * wconv_ref[CONV_W - 1:CONV_W, :]
    for j in range(CONV_W - 1):
        y = y + hist_ref[j] * wconv_ref[j:j + 1, :]
        if j > 0:
            nconv_ref[j - 1] = hist_ref[j]
    nconv_ref[CONV_W - 2] = pre
    y = _silu(y)
    v_ref[...] = y[:, 2 * D_GDN:]
    for hh in range(HEADS):
        lo = hh * DH
        qt_ref[hh] = jnp.transpose(_l2n(y[:, lo:lo + DH], DH ** -0.5))
        kt_ref[hh] = jnp.transpose(_l2n(y[:, D_GDN + lo:D_GDN + lo + DH]))
    zs_ref[...] = _silu(_dot(h, w_ref[:, COL_Z:COL_Z + D_GDN]))
    ab = _dot(h, w_ref[:, COL_AB:COL_AB + LANES])
    lane = lax.broadcasted_iota(jnp.int32, ab.shape, 1)
    g_col = (-jnp.exp(_decay_slots(alog_ref, (1, LANES), 1))
             * _softplus(ab + _decay_slots(dtb_ref, (1, LANES), 1)))
    col_ref[...] = jnp.where(lane < HEADS, _sigmoid(ab), g_col)
    uv = _gelu_tanh(_dot(h, w_ref[:, COL_U:COL_U + 2 * D_SGU]))
    vg = _layer_norm(uv[:, D_SGU:], lng_ref[...], lnb_ref[...])
    vg_ref[:, 0, :] = vg
    for hh in range(HEADS):
        lo = hh * DH
        mix = ws_ref[hh, 0:1, 0:1] * vg[:, lo:lo + DH] + bs_ref[hh:hh + 1, 0:1]
        sg_ref[:, lo:lo + DH] = (uv[:, lo:lo + DH] * mix).astype(BF16)


def _sample_proj_call(x, hist, w_all, wconv, alog, dtb, lng, lnb, ws, bs):
    n = x.shape[0]
    ins = [x, hist, w_all, wconv, alog, dtb, lng, lnb, ws, bs]
    vmem, smem = pl.BlockSpec(memory_space=pltpu.VMEM), pl.BlockSpec(memory_space=pltpu.SMEM)
    out_shape = [
        jax.ShapeDtypeStruct((CONV_W - 1, n, D_QKV), F32),
        jax.ShapeDtypeStruct((n, 1, D_SGU), F32),
        jax.ShapeDtypeStruct((n, D_SGU), BF16),
        jax.ShapeDtypeStruct((n, D_GDN), F32),
        jax.ShapeDtypeStruct((n, D_GDN), F32),
        jax.ShapeDtypeStruct((n, LANES), F32),
        jax.ShapeDtypeStruct((HEADS, DH, n), F32),
        jax.ShapeDtypeStruct((HEADS, DH, n), F32),
    ]
    return pl.pallas_call(
        _sample_proj_kernel,
        in_specs=[smem if a is alog or a is dtb else vmem for a in ins],
        out_shape=out_shape,
        compiler_params=pltpu.CompilerParams(vmem_limit_bytes=VMEM_LIMIT),
        name="sample_proj",
    )(*ins)


def _sample_delta_step(upper, lane0, s_ref, qt_ref, kt_ref, v_ref, col_ref, zs_ref, gnorm_ref, og_ref, snew_ref):
    gnorm = gnorm_ref[...]
    row0 = jnp.where(upper, SAMPLE_HALF, 0)
    pick = lambda blk, bb: jnp.where(upper, blk[SAMPLE_HALF + bb:SAMPLE_HALF + bb + 1], blk[bb:bb + 1])
    col_blk, v_blk, zs_blk = col_ref[...], v_ref[...], zs_ref[...]
    pairs = [(bb, hh) for bb in range(SAMPLE_HALF) for hh in range(HEADS)]
    n_seq = kt_ref.shape[-1]
    k_here = [pltpu.roll(kt_ref[hh], n_seq - lane0, axis=1) for hh in range(HEADS)]
    q_here = [pltpu.roll(qt_ref[hh], n_seq - lane0, axis=1) for hh in range(HEADS)]
    kcol = [k_here[hh][:, bb:bb + 1] for bb, hh in pairs]
    qcol = [q_here[hh][:, bb:bb + 1] for bb, hh in pairs]
    beta = [pick(col_blk[:, hh:hh + 1], bb) for bb, hh in pairs]
    eg = [jnp.exp(pick(col_blk[:, HEADS + hh:HEADS + hh + 1], bb)) for bb, hh in pairs]
    sd = [s_ref[row0 + bb, hh] * e for (bb, hh), e in zip(pairs, eg)]
    sk = [jnp.sum(s * kc, axis=0, keepdims=True) for s, kc in zip(sd, kcol)]
    v_new = [b * (pick(v_blk[:, hh * DH:(hh + 1) * DH], bb) - x) for (bb, hh), b, x in zip(pairs, beta, sk)]
    s1 = [s + kc * vn for s, kc, vn in zip(sd, kcol, v_new)]
    for (bb, hh), s in zip(pairs, s1):
        snew_ref[row0 + bb, hh] = s
    o = [jnp.sum(s * qc, axis=0, keepdims=True) for s, qc in zip(s1, qcol)]
    sub = lax.broadcasted_iota(jnp.int32, (SAMPLE_BLOCK, DH), 0)
    for hh in range(HEADS):
        lo = hh * DH
        merged = og_ref[:, lo:lo + DH]
        for bb in range(SAMPLE_HALF):
            og = _rms(o[bb * HEADS + hh], gnorm) * pick(zs_blk[:, lo:lo + DH], bb)
            merged = jnp.where(sub == row0 + bb, og.astype(og_ref.dtype), merged)
        og_ref[:, lo:lo + DH] = merged


def kernel(x_prompt, x_sample, state_conv, state_gdn, p_prompt, p_sample, g_mix, w_in, w_conv, a_log, dt_bias, gdn_norm, sgu_ln_g, sgu_ln_b, w_s, b_s, w_out, g_ff, w_up, w_down, g_ple, w_ple, w_ple_gate, g_final):
    depth = w_in.shape[0]
    assert depth == 1, "single-layer trunk only"
    b, l, d = x_prompt.shape
    n = x_sample.shape[0]
    assert x_sample.shape[1] == 1 and d == D_MODEL
    assert l % PROJ_ROWS == 0 and l % GDN_ROWS == 0 and (b * l) % TAIL_ROWS == 0 and n % SAMPLE_BLOCK == 0

    row2 = lambda a: a.reshape(1, -1)
    w_all, wabt = _prep_call(w_in, row2(g_mix[0]))
    alog, dtb = a_log[0], dt_bias[0]
    gff, gple, gfin = row2(g_ff[0]), row2(g_ple[0]), row2(g_final)
    lng, lnb, gnorm = row2(sgu_ln_g[0]), row2(sgu_ln_b[0]), row2(gdn_norm[0])

    qs, k, v, zs, sg, col, row, conv_p, wout, wup, wdown, wple, wgate = _proj_call(
        x_prompt, w_all, wabt, w_conv[0], alog, dtb,
        lng, lnb, w_s[0], b_s[0].T, (w_out, w_up, w_down, w_ple, w_ple_gate))
    tail_w = (wout, gff, wup, wdown, gple, wple, wgate, gfin)
    og, gdn_p = _gdn_call(qs, k, v, col, row, zs, gnorm)

    nconv, vg_s, sg_s, zs_s, v_s, col_s, q_s, kt = _sample_proj_call(
        x_sample, jnp.transpose(state_conv[0], (1, 0, 2)), w_all,
        w_conv[0], alog, dtb, lng, lnb, w_s[0], b_s[0])

    y_prompt, y_sample, gdn_s = _tail_call(
        x_prompt.reshape(b * l, d), og.reshape(b * l, D_GDN), sg.reshape(b * l, D_SGU),
        p_prompt[0].reshape(b * l, D_PLE),
        x_sample, sg_s, p_sample[0],
        state_gdn[0], q_s, kt, v_s, col_s, zs_s, gnorm, *tail_w)
    y_prompt = y_prompt.reshape(b, l, d)

    return (y_prompt, y_sample,
            conv_p[None], gdn_p[None],
            jnp.transpose(nconv, (1, 0, 2))[None], gdn_s[None],
            vg_s[None])
```
